```python
import math
import jax, jax.numpy as jnp
from jax import lax
import numpy as np

D_MODEL = 1024
BATCH = 16
SEQ = 2048
DEPTH = 4

N_MIXERS = 3
NORM_EPS = 1e-6
ROPE_THETA = 500000.0
NEG_INF = -1e30
POS_OFFSET_MAX = 4096
DILATED_PAIRS = ((128, 1), (512, 4), (2048, 16))
A_GROUPS = len(DILATED_PAIRS)
A_HEAD_DIM = 128
A_HEADS = D_MODEL // A_HEAD_DIM
ROT_DIM = A_HEAD_DIM // 4
B_HEAD_DIM = 64
B_HEADS = D_MODEL // B_HEAD_DIM
B_DECAY_LORA = 64
B_ICLR_LORA = 64
B_GATE_LORA = 128
B_N_MU = 8
B_DECAY_SCALE = math.exp(-0.5)
B_GN_EPS = 64e-5
C_HEADS = D_MODEL // 64
C_Q_LORA = 384
C_KV_LORA = 128
C_NOPE = 64
C_ROPE = ROT_DIM
C_V = 64
Q_BLOCK = 128
D_FF = 2816
N_EXPERTS = 8
TOP_K = 2
D_FF_EXPERT = 3584

kernel_name = "hybrid_dilated_rwkv7_mla_moe_encoder"


def _count_layers(m, period):
    return (DEPTH - m + period - 1) // period


def rms_norm(x, g, eps=NORM_EPS):
    xf = x.astype(jnp.float32)
    y = xf * lax.rsqrt(jnp.mean(xf * xf, axis=-1, keepdims=True) + eps)
    return (y * g.astype(jnp.float32)).astype(x.dtype)


def rope_tables(positions, rot):
    inv_freq = ROPE_THETA ** (-jnp.arange(0, rot, 2, dtype=jnp.float32) / rot)
    ang = positions.astype(jnp.float32)[..., None] * inv_freq
    return jnp.cos(ang), jnp.sin(ang)


def apply_rope(x, cos, sin):
    half = cos.shape[-1]
    bshape = cos.shape[:2] + (1,) * (x.ndim - 3) + (half,)
    c = cos.reshape(bshape)
    s = sin.reshape(bshape)
    xr = x[..., :2 * half].astype(jnp.float32)
    x1, x2 = xr[..., :half], xr[..., half:]
    rot = jnp.concatenate([x1 * c - x2 * s, x2 * c + x1 * s], axis=-1).astype(x.dtype)
    return jnp.concatenate([rot, x[..., 2 * half:]], axis=-1)


def swiglu(h, w13, w2):
    gate, up = jnp.split(h @ w13, 2, axis=-1)
    return (jax.nn.silu(gate) * up) @ w2


def banded_attention(q, k, v, half):
    lead = q.shape[:-2]
    L, hd = q.shape[-2:]
    blk = half
    nb = -(-L // blk)
    lp = nb * blk
    nl = len(lead)
    qb = jnp.pad(q.astype(jnp.float32), [(0, 0)] * nl + [(0, lp - L), (0, 0)]).reshape(lead + (nb, blk, hd))

    def windows(t):
        tb = jnp.pad(t.astype(jnp.float32), [(0, 0)] * nl + [(blk, lp - L + blk), (0, 0)])
        tb = tb.reshape(lead + (nb + 2, blk, hd))
        return jnp.concatenate([tb[..., 0:nb, :, :], tb[..., 1:nb + 1, :, :], tb[..., 2:nb + 2, :, :]], axis=-2)

    kw = windows(k)
    vw = windows(v)
    qi = np.arange(nb)[:, None, None] * blk + np.arange(blk)[None, :, None]
    kj = (np.arange(nb)[:, None, None] - 1) * blk + np.arange(3 * blk)[None, None, :]
    mask = (np.abs(qi - kj) <= half) & (kj >= 0) & (kj < L)
    s = jnp.einsum('...nqd,...nkd->...nqk', qb, kw) * (hd ** -0.5)
    s = jnp.where(mask, s, NEG_INF)
    m = jnp.max(s, axis=-1, keepdims=True)
    p = jnp.exp(s - m)
    l = jnp.sum(p, axis=-1, keepdims=True)
    o = jnp.einsum('...nqk,...nkd->...nqd', p, vw) / l
    lse = (m + jnp.log(l))[..., 0]
    o = o.reshape(lead + (lp, hd))[..., :L, :]
    lse = lse.reshape(lead + (lp,))[..., :L]
    return o, lse


def dilated_attention(h, wqkv, qk_norm, wo, cos, sin):
    B, S, _ = h.shape
    qkv = (h @ wqkv).reshape(B, S, 3, A_GROUPS, A_HEADS, A_HEAD_DIM)
    q = apply_rope(rms_norm(qkv[:, :, 0], qk_norm[0][:, None, :]), cos, sin)
    k = apply_rope(rms_norm(qkv[:, :, 1], qk_norm[1][:, None, :]), cos, sin)
    v = qkv[:, :, 2]
    outs, lses = [], []
    for g, (window, dil) in enumerate(DILATED_PAIRS):
        half = window // (2 * dil)
        L = S // dil

        def to_sub(t):
            return t[:, :, g].reshape(B, L, dil, A_HEADS, A_HEAD_DIM).transpose(0, 2, 3, 1, 4)

        o, lse = banded_attention(to_sub(q), to_sub(k), to_sub(v), half)
        outs.append(o.transpose(0, 3, 1, 2, 4).reshape(B, S, A_HEADS, A_HEAD_DIM))
        lses.append(lse.transpose(0, 3, 1, 2).reshape(B, S, A_HEADS))
    alpha = jax.nn.softmax(jnp.stack(lses), axis=0)
    o = jnp.einsum('gbsh,gbshd->bshd', alpha, jnp.stack(outs))
    return o.reshape(B, S, A_HEADS * A_HEAD_DIM).astype(h.dtype) @ wo


def wkv_scan(r, w, k, v, a, b, reverse):
    B, S, H, N = r.shape

    def step(state, inp):
        r_t, w_t, k_t, v_t, a_t, b_t = inp
        sa = jnp.einsum('bhvk,bhk->bhv', state, a_t)
        state = state * w_t[:, :, None, :] + sa[..., None] * b_t[:, :, None, :] + v_t[..., None] * k_t[:, :, None, :]
        return state, jnp.einsum('bhvk,bhk->bhv', state, r_t)

    xs = tuple(jnp.moveaxis(t, 1, 0) for t in (r, w, k, v, a, b))
    _, ys = lax.scan(step, jnp.zeros((B, H, N, N), jnp.float32), xs, reverse=reverse)
    return jnp.moveaxis(ys, 0, 1)


def _heads(t):
    return t.reshape(t.shape[:-1] + (B_HEADS, B_HEAD_DIM))


def rwkv7_bidirectional(h, mu, w_rkv, w0, w1, w2, a0, a1, a2, g1, g2, k_k, k_a, r_k, ln_x, wo):
    B, S, D = h.shape
    f32 = jnp.float32
    zero = jnp.zeros_like(h[:, :1])
    d_f = jnp.concatenate([zero, h[:, :-1]], axis=1) - h
    d_b = jnp.concatenate([h[:, 1:], zero], axis=1) - h
    d_c = 0.5 * (d_f + d_b)
    shared = h[None] + d_c[None] * mu[:4, None, None, :]
    r, k, v = jnp.einsum('nbsd,nde->nbse', shared[:3], w_rkv)
    g = jax.nn.sigmoid(shared[3] @ g1) @ g2
    dirs = jnp.stack([d_f, d_b])
    x_w = h[None] + dirs * mu[4:6, None, None, :]
    x_a = h[None] + dirs * mu[6:8, None, None, :]
    z = w0[:, None, None, :] + jnp.einsum('nbsr,nrd->nbsd', jnp.tanh(jnp.einsum('nbsd,ndr->nbsr', x_w, w1)), w2)
    decay = _heads(jnp.exp(-B_DECAY_SCALE * jax.nn.sigmoid(z.astype(f32))))
    a = _heads(jax.nn.sigmoid((a0[:, None, None, :] + jnp.einsum('nbsr,nrd->nbsd', jnp.einsum('nbsd,ndr->nbsr', x_a, a1), a2)).astype(f32)))
    rf = _heads(r.astype(f32))
    kf = _heads(k.astype(f32))
    vf = _heads(v.astype(f32))
    kk = kf * _heads(k_k.astype(f32))
    kk = kk / jnp.maximum(jnp.sqrt(jnp.sum(kk * kk, axis=-1, keepdims=True)), 1e-12)
    k_dir = kf[None] * (1.0 + (a - 1.0) * _heads(k_a.astype(f32)))
    y_f = wkv_scan(rf, decay[0], k_dir[0], vf, -kk, kk * a[0], reverse=False)
    y_b = wkv_scan(rf, decay[1], k_dir[1], vf, -kk, kk * a[1], reverse=True)
    y = y_f + y_b
    mean = jnp.mean(y, axis=-1, keepdims=True)
    var = jnp.mean(jnp.square(y - mean), axis=-1, keepdims=True)
    y = ((y - mean) * lax.rsqrt(var + B_GN_EPS)).reshape(B, S, D) * ln_x[0].astype(f32) + ln_x[1].astype(f32)
    bonus = jnp.sum(rf[None] * k_dir * _heads(r_k.astype(f32)), axis=(0, -1))[..., None] * vf
    out = (y + bonus.reshape(B, S, D)) * g.astype(f32)
    return out.astype(h.dtype) @ wo


def dense_attention(q, k, v):
    B, S, H, dk = q.shape
    nb = S // Q_BLOCK
    scale = dk ** -0.5
    qb = q.astype(jnp.float32).reshape(B, nb, Q_BLOCK, H, dk).transpose(1, 0, 3, 2, 4)
    kt = k.astype(jnp.float32).transpose(0, 2, 1, 3)
    vt = v.astype(jnp.float32).transpose(0, 2, 1, 3)

    def block(qblk):
        p = jax.nn.softmax(jnp.einsum('bhqd,bhkd->bhqk', qblk, kt) * scale, axis=-1)
        return jnp.einsum('bhqk,bhkd->bhqd', p, vt)

    o = lax.map(block, qb)
    return o.transpose(1, 0, 3, 2, 4).reshape(B, S, H, v.shape[-1])


def mla(h, wa, qa_norm, kva_norm, wq_b, wkv_b, qk_norm, wo, cos, sin):
    B, S, _ = h.shape
    lat = h @ wa
    q_lat, kv_lat, k_rope = jnp.split(lat, [C_Q_LORA, C_Q_LORA + C_KV_LORA], axis=-1)
    q = (rms_norm(q_lat, qa_norm) @ wq_b).reshape(B, S, C_HEADS, C_ROPE + C_NOPE)
    kv = (rms_norm(kv_lat, kva_norm) @ wkv_b).reshape(B, S, C_HEADS, C_NOPE + C_V)
    k_nope, v = kv[..., :C_NOPE], kv[..., C_NOPE:]
    k = jnp.concatenate([jnp.broadcast_to(k_rope[:, :, None, :], (B, S, C_HEADS, C_ROPE)), k_nope], axis=-1)
    q = apply_rope(rms_norm(q, qk_norm[0]), cos, sin)
    k = apply_rope(rms_norm(k, qk_norm[1]), cos, sin)
    o = dense_attention(q, k, v)
    return o.reshape(B, S, C_HEADS * C_V).astype(h.dtype) @ wo


def moe_swiglu(h, router, w13, w2):
    B, S, D = h.shape
    t = h.reshape(B * S, D)
    logits = (t @ router).astype(jnp.float32)
    top_val, top_idx = lax.top_k(logits, TOP_K)
    gate = jax.nn.softmax(top_val, axis=-1)
    combine = jnp.einsum('tk,tke->te', gate, jax.nn.one_hot(top_idx, N_EXPERTS, dtype=jnp.float32))
    out = jnp.zeros_like(t)
    for e in range(N_EXPERTS):
        out = out + combine[:, e:e + 1].astype(t.dtype) * swiglu(t, w13[e], w2[e])
    return out.reshape(B, S, D)


def setup_inputs(seed: int = 0) -> dict:
    key = jax.random.key(seed)
    keys = iter(jax.random.split(key, 64))
    f32 = jnp.float32
    D = D_MODEL

    def nrm(shape, scale):
        return jax.random.normal(next(keys), shape, f32) * scale

    def gain(shape):
        return 1.0 + nrm(shape, 0.02)

    n_a = _count_layers(0, N_MIXERS)
    n_b = _count_layers(1, N_MIXERS)
    n_c = _count_layers(2, N_MIXERS)
    n_dense = _count_layers(0, 2)
    n_moe = _count_layers(1, 2)
    a_width = A_GROUPS * A_HEADS * A_HEAD_DIM
    x = nrm((BATCH, SEQ, D), 1.0)
    positions = jnp.arange(SEQ, dtype=jnp.int32)[None, :] + jax.random.randint(next(keys), (BATCH, 1), 0, POS_OFFSET_MAX, dtype=jnp.int32)
    return {
        "x": x,
        "positions": positions,
        "ln_mix": gain((DEPTH, D)),
        "ln_ffn": gain((DEPTH, D)),
        "a_wqkv": nrm((n_a, D, 3 * a_width), D ** -0.5),
        "a_qk_norm": gain((n_a, 2, A_GROUPS, A_HEAD_DIM)),
        "a_wo": nrm((n_a, A_HEADS * A_HEAD_DIM, D), (A_HEADS * A_HEAD_DIM) ** -0.5),
        "b_mu": jax.random.uniform(next(keys), (n_b, B_N_MU, D), f32),
        "b_wrkv": nrm((n_b, 3, D, D), D ** -0.5),
        "b_w0": nrm((n_b, 2, D), 1.0),
        "b_w1": nrm((n_b, 2, D, B_DECAY_LORA), D ** -0.5),
        "b_w2": nrm((n_b, 2, B_DECAY_LORA, D), B_DECAY_LORA ** -0.5),
        "b_a0": nrm((n_b, 2, D), 0.5),
        "b_a1": nrm((n_b, 2, D, B_ICLR_LORA), D ** -0.5),
        "b_a2": nrm((n_b, 2, B_ICLR_LORA, D), 0.5 * B_ICLR_LORA ** -0.5),
        "b_g1": nrm((n_b, D, B_GATE_LORA), D ** -0.5),
        "b_g2": nrm((n_b, B_GATE_LORA, D), B_GATE_LORA ** -0.5),
        "b_kk": 0.85 + nrm((n_b, D), 0.05),
        "b_ka": 1.0 + nrm((n_b, D), 0.05),
        "b_rk": nrm((n_b, D), 0.1),
        "b_lnx": jnp.stack([gain((n_b, D)), nrm((n_b, D), 0.02)], axis=1),
        "b_wo": nrm((n_b, D, D), D ** -0.5),
        "c_wa": nrm((n_c, D, C_Q_LORA + C_KV_LORA + C_ROPE), D ** -0.5),
        "c_qa_norm": gain((n_c, C_Q_LORA)),
        "c_kva_norm": gain((n_c, C_KV_LORA)),
        "c_wq_b": nrm((n_c, C_Q_LORA, C_HEADS * (C_ROPE + C_NOPE)), C_Q_LORA ** -0.5),
        "c_wkv_b": nrm((n_c, C_KV_LORA, C_HEADS * (C_NOPE + C_V)), C_KV_LORA ** -0.5),
        "c_qk_norm": gain((n_c, 2, C_ROPE + C_NOPE)),
        "c_wo": nrm((n_c, C_HEADS * C_V, D), (C_HEADS * C_V) ** -0.5),
        "f_w13": nrm((n_dense, D, 2 * D_FF), D ** -0.5),
        "f_w2": nrm((n_dense, D_FF, D), D_FF ** -0.5),
        "m_router": nrm((n_moe, D, N_EXPERTS), D ** -0.5),
        "m_w13": nrm((n_moe, N_EXPERTS, D, 2 * D_FF_EXPERT), D ** -0.5),
        "m_w2": nrm((n_moe, N_EXPERTS, D_FF_EXPERT, D), D_FF_EXPERT ** -0.5),
    }


def reference(x, positions, ln_mix, ln_ffn, a_wqkv, a_qk_norm, a_wo, b_mu, b_wrkv, b_w0, b_w1, b_w2, b_a0, b_a1, b_a2, b_g1, b_g2, b_kk, b_ka, b_rk, b_lnx, b_wo, c_wa, c_qa_norm, c_kva_norm, c_wq_b, c_wkv_b, c_qk_norm, c_wo, f_w13, f_w2, m_router, m_w13, m_w2):
    cos, sin = rope_tables(positions, ROT_DIM)
    for i in range(DEPTH):
        j = i // N_MIXERS
        kind = i % N_MIXERS
        h = rms_norm(x, ln_mix[i])
        if kind == 0:
            y = dilated_attention(h, a_wqkv[j], a_qk_norm[j], a_wo[j], cos, sin)
        elif kind == 1:
            y = rwkv7_bidirectional(h, b_mu[j], b_wrkv[j], b_w0[j], b_w1[j], b_w2[j], b_a0[j], b_a1[j], b_a2[j], b_g1[j], b_g2[j], b_kk[j], b_ka[j], b_rk[j], b_lnx[j], b_wo[j])
        else:
            y = mla(h, c_wa[j], c_qa_norm[j], c_kva_norm[j], c_wq_b[j], c_wkv_b[j], c_qk_norm[j], c_wo[j], cos, sin)
        x = x + y
        h = rms_norm(x, ln_ffn[i])
        f = i // 2
        if i % 2 == 0:
            x = x + swiglu(h, f_w13[f], f_w2[f])
        else:
            x = x + moe_swiglu(h, m_router[f], m_w13[f], m_w2[f])
    return x
```

```python
import functools
import math

import jax
import jax.numpy as jnp
from jax import lax
from jax.experimental import pallas as pl
from jax.experimental.pallas import tpu as pltpu

F32 = jnp.float32
BF16 = jnp.bfloat16
I32 = jnp.int32

NORM_EPS = 1e-6
ROPE_THETA = 500000.0
NEG_INF = -1e30
LANES = 128
ROT_DIM = 32
ROT_HALF = ROT_DIM // 2
DILATED_PAIRS = ((128, 1), (512, 4), (2048, 16))
A_HEAD_DIM = 128
VMEM_LIMIT = 48 * 1024 * 1024


def _params(*sem):
    return pltpu.CompilerParams(dimension_semantics=sem, vmem_limit_bytes=VMEM_LIMIT)


def _rms(x, g):
    return x * lax.rsqrt(jnp.mean(x * x, axis=-1, keepdims=True) + NORM_EPS) * g


def _row_tile(t, want):
    tm = min(t, want)
    assert t % tm == 0
    return tm


def _norm_matmul_kernel(x_ref, g_ref, w_ref, o_ref, xn_ref):
    @pl.when(pl.program_id(1) == 0)
    def _():
        xn_ref[...] = _rms(x_ref[...], g_ref[...]).astype(BF16)

    o_ref[...] = jnp.dot(xn_ref[...], w_ref[...], preferred_element_type=F32).astype(o_ref.dtype)


def norm_matmul(x, g, w, out_dtype, tn):
    t, k = x.shape
    n = w.shape[1]
    tm = _row_tile(t, 1024)
    assert n % tn == 0
    return pl.pallas_call(
        _norm_matmul_kernel,
        grid=(t // tm, n // tn),
        in_specs=[pl.BlockSpec((tm, k), lambda i, j: (i, 0)),
                  pl.BlockSpec((1, k), lambda i, j: (0, 0)),
                  pl.BlockSpec((k, tn), lambda i, j: (0, j))],
        out_specs=pl.BlockSpec((tm, tn), lambda i, j: (i, j)),
        out_shape=jax.ShapeDtypeStruct((t, n), out_dtype),
        scratch_shapes=[pltpu.VMEM((tm, k), BF16)],
        compiler_params=_params("parallel", "arbitrary"),
        name="norm_matmul",
    )(x, g.reshape(1, k), w)


def _matmul_res_kernel(a_ref, w_ref, x_ref, o_ref):
    o_ref[...] = x_ref[...] + jnp.dot(a_ref[...], w_ref[...], preferred_element_type=F32)


def matmul_residual(a, w, x):
    t, k = a.shape
    n = w.shape[1]
    tm = _row_tile(t, 1024)
    return pl.pallas_call(
        _matmul_res_kernel,
        grid=(t // tm,),
        in_specs=[pl.BlockSpec((tm, k), lambda i: (i, 0)),
                  pl.BlockSpec((k, n), lambda i: (0, 0)),
                  pl.BlockSpec((tm, n), lambda i: (i, 0))],
        out_specs=pl.BlockSpec((tm, n), lambda i: (i, 0)),
        out_shape=jax.ShapeDtypeStruct((t, n), F32),
        compiler_params=_params("parallel"),
        name="matmul_residual",
    )(a, w, x)


def _rope_tables(positions):
    b, s = positions.shape
    inv_freq = ROPE_THETA ** (-jnp.arange(0, ROT_DIM, 2, dtype=F32) / ROT_DIM)
    ang = positions.astype(F32)[..., None] * inv_freq
    c, sn = jnp.cos(ang), jnp.sin(ang)
    rest = (b, s, LANES - ROT_DIM)
    cs_t = jnp.concatenate([c, c, jnp.ones(rest, F32)], axis=-1)
    sn_t = jnp.concatenate([-sn, sn, jnp.zeros(rest, F32)], axis=-1)
    return cs_t, sn_t


def _rope(x, cs, sn, lane):
    n = x.shape[-1]
    partner = jnp.where(lane < ROT_HALF, pltpu.roll(x, n - ROT_HALF, 1), pltpu.roll(x, ROT_HALF, 1))
    return x * cs + partner * sn


def _dil_attn_kernel(q_ref, k_ref, v_ref, cs_ref, sn_ref, gq_ref, gk_ref, o_ref, lse_ref, kn_ref,
                     *, seq, qt, kw, half, heads):
    hd = A_HEAD_DIM
    qi = pl.program_id(2)
    lane = lax.broadcasted_iota(I32, (1, hd), 1)

    @pl.when(qi == 0)
    def _():
        cs, sn = cs_ref[0], sn_ref[0]
        for h in range(heads):
            kh = k_ref[0, :, h * hd:(h + 1) * hd].astype(F32)
            kn_ref[:, h * hd:(h + 1) * hd] = _rope(_rms(kh, gk_ref[...]), cs, sn, lane).astype(BF16)

    q0 = pl.multiple_of(qi * qt, qt)
    start = jnp.clip(q0 - half, 0, seq - kw)
    start = pl.multiple_of(start, math.gcd(half, qt))
    cs_q = cs_ref[0, pl.ds(q0, qt), :]
    sn_q = sn_ref[0, pl.ds(q0, qt), :]
    qpos = q0 + lax.broadcasted_iota(I32, (qt, kw), 0)
    kpos = start + lax.broadcasted_iota(I32, (qt, kw), 1)
    valid = jnp.abs(qpos - kpos) <= half
    head_of_lane = lax.broadcasted_iota(I32, (1, LANES), 1) // (LANES // heads)
    lse_row = jnp.zeros((qt, LANES), F32)
    scale = hd ** -0.5
    for h in range(heads):
        sl = slice(h * hd, (h + 1) * hd)
        qh = _rope(_rms(q_ref[0, :, sl].astype(F32), gq_ref[...]), cs_q, sn_q, lane) * scale
        kwin = kn_ref[pl.ds(start, kw), sl]
        s = lax.dot_general(qh.astype(BF16), kwin, (((1,), (1,)), ((), ())), preferred_element_type=F32)
        s = jnp.where(valid, s, NEG_INF)
        m = jnp.max(s, axis=-1, keepdims=True)
        p = jnp.exp(s - m)
        l = jnp.sum(p, axis=-1, keepdims=True)
        vwin = v_ref[0, pl.ds(start, kw), sl]
        o = jnp.dot(p.astype(BF16), vwin, preferred_element_type=F32) / l
        o_ref[0, :, sl] = o.astype(o_ref.dtype)
        lse_row = jnp.where(head_of_lane == h, m + jnp.log(l), lse_row)
    lse_ref[0] = lse_row


def dilated_group_attention(qkv, cs, sn, gq, gk, group, batch, s_len, heads):
    window, dil = DILATED_PAIRS[group]
    n_groups = len(DILATED_PAIRS)
    half = window // (2 * dil)
    seq = s_len // dil
    width = heads * A_HEAD_DIM
    qt = min(128, seq)
    kw = min(qt + 2 * half, seq)
    assert seq % qt == 0 and (seq - kw) % math.gcd(half, qt) == 0
    blocks_per_token = 3 * n_groups
    qkv_v = qkv.reshape(batch, seq, dil * blocks_per_token * width)
    cs_v = cs.reshape(batch, seq, dil * LANES)
    sn_v = sn.reshape(batch, seq, dil * LANES)
    col = lambda kind: (lambda b, r, qi: (b, 0 if kind else qi, r * blocks_per_token + kind * n_groups + group))
    kern = functools.partial(_dil_attn_kernel, seq=seq, qt=qt, kw=kw, half=half, heads=heads)
    o, lse = pl.pallas_call(
        kern,
        grid=(batch, dil, seq // qt),
        in_specs=[pl.BlockSpec((1, qt, width), col(0)),
                  pl.BlockSpec((1, seq, width), col(1)),
                  pl.BlockSpec((1, seq, width), col(2)),
                  pl.BlockSpec((1, seq, LANES), lambda b, r, qi: (b, 0, r)),
                  pl.BlockSpec((1, seq, LANES), lambda b, r, qi: (b, 0, r)),
                  pl.BlockSpec((1, A_HEAD_DIM), lambda b, r, qi: (0, 0)),
                  pl.BlockSpec((1, A_HEAD_DIM), lambda b, r, qi: (0, 0))],
        out_specs=[pl.BlockSpec((1, qt, width), lambda b, r, qi: (b, qi, r)),
                   pl.BlockSpec((1, qt, LANES), lambda b, r, qi: (b, qi, r))],
        out_shape=[jax.ShapeDtypeStruct((batch, seq, dil * width), BF16),
                   jax.ShapeDtypeStruct((batch, seq, dil * LANES), F32)],
        scratch_shapes=[pltpu.VMEM((seq, width), BF16)],
        compiler_params=_params("parallel", "parallel", "arbitrary"),
        name=f"dilated_attention_g{group}",
    )(qkv_v, qkv_v, qkv_v, cs_v, sn_v, gq.reshape(1, -1), gk.reshape(1, -1))
    return o.reshape(batch * s_len, width), lse.reshape(batch * s_len, LANES)


def _combine_wo_kernel(x_ref, o0_ref, o1_ref, o2_ref, l0_ref, l1_ref, l2_ref, w_ref, out_ref, *, heads):
    hd = A_HEAD_DIM
    rep = LANES // heads
    l0, l1, l2 = l0_ref[...], l1_ref[...], l2_ref[...]
    m = jnp.maximum(jnp.maximum(l0, l1), l2)
    e0, e1, e2 = jnp.exp(l0 - m), jnp.exp(l1 - m), jnp.exp(l2 - m)
    inv = 1.0 / (e0 + e1 + e2)
    a0, a1, a2 = e0 * inv, e1 * inv, e2 * inv
    parts = []
    for h in range(heads):
        sl = slice(h * hd, (h + 1) * hd)
        c = slice(h * rep, h * rep + 1)
        parts.append((a0[:, c] * o0_ref[:, sl].astype(F32) + a1[:, c] * o1_ref[:, sl].astype(F32)
                      + a2[:, c] * o2_ref[:, sl].astype(F32)).astype(BF16))
    o = jnp.concatenate(parts, axis=-1)
    out_ref[...] = x_ref[...] + jnp.dot(o, w_ref[...], preferred_element_type=F32)


def combine_wo(x, outs, lses, wo, heads):
    t, d = x.shape
    width = wo.shape[0]
    tm = _row_tile(t, 512)
    row = lambda n: pl.BlockSpec((tm, n), lambda i: (i, 0))
    return pl.pallas_call(
        functools.partial(_combine_wo_kernel, heads=heads),
        grid=(t // tm,),
        in_specs=[row(d), row(width), row(width), row(width), row(LANES), row(LANES), row(LANES),
                  pl.BlockSpec((width, d), lambda i: (0, 0))],
        out_specs=row(d),
        out_shape=jax.ShapeDtypeStruct((t, d), F32),
        compiler_params=_params("parallel"),
        name="combine_wo",
    )(x, *outs, *lses, wo)


def mixer_a(x, ln, wqkv, qk_norm, wo, cs, sn, batch, s_len):
    heads = wo.shape[0] // A_HEAD_DIM
    qkv = norm_matmul(x, ln, wqkv.astype(BF16), BF16, tn=1024)
    outs, lses = [], []
    for g in range(len(DILATED_PAIRS)):
        o, lse = dilated_group_attention(qkv, cs, sn, qk_norm[0, g], qk_norm[1, g], g, batch, s_len, heads)
        outs.append(o)
        lses.append(lse)
    return combine_wo(x, outs, lses, wo.astype(BF16), heads)


def _ffn_kernel(x_ref, g_ref, w1_ref, w3_ref, w2_ref, o_ref, xn_ref, acc_ref):
    j = pl.program_id(1)

    @pl.when(j == 0)
    def _():
        xn_ref[...] = _rms(x_ref[...], g_ref[...]).astype(BF16)
        acc_ref[...] = jnp.zeros_like(acc_ref)

    xn = xn_ref[...]
    gate = jnp.dot(xn, w1_ref[...], preferred_element_type=F32)
    up = jnp.dot(xn, w3_ref[...], preferred_element_type=F32)
    act = (gate * jax.nn.sigmoid(gate) * up).astype(BF16)
    acc_ref[...] += jnp.dot(act, w2_ref[...], preferred_element_type=F32)

    @pl.when(j == pl.num_programs(1) - 1)
    def _():
        o_ref[...] = x_ref[...] + acc_ref[...]


def dense_ffn(x, ln, w13, w2, tf=256):
    t, d = x.shape
    dff = w2.shape[0]
    assert dff % tf == 0
    nj = dff // tf
    tm = _row_tile(t, 1024)
    return pl.pallas_call(
        _ffn_kernel,
        grid=(t // tm, nj),
        in_specs=[pl.BlockSpec((tm, d), lambda i, j: (i, 0)),
                  pl.BlockSpec((1, d), lambda i, j: (0, 0)),
                  pl.BlockSpec((d, tf), lambda i, j: (0, j)),
                  pl.BlockSpec((d, tf), lambda i, j: (0, nj + j)),
                  pl.BlockSpec((tf, d), lambda i, j: (j, 0))],
        out_specs=pl.BlockSpec((tm, d), lambda i, j: (i, 0)),
        out_shape=jax.ShapeDtypeStruct((t, d), F32),
        scratch_shapes=[pltpu.VMEM((tm, d), BF16), pltpu.VMEM((tm, d), F32)],
        compiler_params=_params("parallel", "arbitrary"),
        name="dense_ffn",
    )(x, ln.reshape(1, d), w13, w13, w2)


C_ROPE = ROT_DIM
C_NOPE = 64
C_V = 64
C_QK = C_ROPE + C_NOPE


def _mla_qkv_kernel(lat_ref, qa_ref, kva_ref, gq_ref, gk_ref, wq_ref, wkv_ref, cs_ref, sn_ref,
                    q_ref, k_ref, v_ref, *, heads, q_lora, kv_lora):
    lane = lax.broadcasted_iota(I32, (1, LANES), 1)
    cs, sn = cs_ref[...], sn_ref[...]
    lat = lat_ref[...]
    qn = _rms(lat[:, :q_lora], qa_ref[...]).astype(BF16)
    kvn = _rms(lat[:, q_lora:q_lora + kv_lora], kva_ref[...]).astype(BF16)
    k_rope = lat[:, q_lora + kv_lora:]
    q = jnp.dot(qn, wq_ref[...], preferred_element_type=F32)
    kv = jnp.dot(kvn, wkv_ref[...], preferred_element_type=F32)
    scale = C_QK ** -0.5

    def head_norm(xh, g):
        ms = jnp.sum(xh * xh, axis=-1, keepdims=True) * (1.0 / C_QK)
        return _rope(xh * lax.rsqrt(ms + NORM_EPS) * g, cs, sn, lane)

    for h in range(heads):
        sl = slice(h * LANES, (h + 1) * LANES)
        q_ref[:, sl] = (head_norm(q[:, sl], gq_ref[...]) * scale).astype(BF16)
        k_ref[:, sl] = head_norm(kv[:, sl] + k_rope, gk_ref[...]).astype(BF16)
    v_ref[...] = kv[:, heads * LANES:].astype(BF16)


def _mla_attn_kernel(q_ref, k_ref, v_ref, o_ref, *, pack):
    outs = []
    for i in range(pack):
        q = q_ref[0, :, i * LANES:(i + 1) * LANES]
        k = k_ref[0, :, i * LANES:(i + 1) * LANES]
        s = lax.dot_general(q, k, (((1,), (1,)), ((), ())), preferred_element_type=F32)
        m = jnp.max(s, axis=-1, keepdims=True)
        p = jnp.exp(s - m)
        l = jnp.sum(p, axis=-1, keepdims=True)
        v = v_ref[0, :, i * C_V:(i + 1) * C_V]
        outs.append(jnp.dot(p.astype(BF16), v, preferred_element_type=F32) / l)
    o_ref[0] = jnp.concatenate(outs, axis=-1).astype(o_ref.dtype)


def _pad_cols(w, heads, real, slot):
    k = w.shape[0]
    w = w.reshape(k, heads, real)
    return jnp.pad(w, ((0, 0), (0, 0), (0, slot - real))).reshape(k, heads * slot)


def mixer_c(x, ln, wa, qa_norm, kva_norm, wq_b, wkv_b, qk_norm, wo, cs, sn, batch, s_len):
    t, d = x.shape
    q_lora, kv_lora = qa_norm.shape[0], kva_norm.shape[0]
    heads = wo.shape[0] // C_V
    lat_w = q_lora + kv_lora + LANES
    wa_p = jnp.pad(wa, ((0, 0), (0, lat_w - wa.shape[1]))).astype(BF16)
    lat = norm_matmul(x, ln, wa_p, F32, tn=lat_w)
    wq_p = _pad_cols(wq_b, heads, C_QK, LANES).astype(BF16)
    wkv = wkv_b.reshape(kv_lora, heads, C_NOPE + C_V)
    wk_p = jnp.pad(wkv[:, :, :C_NOPE], ((0, 0), (0, 0), (C_ROPE, LANES - C_QK))).reshape(kv_lora, heads * LANES)
    wv = wkv[:, :, C_NOPE:].reshape(kv_lora, heads * C_V)
    wkv_p = jnp.concatenate([wk_p, wv], axis=1).astype(BF16)
    gq = jnp.pad(qk_norm[0], (0, LANES - C_QK)).reshape(1, LANES)
    gk = jnp.pad(qk_norm[1], (0, LANES - C_QK)).reshape(1, LANES)
    tm = _row_tile(t, 512)
    row = lambda n: pl.BlockSpec((tm, n), lambda i: (i, 0))
    full = lambda a: pl.BlockSpec(a.shape, lambda i: (0, 0))
    qa2, kva2 = qa_norm.reshape(1, -1), kva_norm.reshape(1, -1)
    q, k, v = pl.pallas_call(
        functools.partial(_mla_qkv_kernel, heads=heads, q_lora=q_lora, kv_lora=kv_lora),
        grid=(t // tm,),
        in_specs=[row(lat_w), full(qa2), full(kva2), full(gq), full(gk), full(wq_p), full(wkv_p),
                  row(LANES), row(LANES)],
        out_specs=[row(heads * LANES), row(heads * LANES), row(heads * C_V)],
        out_shape=[jax.ShapeDtypeStruct((t, heads * LANES), BF16),
                   jax.ShapeDtypeStruct((t, heads * LANES), BF16),
                   jax.ShapeDtypeStruct((t, heads * C_V), BF16)],
        compiler_params=_params("parallel"),
        name="mla_qkv",
    )(lat, qa2, kva2, gq, gk, wq_p, wkv_p, cs, sn)
    pack = LANES // C_V
    tq = min(256, s_len)
    o = pl.pallas_call(
        functools.partial(_mla_attn_kernel, pack=pack),
        grid=(batch, heads // pack, s_len // tq),
        in_specs=[pl.BlockSpec((1, tq, pack * LANES), lambda b, h, i: (b, i, h)),
                  pl.BlockSpec((1, s_len, pack * LANES), lambda b, h, i: (b, 0, h)),
                  pl.BlockSpec((1, s_len, pack * C_V), lambda b, h, i: (b, 0, h))],
        out_specs=pl.BlockSpec((1, tq, pack * C_V), lambda b, h, i: (b, i, h)),
        out_shape=jax.ShapeDtypeStruct((batch, s_len, heads * C_V), BF16),
        compiler_params=_params("parallel", "parallel", "arbitrary"),
        name="mla_attention",
    )(q.reshape(batch, s_len, -1), k.reshape(batch, s_len, -1), v.reshape(batch, s_len, -1))
    return matmul_residual(o.reshape(t, heads * C_V), wo.astype(BF16), x)


B_HEAD_DIM = 64
B_DECAY_SCALE = math.exp(-0.5)
B_GN_EPS = 64e-5
WKV_CHUNK = 64
WKV_HEADS_PER_STEP = 8


def _bdot(a, b):
    return jnp.dot(a.astype(BF16), b.astype(BF16), preferred_element_type=F32)


def _rwkv_prep_kernel(x_ref, xp_ref, xn_ref, ln_ref, mu_ref, wrkv_ref, g1_ref, g2_ref, w0_ref, w1_ref, w2_ref,
                      a0_ref, a1_ref, a2_ref, r_ref, k_ref, v_ref, g_ref, lw_ref, as_ref, *, tiles_per_seq):
    i = pl.program_id(0)
    tm = x_ref.shape[0]
    ln = ln_ref[...]
    h = _rms(x_ref[...], ln)
    pos = i % tiles_per_seq
    prev_row = jnp.where(pos == 0, 0.0, _rms(xp_ref[7:8, :], ln))
    next_row = jnp.where(pos == tiles_per_seq - 1, 0.0, _rms(xn_ref[0:1, :], ln))
    row = lax.broadcasted_iota(I32, (tm, 1), 0)
    d_f = jnp.where(row == 0, prev_row, pltpu.roll(h, 1, 0)) - h
    d_b = jnp.where(row == tm - 1, next_row, pltpu.roll(h, tm - 1, 0)) - h
    d_c = 0.5 * (d_f + d_b)
    mu = mu_ref[...]
    mix = lambda dd, n: h + dd * mu[n:n + 1]
    r_ref[...] = _bdot(mix(d_c, 0), wrkv_ref[0])
    k_ref[...] = _bdot(mix(d_c, 1), wrkv_ref[1])
    v_ref[...] = _bdot(mix(d_c, 2), wrkv_ref[2])
    g_ref[...] = _bdot(jax.nn.sigmoid(_bdot(mix(d_c, 3), g1_ref[...])), g2_ref[...])
    for dr, dd in enumerate((d_f, d_b)):
        z = w0_ref[dr:dr + 1, :] + _bdot(jnp.tanh(_bdot(mix(dd, 4 + dr), w1_ref[dr])), w2_ref[dr])
        lw_ref[dr] = -B_DECAY_SCALE * jax.nn.sigmoid(z)
        aa = a0_ref[dr:dr + 1, :] + _bdot(_bdot(mix(dd, 6 + dr), a1_ref[dr]), a2_ref[dr])
        as_ref[dr] = jax.nn.sigmoid(aa)


def _wkv_kernel(r_ref, k_ref, v_ref, lw_ref, as_ref, kk_ref, ka_ref, y_ref, st_ref, *, hb):
    n = B_HEAD_DIM
    c = r_ref.shape[0]
    sgn = 1 - 2 * pl.program_id(1)

    @pl.when(pl.program_id(3) == 0)
    def _():
        st_ref[...] = jnp.zeros_like(st_ref)

    row = lax.broadcasted_iota(I32, (c, c), 0)
    col = lax.broadcasted_iota(I32, (c, c), 1)
    ahead = (row - col) * sgn
    strict, incl = ahead > 0, ahead >= 0
    lw = lw_ref[...]
    cl = jnp.dot(incl.astype(F32), lw, preferred_element_type=F32, precision=lax.Precision.HIGHEST)
    total = jnp.sum(lw, axis=0, keepdims=True)
    e_in, e_ex, e_ng, e_end = jnp.exp(cl), jnp.exp(cl - lw), jnp.exp(-cl), jnp.exp(total - cl)
    p_end = jnp.exp(total)
    rr, kk_raw, vv, asig = r_ref[...], k_ref[...], v_ref[...], as_ref[...]
    kk_all = kk_raw * kk_ref[...]
    kdir = kk_raw * (1.0 + (asig - 1.0) * ka_ref[...])
    r_t, k_t, k_h = rr * e_in, kdir * e_ng, kdir * e_end

    def same_block(m):
        return (row // m) == (col // m)

    eye = (row == col).astype(F32)
    nt = (((1,), (1,)), ((), ()))
    tn = (((0,), (0,)), ((), ()))
    outs = []
    for h in range(hb):
        sl = slice(h * n, (h + 1) * n)
        kkh = kk_all[:, sl]
        kkn = kkh * lax.rsqrt(jnp.maximum(jnp.sum(kkh * kkh, axis=-1, keepdims=True), 1e-24))
        beta = kkn * asig[:, sl]
        a_t = (-kkn * e_ex[:, sl]).astype(BF16)
        b_t = (beta * e_ng[:, sl]).astype(BF16)
        rt, kt, v = r_t[:, sl].astype(BF16), k_t[:, sl].astype(BF16), vv[:, sl].astype(BF16)
        g_ab = lax.dot_general(a_t, b_t, nt, preferred_element_type=F32)
        g_ak = lax.dot_general(a_t, kt, nt, preferred_element_type=F32)
        g_rb = lax.dot_general(rt, b_t, nt, preferred_element_type=F32)
        g_rk = lax.dot_general(rt, kt, nt, preferred_element_type=F32)
        l_ab = jnp.where(strict, g_ab, 0.0)
        x = eye + jnp.where(same_block(2), l_ab, 0.0)
        m = 2
        while m < c:
            l_m = jnp.where(same_block(2 * m) & jnp.logical_not(same_block(m)), l_ab, 0.0)
            x = x + _bdot(_bdot(x, l_m), x)
            m *= 2
        s0 = st_ref[h]
        s0b = s0.astype(BF16)
        w = lax.dot_general(a_t, s0b, nt, preferred_element_type=F32) + _bdot(jnp.where(strict, g_ak, 0.0), v)
        u = _bdot(x, w)
        y = (lax.dot_general(rt, s0b, nt, preferred_element_type=F32)
             + _bdot(jnp.where(incl, g_rb, 0.0), u) + _bdot(jnp.where(incl, g_rk, 0.0), v))
        outs.append(y)
        b_h = (beta * e_end[:, sl]).astype(BF16)
        st_ref[h] = (s0 * p_end[:, sl]
                     + lax.dot_general(u.astype(BF16), b_h, tn, preferred_element_type=F32)
                     + lax.dot_general(v, k_h[:, sl].astype(BF16), tn, preferred_element_type=F32))
    y_ref[...] = jnp.concatenate(outs, axis=-1)


def _rwkv_out_kernel(x_ref, y_ref, r_ref, k_ref, v_ref, g_ref, as_ref, ka_ref, rk_ref, lnx_ref, wo_ref, o_ref):
    d = x_ref.shape[1]
    n = B_HEAD_DIM
    y = y_ref[0] + y_ref[1]
    kd = k_ref[...] * (2.0 + (as_ref[0] + as_ref[1] - 2.0) * ka_ref[...])
    prod = r_ref[...] * kd * rk_ref[...]
    low = lax.broadcasted_iota(I32, (1, LANES), 1) < n

    def head_sum(a):
        s_lo = jnp.sum(jnp.where(low, a, 0.0), axis=-1, keepdims=True)
        s_hi = jnp.sum(jnp.where(low, 0.0, a), axis=-1, keepdims=True)
        return jnp.where(low, s_lo, s_hi)

    parts = []
    for p in range(d // LANES):
        sl = slice(p * LANES, (p + 1) * LANES)
        ys = y[:, sl]
        cen = ys - head_sum(ys) * (1.0 / n)
        var = head_sum(cen * cen) * (1.0 / n)
        yn = cen * lax.rsqrt(var + B_GN_EPS) * lnx_ref[0:1, sl] + lnx_ref[1:2, sl]
        bonus = head_sum(prod[:, sl]) * v_ref[:, sl]
        parts.append(((yn + bonus) * g_ref[:, sl]).astype(BF16))
    o_ref[...] = x_ref[...] + jnp.dot(jnp.concatenate(parts, axis=-1), wo_ref[...], preferred_element_type=F32)


def mixer_b(x, ln, mu, w_rkv, w0, w1, w2, a0, a1, a2, g1, g2, k_k, k_a, r_k, ln_x, wo, batch, s_len):
    t, d = x.shape
    bf = lambda a: a.astype(BF16)
    tm = _row_tile(s_len, 256)
    tiles_per_seq = s_len // tm
    halo = 8
    row = lambda n: pl.BlockSpec((tm, n), lambda i: (i, 0))
    row2 = lambda n: pl.BlockSpec((2, tm, n), lambda i: (0, i, 0))
    full = lambda a: pl.BlockSpec(a.shape, lambda i: (0,) * a.ndim)
    ln2 = ln.reshape(1, d)
    weights = (ln2, mu, bf(w_rkv), bf(g1), bf(g2), w0, bf(w1), bf(w2), a0, bf(a1), bf(a2))
    tok = jax.ShapeDtypeStruct((t, d), F32)
    tok2 = jax.ShapeDtypeStruct((2, t, d), F32)
    r, k, v, g, lw, asig = pl.pallas_call(
        functools.partial(_rwkv_prep_kernel, tiles_per_seq=tiles_per_seq),
        grid=(t // tm,),
        in_specs=[row(d),
                  pl.BlockSpec((halo, d), lambda i: (jnp.maximum(i * (tm // halo) - 1, 0), 0)),
                  pl.BlockSpec((halo, d), lambda i: (jnp.minimum((i + 1) * (tm // halo), t // halo - 1), 0))]
                 + [full(a) for a in weights],
        out_specs=[row(d), row(d), row(d), row(d), row2(d), row2(d)],
        out_shape=[tok, tok, tok, tok, tok2, tok2],
        compiler_params=_params("parallel"),
        name="rwkv_prep",
    )(x, x, x, *weights)

    hb = WKV_HEADS_PER_STEP
    width = hb * B_HEAD_DIM
    c = min(WKV_CHUNK, s_len)
    nc = s_len // c
    chunk_of = lambda dr, ci: ci + dr * (nc - 1 - 2 * ci)
    tokb = lambda: pl.BlockSpec((None, c, width), lambda b, dr, hg, ci: (b, chunk_of(dr, ci), hg))
    dirb = lambda: pl.BlockSpec((None, None, c, width), lambda b, dr, hg, ci: (dr, b, chunk_of(dr, ci), hg))
    par = lambda: pl.BlockSpec((1, width), lambda b, dr, hg, ci: (0, hg))
    v3 = lambda a: a.reshape(batch, s_len, d)
    v4 = lambda a: a.reshape(2, batch, s_len, d)
    y = pl.pallas_call(
        functools.partial(_wkv_kernel, hb=hb),
        grid=(batch, 2, d // width, nc),
        in_specs=[tokb(), tokb(), tokb(), dirb(), dirb(), par(), par()],
        out_specs=dirb(),
        out_shape=jax.ShapeDtypeStruct((2, batch, s_len, d), F32),
        scratch_shapes=[pltpu.VMEM((hb, B_HEAD_DIM, B_HEAD_DIM), F32)],
        compiler_params=_params("parallel", "parallel", "parallel", "arbitrary"),
        name="wkv_scan",
    )(v3(r), v3(k), v3(v), v4(lw), v4(asig), k_k.reshape(1, d), k_a.reshape(1, d))

    params = (k_a.reshape(1, d), r_k.reshape(1, d), ln_x, bf(wo))
    return pl.pallas_call(
        _rwkv_out_kernel,
        grid=(t // tm,),
        in_specs=[row(d), row2(d), row(d), row(d), row(d), row(d), row2(d)] + [full(a) for a in params],
        out_specs=row(d),
        out_shape=tok,
        compiler_params=_params("parallel"),
        name="rwkv_out",
    )(x, y.reshape(2, t, d), r, k, v, g, asig, *params)


TOP_K = 2
ROUTE_EXPERT, ROUTE_GATE, ROUTE_RANK = 0, TOP_K, 2 * TOP_K
MOE_ROW_TILE = 512


def _router_kernel(x_ref, g_ref, wr_ref, hn_ref, route_ref, cnt_ref, run_ref, *, n_exp):
    @pl.when(pl.program_id(0) == 0)
    def _():
        run_ref[...] = jnp.zeros_like(run_ref)

    tm = x_ref.shape[0]
    h = _rms(x_ref[...], g_ref[...])
    hn_ref[...] = h
    logits = jnp.dot(h, wr_ref[...], preferred_element_type=F32, precision=lax.Precision.HIGHEST)
    lane = lax.broadcasted_iota(I32, (tm, LANES), 1)
    logits = jnp.where(lane < n_exp, logits, NEG_INF)

    def top1(v):
        m = jnp.max(v, axis=-1, keepdims=True)
        return m, jnp.min(jnp.where(v == m, lane, LANES), axis=-1, keepdims=True)

    m1, i1 = top1(logits)
    m2, i2 = top1(jnp.where(lane == i1, NEG_INF, logits))
    e = jnp.exp(m2 - m1)
    g1 = 1.0 / (1.0 + e)
    g2 = e * g1
    sel = ((lane == i1) | (lane == i2)).astype(F32)
    earlier = (lax.broadcasted_iota(I32, (tm, tm), 0) > lax.broadcasted_iota(I32, (tm, tm), 1)).astype(BF16)
    before = jnp.dot(earlier, sel.astype(BF16), preferred_element_type=F32) + run_ref[...]
    r1 = jnp.sum(jnp.where(lane == i1, before, 0.0), axis=-1, keepdims=True)
    r2 = jnp.sum(jnp.where(lane == i2, before, 0.0), axis=-1, keepdims=True)
    run_ref[...] += jnp.sum(sel, axis=0, keepdims=True)
    cnt_ref[...] = run_ref[...]
    route = jnp.zeros((tm, LANES), F32)
    for off, vals in ((ROUTE_EXPERT, (i1.astype(F32), i2.astype(F32))), (ROUTE_GATE, (g1, g2)), (ROUTE_RANK, (r1, r2))):
        for kk, v in enumerate(vals):
            route = jnp.where(lane == off + kk, v, route)
    route_ref[...] = route


def _dispatch_kernel(dest_ref, hn_ref, xs_in_ref, xs_ref, sem):
    del xs_in_ref
    tm = hn_ref.shape[0]

    def row_copy(r, slot):
        return pltpu.make_async_copy(hn_ref.at[pl.ds(r, 1)], xs_ref.at[pl.ds(dest_ref[TOP_K * r + slot], 1)], sem)

    def start(r, c):
        for slot in range(TOP_K):
            row_copy(r, slot).start()
        return c

    def wait(r, c):
        for slot in range(TOP_K):
            row_copy(r, slot).wait()
        return c

    lax.fori_loop(0, tm, start, 0)
    lax.fori_loop(0, tm, wait, 0)


def _expert_ffn_kernel(te_ref, x_ref, w1_ref, w3_ref, w2_ref, o_ref, xb_ref, acc_ref):
    del te_ref
    j = pl.program_id(1)

    @pl.when(j == 0)
    def _():
        xb_ref[...] = x_ref[...].astype(BF16)
        acc_ref[...] = jnp.zeros_like(acc_ref)

    xb = xb_ref[...]
    gate = jnp.dot(xb, w1_ref[...], preferred_element_type=F32)
    up = jnp.dot(xb, w3_ref[...], preferred_element_type=F32)
    act = (gate * jax.nn.sigmoid(gate) * up).astype(BF16)
    acc_ref[...] += jnp.dot(act, w2_ref[...], preferred_element_type=F32)

    @pl.when(j == pl.num_programs(1) - 1)
    def _():
        o_ref[...] = acc_ref[...]


def _moe_combine_kernel(dest_ref, route_ref, x_ref, ys_ref, o_ref, buf_ref, sem):
    tm = x_ref.shape[0]

    def row_copy(r, slot):
        return pltpu.make_async_copy(ys_ref.at[pl.ds(dest_ref[TOP_K * r + slot], 1)], buf_ref.at[slot, pl.ds(r, 1)], sem)

    def start(r, c):
        for slot in range(TOP_K):
            row_copy(r, slot).start()
        return c

    def wait(r, c):
        for slot in range(TOP_K):
            row_copy(r, slot).wait()
        return c

    lax.fori_loop(0, tm, start, 0)
    lax.fori_loop(0, tm, wait, 0)
    route = route_ref[...]
    out = x_ref[...]
    for slot in range(TOP_K):
        out = out + route[:, ROUTE_GATE + slot:ROUTE_GATE + slot + 1] * buf_ref[slot]
    o_ref[...] = out


def moe_ffn(x, ln, router, w13, w2, tf=512):
    t, d = x.shape
    n_exp, dff = w2.shape[0], w2.shape[1]
    assert dff % tf == 0
    nj = dff // tf
    tmx = min(MOE_ROW_TILE, t)
    tm = _row_tile(t, 512)
    wr = jnp.pad(router, ((0, 0), (0, LANES - n_exp)))
    hn, route, cnt = pl.pallas_call(
        functools.partial(_router_kernel, n_exp=n_exp),
        grid=(t // tm,),
        in_specs=[pl.BlockSpec((tm, d), lambda i: (i, 0)),
                  pl.BlockSpec((1, d), lambda i: (0, 0)),
                  pl.BlockSpec((d, LANES), lambda i: (0, 0))],
        out_specs=[pl.BlockSpec((tm, d), lambda i: (i, 0)),
                   pl.BlockSpec((tm, LANES), lambda i: (i, 0)),
                   pl.BlockSpec((1, LANES), lambda i: (0, 0))],
        out_shape=[jax.ShapeDtypeStruct((t, d), F32),
                   jax.ShapeDtypeStruct((t, LANES), F32),
                   jax.ShapeDtypeStruct((1, LANES), F32)],
        scratch_shapes=[pltpu.VMEM((1, LANES), F32)],
        compiler_params=_params("arbitrary"),
        name="moe_router",
    )(x, ln.reshape(1, d), wr)

    expert = route[:, ROUTE_EXPERT:ROUTE_EXPERT + TOP_K].astype(I32)
    rank = route[:, ROUTE_RANK:ROUTE_RANK + TOP_K].astype(I32)
    counts = cnt[0, :n_exp].astype(I32)
    padded = (counts + tmx - 1) // tmx * tmx
    ends = jnp.cumsum(padded)
    dest = ((ends - padded)[expert] + rank).reshape(TOP_K * t)
    n_tiles = (TOP_K * t) // tmx + n_exp
    rows = n_tiles * tmx
    tile_expert = jnp.minimum(jnp.searchsorted(ends, jnp.arange(n_tiles, dtype=I32) * tmx, side="right"),
                              n_exp - 1).astype(I32)

    smem_rows = lambda n: pl.BlockSpec((TOP_K * n,), lambda i: (i,), memory_space=pltpu.SMEM)
    xs = pl.pallas_call(
        _dispatch_kernel,
        grid=(t // tm,),
        in_specs=[smem_rows(tm),
                  pl.BlockSpec((tm, d), lambda i: (i, 0)),
                  pl.BlockSpec(memory_space=pl.ANY)],
        out_specs=pl.BlockSpec(memory_space=pl.ANY),
        out_shape=jax.ShapeDtypeStruct((rows, d), F32),
        scratch_shapes=[pltpu.SemaphoreType.DMA(())],
        input_output_aliases={2: 0},
        compiler_params=_params("arbitrary"),
        name="moe_dispatch",
    )(dest, hn, jnp.zeros((rows, d), F32))

    ys = pl.pallas_call(
        _expert_ffn_kernel,
        grid_spec=pltpu.PrefetchScalarGridSpec(
            num_scalar_prefetch=1,
            grid=(n_tiles, nj),
            in_specs=[pl.BlockSpec((tmx, d), lambda i, j, te: (i, 0)),
                      pl.BlockSpec((None, d, tf), lambda i, j, te: (te[i], 0, j)),
                      pl.BlockSpec((None, d, tf), lambda i, j, te: (te[i], 0, nj + j)),
                      pl.BlockSpec((None, tf, d), lambda i, j, te: (te[i], j, 0))],
            out_specs=pl.BlockSpec((tmx, d), lambda i, j, te: (i, 0)),
            scratch_shapes=[pltpu.VMEM((tmx, d), BF16), pltpu.VMEM((tmx, d), F32)]),
        out_shape=jax.ShapeDtypeStruct((rows, d), F32),
        compiler_params=_params("parallel", "arbitrary"),
        name="moe_expert_ffn",
    )(tile_expert, xs, w13, w13, w2)

    return pl.pallas_call(
        _moe_combine_kernel,
        grid=(t // tm,),
        in_specs=[smem_rows(tm),
                  pl.BlockSpec((tm, LANES), lambda i: (i, 0)),
                  pl.BlockSpec((tm, d), lambda i: (i, 0)),
                  pl.BlockSpec(memory_space=pl.ANY)],
        out_specs=pl.BlockSpec((tm, d), lambda i: (i, 0)),
        out_shape=jax.ShapeDtypeStruct((t, d), F32),
        scratch_shapes=[pltpu.VMEM((TOP_K, tm, d), F32), pltpu.SemaphoreType.DMA(())],
        compiler_params=_params("arbitrary"),
        name="moe_combine",
    )(dest, route, x, ys)


def kernel(x, positions, ln_mix, ln_ffn, a_wqkv, a_qk_norm, a_wo, b_mu, b_wrkv, b_w0, b_w1, b_w2, b_a0, b_a1, b_a2, b_g1, b_g2, b_kk, b_ka, b_rk, b_lnx, b_wo, c_wa, c_qa_norm, c_kva_norm, c_wq_b, c_wkv_b, c_qk_norm, c_wo, f_w13, f_w2, m_router, m_w13, m_w2):
    batch, s_len, d = x.shape
    t = batch * s_len
    cs, sn = _rope_tables(positions)
    cs, sn = cs.reshape(t, LANES), sn.reshape(t, LANES)
    h = x.reshape(t, d)
    n_mixers = 3
    for i in range(ln_mix.shape[0]):
        j, kind = divmod(i, n_mixers)
        if kind == 0:
            h = mixer_a(h, ln_mix[i], a_wqkv[j], a_qk_norm[j], a_wo[j], cs, sn, batch, s_len)
        elif kind == 1:
            h = mixer_b(h, ln_mix[i], b_mu[j], b_wrkv[j], b_w0[j], b_w1[j], b_w2[j], b_a0[j], b_a1[j], b_a2[j],
                        b_g1[j], b_g2[j], b_kk[j], b_ka[j], b_rk[j], b_lnx[j], b_wo[j], batch, s_len)
        else:
            h = mixer_c(h, ln_mix[i], c_wa[j], c_qa_norm[j], c_kva_norm[j], c_wq_b[j], c_wkv_b[j], c_qk_norm[j],
                        c_wo[j], cs, sn, batch, s_len)
        f = i // 2
        if i % 2 == 0:
            h = dense_ffn(h, ln_ffn[i], f_w13[f].astype(BF16), f_w2[f].astype(BF16))
        else:
            h = moe_ffn(h, ln_ffn[i], m_router[f], m_w13[f].astype(BF16), m_w2[f].astype(BF16))
    return h.reshape(batch, s_len, d)
```

```python
import functools
import math

import jax
import jax.numpy as jnp
from jax import lax
from jax.experimental import pallas as pl
from jax.experimental.pallas import tpu as pltpu

F32 = jnp.float32
BF16 = jnp.bfloat16
I32 = jnp.int32

NORM_EPS = 1e-6
ROPE_THETA = 500000.0
NEG_INF = -1e30
LANES = 128
ROT_DIM = 32
ROT_HALF = ROT_DIM // 2
DILATED_PAIRS = ((128, 1), (512, 4), (2048, 16))
A_HEAD_DIM = 128
VMEM_LIMIT = 48 * 1024 * 1024


def _params(*sem):
    return pltpu.CompilerParams(dimension_semantics=sem, vmem_limit_bytes=VMEM_LIMIT)


def _rms(x, g):
    return x * lax.rsqrt(jnp.mean(x * x, axis=-1, keepdims=True) + NORM_EPS) * g


def _row_tile(t, want):
    tm = min(t, want)
    assert t % tm == 0
    return tm


def _norm_matmul_kernel(x_ref, g_ref, w_ref, o_ref, xn_ref):
    @pl.when(pl.program_id(1) == 0)
    def _():
        xn_ref[...] = _rms(x_ref[...], g_ref[...]).astype(BF16)

    o_ref[...] = jnp.dot(xn_ref[...], w_ref[...], preferred_element_type=F32).astype(o_ref.dtype)


def norm_matmul(x, g, w, out_dtype, tn):
    t, k = x.shape
    n = w.shape[1]
    tm = _row_tile(t, 1024)
    assert n % tn == 0
    return pl.pallas_call(
        _norm_matmul_kernel,
        grid=(t // tm, n // tn),
        in_specs=[pl.BlockSpec((tm, k), lambda i, j: (i, 0)),
                  pl.BlockSpec((1, k), lambda i, j: (0, 0)),
                  pl.BlockSpec((k, tn), lambda i, j: (0, j))],
        out_specs=pl.BlockSpec((tm, tn), lambda i, j: (i, j)),
        out_shape=jax.ShapeDtypeStruct((t, n), out_dtype),
        scratch_shapes=[pltpu.VMEM((tm, k), BF16)],
        compiler_params=_params("parallel", "arbitrary"),
        name="norm_matmul",
    )(x, g.reshape(1, k), w)


def _matmul_res_kernel(a_ref, w_ref, x_ref, o_ref):
    o_ref[...] = x_ref[...] + jnp.dot(a_ref[...], w_ref[...], preferred_element_type=F32)


def matmul_residual(a, w, x):
    t, k = a.shape
    n = w.shape[1]
    tm = _row_tile(t, 1024)
    return pl.pallas_call(
        _matmul_res_kernel,
        grid=(t // tm,),
        in_specs=[pl.BlockSpec((tm, k), lambda i: (i, 0)),
                  pl.BlockSpec((k, n), lambda i: (0, 0)),
                  pl.BlockSpec((tm, n), lambda i: (i, 0))],
        out_specs=pl.BlockSpec((tm, n), lambda i: (i, 0)),
        out_shape=jax.ShapeDtypeStruct((t, n), F32),
        compiler_params=_params("parallel"),
        name="matmul_residual",
    )(a, w, x)


def _rope_tables(positions):
    b, s = positions.shape
    inv_freq = ROPE_THETA ** (-jnp.arange(0, ROT_DIM, 2, dtype=F32) / ROT_DIM)
    ang = positions.astype(F32)[..., None] * inv_freq
    c, sn = jnp.cos(ang), jnp.sin(ang)
    rest = (b, s, LANES - ROT_DIM)
    cs_t = jnp.concatenate([c, c, jnp.ones(rest, F32)], axis=-1)
    sn_t = jnp.concatenate([-sn, sn, jnp.zeros(rest, F32)], axis=-1)
    return cs_t, sn_t


def _rope(x, cs, sn, lane):
    n = x.shape[-1]
    partner = jnp.where(lane < ROT_HALF, pltpu.roll(x, n - ROT_HALF, 1), pltpu.roll(x, ROT_HALF, 1))
    return x * cs + partner * sn


def _dil_attn_kernel(q_ref, k_ref, v_ref, cs_ref, sn_ref, gq_ref, gk_ref, o_ref, lse_ref, kn_ref,
                     *, seq, qt, kw, half, heads):
    hd = A_HEAD_DIM
    qi = pl.program_id(2)
    lane = lax.broadcasted_iota(I32, (1, hd), 1)

    @pl.when(qi == 0)
    def _():
        cs, sn = cs_ref[0], sn_ref[0]
        for h in range(heads):
            kh = k_ref[0, :, h * hd:(h + 1) * hd].astype(F32)
            kn_ref[:, h * hd:(h + 1) * hd] = _rope(_rms(kh, gk_ref[...]), cs, sn, lane).astype(BF16)

    q0 = pl.multiple_of(qi * qt, qt)
    start = jnp.clip(q0 - half, 0, seq - kw)
    start = pl.multiple_of(start, math.gcd(half, qt))
    cs_q = cs_ref[0, pl.ds(q0, qt), :]
    sn_q = sn_ref[0, pl.ds(q0, qt), :]
    qpos = q0 + lax.broadcasted_iota(I32, (qt, kw), 0)
    kpos = start + lax.broadcasted_iota(I32, (qt, kw), 1)
    valid = jnp.abs(qpos - kpos) <= half
    head_of_lane = lax.broadcasted_iota(I32, (1, LANES), 1) // (LANES // heads)
    scale = hd ** -0.5
    sls = [slice(h * hd, (h + 1) * hd) for h in range(heads)]
    qh = [(_rope(_rms(q_ref[0, :, sl].astype(F32), gq_ref[...]), cs_q, sn_q, lane) * scale).astype(BF16) for sl in sls]
    s = [lax.dot_general(qh[h], kn_ref[pl.ds(start, kw), sls[h]], (((1,), (1,)), ((), ())),
                         preferred_element_type=F32) for h in range(heads)]
    s = [jnp.where(valid, t, NEG_INF) for t in s]
    m = [jnp.max(t, axis=-1, keepdims=True) for t in s]
    p = [jnp.exp(t - mm) for t, mm in zip(s, m)]
    l = [jnp.sum(t, axis=-1, keepdims=True) for t in p]
    o = [jnp.dot(p[h].astype(BF16), v_ref[0, pl.ds(start, kw), sls[h]], preferred_element_type=F32)
         for h in range(heads)]
    lse_row = jnp.zeros((qt, LANES), F32)
    for h in range(heads):
        o_ref[0, :, sls[h]] = (o[h] / l[h]).astype(o_ref.dtype)
        lse_row = jnp.where(head_of_lane == h, m[h] + jnp.log(l[h]), lse_row)
    lse_ref[0] = lse_row


def _qkv_group_kernel(x_ref, g_ref, w_ref, o_ref, xn_ref, acc_ref, *, dil):
    @pl.when(pl.program_id(1) == 0)
    def _():
        xn_ref[...] = _rms(x_ref[...], g_ref[...]).astype(BF16)

    acc = jnp.dot(xn_ref[...], w_ref[...], preferred_element_type=F32)
    if dil == 1:
        o_ref[0, 0] = acc.astype(o_ref.dtype)
    else:
        chunks = acc.shape[1] // LANES
        for c in range(chunks):
            acc_ref[c] = acc[:, c * LANES:(c + 1) * LANES]
        rows = acc.shape[0] // dil
        for r in range(dil):
            o_ref[0, r] = jnp.concatenate([acc_ref[c, pl.ds(r, rows, stride=dil), :] for c in range(chunks)],
                                          axis=-1).astype(o_ref.dtype)


def qkv_group_projection(x, ln, wqkv, group, batch, s_len, width):
    t, d = x.shape
    dil = DILATED_PAIRS[group][1]
    n_groups = len(DILATED_PAIRS)
    tm = _row_tile(s_len, 1024)
    tiles = s_len // tm
    assert tm % (16 * dil) == 0
    return pl.pallas_call(
        functools.partial(_qkv_group_kernel, dil=dil),
        grid=(t // tm, 3),
        in_specs=[pl.BlockSpec((tm, d), lambda i, j: (i, 0)),
                  pl.BlockSpec((1, d), lambda i, j: (0, 0)),
                  pl.BlockSpec((d, width), lambda i, j: (0, j * n_groups + group))],
        out_specs=pl.BlockSpec((1, dil, tm // dil, width), lambda i, j: (i // tiles, 0, i % tiles, j)),
        out_shape=jax.ShapeDtypeStruct((batch, dil, s_len // dil, 3 * width), BF16),
        scratch_shapes=[pltpu.VMEM((tm, d), BF16), pltpu.VMEM((width // LANES, tm, LANES), F32)],
        compiler_params=_params("parallel", "arbitrary"),
        name=f"qkv_projection_g{group}",
    )(x, ln.reshape(1, d), wqkv)


def _by_residue(a, batch, s_len, dil):
    return a.reshape(batch, s_len // dil, dil, a.shape[-1]).transpose(0, 2, 1, 3)


def dilated_group_attention(qkv, cs, sn, gq, gk, group, batch, s_len, heads):
    window, dil = DILATED_PAIRS[group]
    half = window // (2 * dil)
    seq = s_len // dil
    width = heads * A_HEAD_DIM
    qt = min(128, seq)
    kw = min(qt + 2 * half, seq)
    assert seq % qt == 0 and (seq - kw) % math.gcd(half, qt) == 0
    whole = lambda n, c: pl.BlockSpec((None, 1, seq, n), lambda b, r, qi: (b, r, 0, c))
    tile = lambda n: pl.BlockSpec((None, 1, qt, n), lambda b, r, qi: (b, r, qi, 0))
    kern = functools.partial(_dil_attn_kernel, seq=seq, qt=qt, kw=kw, half=half, heads=heads)
    return pl.pallas_call(
        kern,
        grid=(batch, dil, seq // qt),
        in_specs=[tile(width), whole(width, 1), whole(width, 2), whole(LANES, 0), whole(LANES, 0),
                  pl.BlockSpec((1, A_HEAD_DIM), lambda b, r, qi: (0, 0)),
                  pl.BlockSpec((1, A_HEAD_DIM), lambda b, r, qi: (0, 0))],
        out_specs=[tile(width), tile(LANES)],
        out_shape=[jax.ShapeDtypeStruct((batch, dil, seq, width), BF16),
                   jax.ShapeDtypeStruct((batch, dil, seq, LANES), F32)],
        scratch_shapes=[pltpu.VMEM((seq, width), BF16)],
        compiler_params=_params("parallel", "parallel", "arbitrary"),
        name=f"dilated_attention_g{group}",
    )(qkv, qkv, qkv, _by_residue(cs, batch, s_len, dil), _by_residue(sn, batch, s_len, dil),
      gq.reshape(1, -1), gk.reshape(1, -1))


def _combine_wo_kernel(x_ref, o0_ref, o1_ref, o2_ref, l0_ref, l1_ref, l2_ref, w_ref, out_ref, ot_ref, lt_ref, *, heads):
    hd = A_HEAD_DIM
    rep = LANES // heads
    tm = x_ref.shape[0]

    def token_order(src_ref, dst_ref, g):
        dil = src_ref.shape[0]
        if dil == 1:
            return src_ref[0].astype(F32)
        chunks = src_ref.shape[-1] // LANES
        for r in range(dil):
            blk = src_ref[r].astype(F32)
            for c in range(chunks):
                dst_ref[g, c, pl.ds(r, tm // dil, stride=dil), :] = blk[:, c * LANES:(c + 1) * LANES]
        return jnp.concatenate([dst_ref[g, c] for c in range(chunks)], axis=-1)

    ls = [token_order(ref, lt_ref, g) for g, ref in enumerate((l0_ref, l1_ref, l2_ref))]
    os_ = [token_order(ref, ot_ref, g) for g, ref in enumerate((o0_ref, o1_ref, o2_ref))]
    m = jnp.maximum(jnp.maximum(ls[0], ls[1]), ls[2])
    es = [jnp.exp(l - m) for l in ls]
    inv = 1.0 / (es[0] + es[1] + es[2])
    al = [e * inv for e in es]
    parts = []
    for h in range(heads):
        sl = slice(h * hd, (h + 1) * hd)
        c = slice(h * rep, h * rep + 1)
        parts.append((al[0][:, c] * os_[0][:, sl] + al[1][:, c] * os_[1][:, sl] + al[2][:, c] * os_[2][:, sl]).astype(BF16))
    o = jnp.concatenate(parts, axis=-1)
    out_ref[...] = x_ref[...] + jnp.dot(o, w_ref[...], preferred_element_type=F32)


def combine_wo(x, outs, lses, wo, heads, batch, s_len):
    t, d = x.shape
    width = wo.shape[0]
    n_groups = len(DILATED_PAIRS)
    tm = _row_tile(s_len, 512)
    tiles = s_len // tm
    assert all(tm % (16 * dil) == 0 for _, dil in DILATED_PAIRS)
    row = lambda n: pl.BlockSpec((tm, n), lambda i: (i, 0))
    res = lambda g, n: pl.BlockSpec((None, DILATED_PAIRS[g][1], tm // DILATED_PAIRS[g][1], n),
                                    lambda i: (i // tiles, 0, i % tiles, 0))
    return pl.pallas_call(
        functools.partial(_combine_wo_kernel, heads=heads),
        grid=(t // tm,),
        in_specs=[row(d)] + [res(g, width) for g in range(n_groups)] + [res(g, LANES) for g in range(n_groups)]
                 + [pl.BlockSpec((width, d), lambda i: (0, 0))],
        out_specs=row(d),
        out_shape=jax.ShapeDtypeStruct((t, d), F32),
        scratch_shapes=[pltpu.VMEM((n_groups, width // LANES, tm, LANES), F32),
                        pltpu.VMEM((n_groups, 1, tm, LANES), F32)],
        compiler_params=_params("parallel"),
        name="combine_wo",
    )(x, *outs, *lses, wo)


def mixer_a(x, ln, wqkv, qk_norm, wo, cs, sn, batch, s_len):
    heads = wo.shape[0] // A_HEAD_DIM
    wqkv = wqkv.astype(BF16)
    outs, lses = [], []
    for g in range(len(DILATED_PAIRS)):
        qkv = qkv_group_projection(x, ln, wqkv, g, batch, s_len, wo.shape[0])
        o, lse = dilated_group_attention(qkv, cs, sn, qk_norm[0, g], qk_norm[1, g], g, batch, s_len, heads)
        outs.append(o)
        lses.append(lse)
    return combine_wo(x, outs, lses, wo.astype(BF16), heads, batch, s_len)


def _ffn_kernel(x_ref, g_ref, w1_ref, w3_ref, w2_ref, o_ref, xn_ref, acc_ref):
    j = pl.program_id(1)

    @pl.when(j == 0)
    def _():
        xn_ref[...] = _rms(x_ref[...], g_ref[...]).astype(BF16)
        acc_ref[...] = jnp.zeros_like(acc_ref)

    xn = xn_ref[...]
    gate = jnp.dot(xn, w1_ref[...], preferred_element_type=F32)
    up = jnp.dot(xn, w3_ref[...], preferred_element_type=F32)
    act = (gate * jax.nn.sigmoid(gate) * up).astype(BF16)
    acc_ref[...] += jnp.dot(act, w2_ref[...], preferred_element_type=F32)

    @pl.when(j == pl.num_programs(1) - 1)
    def _():
        o_ref[...] = x_ref[...] + acc_ref[...]


def dense_ffn(x, ln, w13, w2, tf=256):
    t, d = x.shape
    dff = w2.shape[0]
    assert dff % tf == 0
    nj = dff // tf
    tm = _row_tile(t, 1024)
    return pl.pallas_call(
        _ffn_kernel,
        grid=(t // tm, nj),
        in_specs=[pl.BlockSpec((tm, d), lambda i, j: (i, 0)),
                  pl.BlockSpec((1, d), lambda i, j: (0, 0)),
                  pl.BlockSpec((d, tf), lambda i, j: (0, j)),
                  pl.BlockSpec((d, tf), lambda i, j: (0, nj + j)),
                  pl.BlockSpec((tf, d), lambda i, j: (j, 0))],
        out_specs=pl.BlockSpec((tm, d), lambda i, j: (i, 0)),
        out_shape=jax.ShapeDtypeStruct((t, d), F32),
        scratch_shapes=[pltpu.VMEM((tm, d), BF16), pltpu.VMEM((tm, d), F32)],
        compiler_params=_params("parallel", "arbitrary"),
        name="dense_ffn",
    )(x, ln.reshape(1, d), w13, w13, w2)


C_ROPE = ROT_DIM
C_NOPE = 64
C_V = 64
C_QK = C_ROPE + C_NOPE


def _mla_qkv_kernel(lat_ref, qa_ref, kva_ref, gq_ref, gk_ref, wq_ref, wkv_ref, cs_ref, sn_ref,
                    q_ref, k_ref, v_ref, *, heads, q_lora, kv_lora):
    lane = lax.broadcasted_iota(I32, (1, LANES), 1)
    cs, sn = cs_ref[...], sn_ref[...]
    lat = lat_ref[...]
    qn = _rms(lat[:, :q_lora], qa_ref[...]).astype(BF16)
    kvn = _rms(lat[:, q_lora:q_lora + kv_lora], kva_ref[...]).astype(BF16)
    k_rope = lat[:, q_lora + kv_lora:]
    q = jnp.dot(qn, wq_ref[...], preferred_element_type=F32)
    kv = jnp.dot(kvn, wkv_ref[...], preferred_element_type=F32)
    scale = C_QK ** -0.5

    def head_norm(xh, g):
        ms = jnp.sum(xh * xh, axis=-1, keepdims=True) * (1.0 / C_QK)
        return _rope(xh * lax.rsqrt(ms + NORM_EPS) * g, cs, sn, lane)

    for h in range(heads):
        sl = slice(h * LANES, (h + 1) * LANES)
        q_ref[:, sl] = (head_norm(q[:, sl], gq_ref[...]) * scale).astype(BF16)
        k_ref[:, sl] = head_norm(kv[:, sl] + k_rope, gk_ref[...]).astype(BF16)
    v_ref[...] = kv[:, heads * LANES:].astype(BF16)


def _mla_attn_kernel(q_ref, k_ref, v_ref, o_ref, *, pack):
    outs = []
    for i in range(pack):
        q = q_ref[0, :, i * LANES:(i + 1) * LANES]
        k = k_ref[0, :, i * LANES:(i + 1) * LANES]
        s = lax.dot_general(q, k, (((1,), (1,)), ((), ())), preferred_element_type=F32)
        m = jnp.max(s, axis=-1, keepdims=True)
        p = jnp.exp(s - m)
        l = jnp.sum(p, axis=-1, keepdims=True)
        v = v_ref[0, :, i * C_V:(i + 1) * C_V]
        outs.append(jnp.dot(p.astype(BF16), v, preferred_element_type=F32) / l)
    o_ref[0] = jnp.concatenate(outs, axis=-1).astype(o_ref.dtype)


def _pad_cols(w, heads, real, slot):
    k = w.shape[0]
    w = w.reshape(k, heads, real)
    return jnp.pad(w, ((0, 0), (0, 0), (0, slot - real))).reshape(k, heads * slot)


def mixer_c(x, ln, wa, qa_norm, kva_norm, wq_b, wkv_b, qk_norm, wo, cs, sn, batch, s_len):
    t, d = x.shape
    q_lora, kv_lora = qa_norm.shape[0], kva_norm.shape[0]
    heads = wo.shape[0] // C_V
    lat_w = q_lora + kv_lora + LANES
    wa_p = jnp.pad(wa, ((0, 0), (0, lat_w - wa.shape[1]))).astype(BF16)
    lat = norm_matmul(x, ln, wa_p, F32, tn=lat_w)
    wq_p = _pad_cols(wq_b, heads, C_QK, LANES).astype(BF16)
    wkv = wkv_b.reshape(kv_lora, heads, C_NOPE + C_V)
    wk_p = jnp.pad(wkv[:, :, :C_NOPE], ((0, 0), (0, 0), (C_ROPE, LANES - C_QK))).reshape(kv_lora, heads * LANES)
    wv = wkv[:, :, C_NOPE:].reshape(kv_lora, heads * C_V)
    wkv_p = jnp.concatenate([wk_p, wv], axis=1).astype(BF16)
    gq = jnp.pad(qk_norm[0], (0, LANES - C_QK)).reshape(1, LANES)
    gk = jnp.pad(qk_norm[1], (0, LANES - C_QK)).reshape(1, LANES)
    tm = _row_tile(t, 512)
    row = lambda n: pl.BlockSpec((tm, n), lambda i: (i, 0))
    full = lambda a: pl.BlockSpec(a.shape, lambda i: (0, 0))
    qa2, kva2 = qa_norm.reshape(1, -1), kva_norm.reshape(1, -1)
    q, k, v = pl.pallas_call(
        functools.partial(_mla_qkv_kernel, heads=heads, q_lora=q_lora, kv_lora=kv_lora),
        grid=(t // tm,),
        in_specs=[row(lat_w), full(qa2), full(kva2), full(gq), full(gk), full(wq_p), full(wkv_p),
                  row(LANES), row(LANES)],
        out_specs=[row(heads * LANES), row(heads * LANES), row(heads * C_V)],
        out_shape=[jax.ShapeDtypeStruct((t, heads * LANES), BF16),
                   jax.ShapeDtypeStruct((t, heads * LANES), BF16),
                   jax.ShapeDtypeStruct((t, heads * C_V), BF16)],
        compiler_params=_params("parallel"),
        name="mla_qkv",
    )(lat, qa2, kva2, gq, gk, wq_p, wkv_p, cs, sn)
    pack = LANES // C_V
    tq = min(256, s_len)
    o = pl.pallas_call(
        functools.partial(_mla_attn_kernel, pack=pack),
        grid=(batch, heads // pack, s_len // tq),
        in_specs=[pl.BlockSpec((1, tq, pack * LANES), lambda b, h, i: (b, i, h)),
                  pl.BlockSpec((1, s_len, pack * LANES), lambda b, h, i: (b, 0, h)),
                  pl.BlockSpec((1, s_len, pack * C_V), lambda b, h, i: (b, 0, h))],
        out_specs=pl.BlockSpec((1, tq, pack * C_V), lambda b, h, i: (b, i, h)),
        out_shape=jax.ShapeDtypeStruct((batch, s_len, heads * C_V), BF16),
        compiler_params=_params("parallel", "parallel", "arbitrary"),
        name="mla_attention",
    )(q.reshape(batch, s_len, -1), k.reshape(batch, s_len, -1), v.reshape(batch, s_len, -1))
    return matmul_residual(o.reshape(t, heads * C_V), wo.astype(BF16), x)


B_HEAD_DIM = 64
B_DECAY_SCALE = math.exp(-0.5)
B_GN_EPS = 64e-5
WKV_CHUNK = 64
WKV_HEADS_PER_STEP = 16


def _bdot(a, b):
    return jnp.dot(a.astype(BF16), b.astype(BF16), preferred_element_type=F32)


def _rwkv_prep_kernel(x_ref, xp_ref, xn_ref, ln_ref, mu_ref, wrkv_ref, g1_ref, g2_ref, w0_ref, w1_ref, w2_ref,
                      a0_ref, a1_ref, a2_ref, r_ref, k_ref, v_ref, g_ref, lw_ref, as_ref, *, tiles_per_seq):
    i = pl.program_id(0)
    tm = x_ref.shape[0]
    ln = ln_ref[...]
    h = _rms(x_ref[...], ln)
    pos = i % tiles_per_seq
    prev_row = jnp.where(pos == 0, 0.0, _rms(xp_ref[7:8, :], ln))
    next_row = jnp.where(pos == tiles_per_seq - 1, 0.0, _rms(xn_ref[0:1, :], ln))
    row = lax.broadcasted_iota(I32, (tm, 1), 0)
    d_f = jnp.where(row == 0, prev_row, pltpu.roll(h, 1, 0)) - h
    d_b = jnp.where(row == tm - 1, next_row, pltpu.roll(h, tm - 1, 0)) - h
    d_c = 0.5 * (d_f + d_b)
    mu = mu_ref[...]
    mix = lambda dd, n: h + dd * mu[n:n + 1]
    r_ref[...] = _bdot(mix(d_c, 0), wrkv_ref[0])
    k_ref[...] = _bdot(mix(d_c, 1), wrkv_ref[1])
    v_ref[...] = _bdot(mix(d_c, 2), wrkv_ref[2])
    g_ref[...] = _bdot(jax.nn.sigmoid(_bdot(mix(d_c, 3), g1_ref[...])), g2_ref[...])
    for dr, dd in enumerate((d_f, d_b)):
        z = w0_ref[dr:dr + 1, :] + _bdot(jnp.tanh(_bdot(mix(dd, 4 + dr), w1_ref[dr])), w2_ref[dr])
        lw_ref[dr] = -B_DECAY_SCALE * jax.nn.sigmoid(z)
        aa = a0_ref[dr:dr + 1, :] + _bdot(_bdot(mix(dd, 6 + dr), a1_ref[dr]), a2_ref[dr])
        as_ref[dr] = jax.nn.sigmoid(aa)


def _wkv_kernel(r_ref, k_ref, v_ref, lw_ref, as_ref, kk_ref, ka_ref, y_ref, st_ref, *, hb):
    n = B_HEAD_DIM
    c = r_ref.shape[0]
    sgn = 1 - 2 * pl.program_id(1)

    @pl.when(pl.program_id(3) == 0)
    def _():
        st_ref[...] = jnp.zeros_like(st_ref)

    row = lax.broadcasted_iota(I32, (c, c), 0)
    col = lax.broadcasted_iota(I32, (c, c), 1)
    ahead = (row - col) * sgn
    strict, incl = ahead > 0, ahead >= 0
    lw = lw_ref[...]
    cl = jnp.dot(incl.astype(F32), lw, preferred_element_type=F32, precision=lax.Precision.HIGHEST)
    total = jnp.sum(lw, axis=0, keepdims=True)
    e_in, e_ex, e_ng, e_end = jnp.exp(cl), jnp.exp(cl - lw), jnp.exp(-cl), jnp.exp(total - cl)
    p_end = jnp.exp(total)
    rr, kk_raw, vv, asig = r_ref[...], k_ref[...], v_ref[...], as_ref[...]
    kk_all = kk_raw * kk_ref[...]
    kdir = kk_raw * (1.0 + (asig - 1.0) * ka_ref[...])
    r_t, k_t, k_h = rr * e_in, kdir * e_ng, kdir * e_end

    def same_block(m):
        return (row // m) == (col // m)

    eye = (row == col).astype(F32)
    nt = (((1,), (1,)), ((), ()))
    tn = (((0,), (0,)), ((), ()))
    heads = range(hb)
    sls = [slice(h * n, (h + 1) * n) for h in heads]
    kkn = []
    for sl in sls:
        kkh = kk_all[:, sl]
        kkn.append(kkh * lax.rsqrt(jnp.maximum(jnp.sum(kkh * kkh, axis=-1, keepdims=True), 1e-24)))
    beta = [kkn[h] * asig[:, sls[h]] for h in heads]
    a_t = [(-kkn[h] * e_ex[:, sls[h]]).astype(BF16) for h in heads]
    b_t = [(beta[h] * e_ng[:, sls[h]]).astype(BF16) for h in heads]
    rt = [r_t[:, sl].astype(BF16) for sl in sls]
    kt = [k_t[:, sl].astype(BF16) for sl in sls]
    v = [vv[:, sl].astype(BF16) for sl in sls]
    dots = lambda a, b, dims: [lax.dot_general(a[h], b[h], dims, preferred_element_type=F32) for h in heads]
    l_ab = [jnp.where(strict, g, 0.0) for g in dots(a_t, b_t, nt)]
    l_ak = [jnp.where(strict, g, 0.0).astype(BF16) for g in dots(a_t, kt, nt)]
    m_rb = [jnp.where(incl, g, 0.0).astype(BF16) for g in dots(rt, b_t, nt)]
    m_rk = [jnp.where(incl, g, 0.0).astype(BF16) for g in dots(rt, kt, nt)]
    in_pair = same_block(2)
    x = [eye + jnp.where(in_pair, l, 0.0) for l in l_ab]
    m = 2
    while m < c:
        level = same_block(2 * m) & jnp.logical_not(same_block(m))
        xb = [xx.astype(BF16) for xx in x]
        l_m = [jnp.where(level, l, 0.0).astype(BF16) for l in l_ab]
        xl = [t.astype(BF16) for t in dots(xb, l_m, (((1,), (0,)), ((), ())))]
        x = [xx + t for xx, t in zip(x, dots(xl, xb, (((1,), (0,)), ((), ()))))]
        m *= 2
    mm = (((1,), (0,)), ((), ()))
    s0 = [st_ref[h] for h in heads]
    s0b = [s.astype(BF16) for s in s0]
    w = [p + q for p, q in zip(dots(a_t, s0b, nt), dots(l_ak, v, mm))]
    u = dots([xx.astype(BF16) for xx in x], [t.astype(BF16) for t in w], mm)
    ub = [t.astype(BF16) for t in u]
    y = [p + q + s for p, q, s in zip(dots(rt, s0b, nt), dots(m_rb, ub, mm), dots(m_rk, v, mm))]
    b_h = [(beta[h] * e_end[:, sls[h]]).astype(BF16) for h in heads]
    k_hb = [k_h[:, sl].astype(BF16) for sl in sls]
    s_u, s_v = dots(ub, b_h, tn), dots(v, k_hb, tn)
    for h in heads:
        st_ref[h] = s0[h] * p_end[:, sls[h]] + s_u[h] + s_v[h]
    y_ref[...] = jnp.concatenate(y, axis=-1)


def _rwkv_out_kernel(x_ref, y_ref, r_ref, k_ref, v_ref, g_ref, as_ref, ka_ref, rk_ref, lnx_ref, wo_ref, o_ref):
    d = x_ref.shape[1]
    n = B_HEAD_DIM
    y = y_ref[0] + y_ref[1]
    kd = k_ref[...] * (2.0 + (as_ref[0] + as_ref[1] - 2.0) * ka_ref[...])
    prod = r_ref[...] * kd * rk_ref[...]
    low = lax.broadcasted_iota(I32, (1, LANES), 1) < n

    def head_sum(a):
        s_lo = jnp.sum(jnp.where(low, a, 0.0), axis=-1, keepdims=True)
        s_hi = jnp.sum(jnp.where(low, 0.0, a), axis=-1, keepdims=True)
        return jnp.where(low, s_lo, s_hi)

    parts = []
    for p in range(d // LANES):
        sl = slice(p * LANES, (p + 1) * LANES)
        ys = y[:, sl]
        cen = ys - head_sum(ys) * (1.0 / n)
        var = head_sum(cen * cen) * (1.0 / n)
        yn = cen * lax.rsqrt(var + B_GN_EPS) * lnx_ref[0:1, sl] + lnx_ref[1:2, sl]
        bonus = head_sum(prod[:, sl]) * v_ref[:, sl]
        parts.append(((yn + bonus) * g_ref[:, sl]).astype(BF16))
    o_ref[...] = x_ref[...] + jnp.dot(jnp.concatenate(parts, axis=-1), wo_ref[...], preferred_element_type=F32)


def mixer_b(x, ln, mu, w_rkv, w0, w1, w2, a0, a1, a2, g1, g2, k_k, k_a, r_k, ln_x, wo, batch, s_len):
    t, d = x.shape
    bf = lambda a: a.astype(BF16)
    tm = _row_tile(s_len, 256)
    tiles_per_seq = s_len // tm
    halo = 8
    row = lambda n: pl.BlockSpec((tm, n), lambda i: (i, 0))
    row2 = lambda n: pl.BlockSpec((2, tm, n), lambda i: (0, i, 0))
    full = lambda a: pl.BlockSpec(a.shape, lambda i: (0,) * a.ndim)
    ln2 = ln.reshape(1, d)
    weights = (ln2, mu, bf(w_rkv), bf(g1), bf(g2), w0, bf(w1), bf(w2), a0, bf(a1), bf(a2))
    tok = jax.ShapeDtypeStruct((t, d), F32)
    tok2 = jax.ShapeDtypeStruct((2, t, d), F32)
    r, k, v, g, lw, asig = pl.pallas_call(
        functools.partial(_rwkv_prep_kernel, tiles_per_seq=tiles_per_seq),
        grid=(t // tm,),
        in_specs=[row(d),
                  pl.BlockSpec((halo, d), lambda i: (jnp.maximum(i * (tm // halo) - 1, 0), 0)),
                  pl.BlockSpec((halo, d), lambda i: (jnp.minimum((i + 1) * (tm // halo), t // halo - 1), 0))]
                 + [full(a) for a in weights],
        out_specs=[row(d), row(d), row(d), row(d), row2(d), row2(d)],
        out_shape=[tok, tok, tok, tok, tok2, tok2],
        compiler_params=_params("parallel"),
        name="rwkv_prep",
    )(x, x, x, *weights)

    hb = WKV_HEADS_PER_STEP
    width = hb * B_HEAD_DIM
    c = min(WKV_CHUNK, s_len)
    nc = s_len // c
    chunk_of = lambda dr, ci: ci + dr * (nc - 1 - 2 * ci)
    tokb = lambda: pl.BlockSpec((None, c, width), lambda b, dr, hg, ci: (b, chunk_of(dr, ci), hg))
    dirb = lambda: pl.BlockSpec((None, None, c, width), lambda b, dr, hg, ci: (dr, b, chunk_of(dr, ci), hg))
    par = lambda: pl.BlockSpec((1, width), lambda b, dr, hg, ci: (0, hg))
    v3 = lambda a: a.reshape(batch, s_len, d)
    v4 = lambda a: a.reshape(2, batch, s_len, d)
    y = pl.pallas_call(
        functools.partial(_wkv_kernel, hb=hb),
        grid=(batch, 2, d // width, nc),
        in_specs=[tokb(), tokb(), tokb(), dirb(), dirb(), par(), par()],
        out_specs=dirb(),
        out_shape=jax.ShapeDtypeStruct((2, batch, s_len, d), F32),
        scratch_shapes=[pltpu.VMEM((hb, B_HEAD_DIM, B_HEAD_DIM), F32)],
        compiler_params=_params("parallel", "parallel", "parallel", "arbitrary"),
        name="wkv_scan",
    )(v3(r), v3(k), v3(v), v4(lw), v4(asig), k_k.reshape(1, d), k_a.reshape(1, d))

    params = (k_a.reshape(1, d), r_k.reshape(1, d), ln_x, bf(wo))
    return pl.pallas_call(
        _rwkv_out_kernel,
        grid=(t // tm,),
        in_specs=[row(d), row2(d), row(d), row(d), row(d), row(d), row2(d)] + [full(a) for a in params],
        out_specs=row(d),
        out_shape=tok,
        compiler_params=_params("parallel"),
        name="rwkv_out",
    )(x, y.reshape(2, t, d), r, k, v, g, asig, *params)


TOP_K = 2
ROUTE_EXPERT, ROUTE_GATE, ROUTE_RANK = 0, TOP_K, 2 * TOP_K
MOE_ROW_TILE = 512


def _router_kernel(x_ref, g_ref, wr_ref, hn_ref, route_ref, cnt_ref, run_ref, *, n_exp):
    @pl.when(pl.program_id(0) == 0)
    def _():
        run_ref[...] = jnp.zeros_like(run_ref)

    tm = x_ref.shape[0]
    h = _rms(x_ref[...], g_ref[...])
    hn_ref[...] = h
    logits = jnp.dot(h, wr_ref[...], preferred_element_type=F32, precision=lax.Precision.HIGHEST)
    lane = lax.broadcasted_iota(I32, (tm, LANES), 1)
    logits = jnp.where(lane < n_exp, logits, NEG_INF)

    def top1(v):
        m = jnp.max(v, axis=-1, keepdims=True)
        return m, jnp.min(jnp.where(v == m, lane, LANES), axis=-1, keepdims=True)

    m1, i1 = top1(logits)
    m2, i2 = top1(jnp.where(lane == i1, NEG_INF, logits))
    e = jnp.exp(m2 - m1)
    g1 = 1.0 / (1.0 + e)
    g2 = e * g1
    sel = ((lane == i1) | (lane == i2)).astype(F32)
    earlier = (lax.broadcasted_iota(I32, (tm, tm), 0) > lax.broadcasted_iota(I32, (tm, tm), 1)).astype(BF16)
    before = jnp.dot(earlier, sel.astype(BF16), preferred_element_type=F32) + run_ref[...]
    r1 = jnp.sum(jnp.where(lane == i1, before, 0.0), axis=-1, keepdims=True)
    r2 = jnp.sum(jnp.where(lane == i2, before, 0.0), axis=-1, keepdims=True)
    run_ref[...] += jnp.sum(sel, axis=0, keepdims=True)
    cnt_ref[...] = run_ref[...]
    route = jnp.zeros((tm, LANES), F32)
    for off, vals in ((ROUTE_EXPERT, (i1.astype(F32), i2.astype(F32))), (ROUTE_GATE, (g1, g2)), (ROUTE_RANK, (r1, r2))):
        for kk, v in enumerate(vals):
            route = jnp.where(lane == off + kk, v, route)
    route_ref[...] = route


def _dispatch_kernel(dest_ref, hn_ref, xs_in_ref, xs_ref, sem):
    del xs_in_ref
    tm = hn_ref.shape[0]

    def row_copy(r, slot):
        return pltpu.make_async_copy(hn_ref.at[pl.ds(r, 1)], xs_ref.at[pl.ds(dest_ref[TOP_K * r + slot], 1)], sem)

    def start(r, c):
        for slot in range(TOP_K):
            row_copy(r, slot).start()
        return c

    def wait(r, c):
        for slot in range(TOP_K):
            row_copy(r, slot).wait()
        return c

    lax.fori_loop(0, tm, start, 0)
    lax.fori_loop(0, tm, wait, 0)


def _expert_ffn_kernel(te_ref, x_ref, w1_ref, w3_ref, w2_ref, o_ref, xb_ref, acc_ref):
    del te_ref
    j = pl.program_id(1)

    @pl.when(j == 0)
    def _():
        xb_ref[...] = x_ref[...].astype(BF16)
        acc_ref[...] = jnp.zeros_like(acc_ref)

    xb = xb_ref[...]
    gate = jnp.dot(xb, w1_ref[...], preferred_element_type=F32)
    up = jnp.dot(xb, w3_ref[...], preferred_element_type=F32)
    act = (gate * jax.nn.sigmoid(gate) * up).astype(BF16)
    acc_ref[...] += jnp.dot(act, w2_ref[...], preferred_element_type=F32)

    @pl.when(j == pl.num_programs(1) - 1)
    def _():
        o_ref[...] = acc_ref[...]


def _moe_combine_kernel(dest_ref, route_ref, x_ref, ys_ref, o_ref, buf_ref, sem):
    tm = x_ref.shape[0]

    def row_copy(r, slot):
        return pltpu.make_async_copy(ys_ref.at[pl.ds(dest_ref[TOP_K * r + slot], 1)], buf_ref.at[slot, pl.ds(r, 1)], sem)

    def start(r, c):
        for slot in range(TOP_K):
            row_copy(r, slot).start()
        return c

    def wait(r, c):
        for slot in range(TOP_K):
            row_copy(r, slot).wait()
        return c

    lax.fori_loop(0, tm, start, 0)
    lax.fori_loop(0, tm, wait, 0)
    route = route_ref[...]
    out = x_ref[...]
    for slot in range(TOP_K):
        out = out + route[:, ROUTE_GATE + slot:ROUTE_GATE + slot + 1] * buf_ref[slot]
    o_ref[...] = out


def moe_ffn(x, ln, router, w13, w2, tf=1792):
    t, d = x.shape
    n_exp, dff = w2.shape[0], w2.shape[1]
    assert dff % tf == 0
    nj = dff // tf
    tmx = min(MOE_ROW_TILE, t)
    tm = _row_tile(t, 512)
    wr = jnp.pad(router, ((0, 0), (0, LANES - n_exp)))
    hn, route, cnt = pl.pallas_call(
        functools.partial(_router_kernel, n_exp=n_exp),
        grid=(t // tm,),
        in_specs=[pl.BlockSpec((tm, d), lambda i: (i, 0)),
                  pl.BlockSpec((1, d), lambda i: (0, 0)),
                  pl.BlockSpec((d, LANES), lambda i: (0, 0))],
        out_specs=[pl.BlockSpec((tm, d), lambda i: (i, 0)),
                   pl.BlockSpec((tm, LANES), lambda i: (i, 0)),
                   pl.BlockSpec((1, LANES), lambda i: (0, 0))],
        out_shape=[jax.ShapeDtypeStruct((t, d), F32),
                   jax.ShapeDtypeStruct((t, LANES), F32),
                   jax.ShapeDtypeStruct((1, LANES), F32)],
        scratch_shapes=[pltpu.VMEM((1, LANES), F32)],
        compiler_params=_params("arbitrary"),
        name="moe_router",
    )(x, ln.reshape(1, d), wr)

    expert = route[:, ROUTE_EXPERT:ROUTE_EXPERT + TOP_K].astype(I32)
    rank = route[:, ROUTE_RANK:ROUTE_RANK + TOP_K].astype(I32)
    counts = cnt[0, :n_exp].astype(I32)
    padded = (counts + tmx - 1) // tmx * tmx
    ends = jnp.cumsum(padded)
    dest = ((ends - padded)[expert] + rank).reshape(TOP_K * t)
    n_tiles = (TOP_K * t) // tmx + n_exp
    rows = n_tiles * tmx
    tile_start = jnp.arange(n_tiles, dtype=I32) * tmx
    tile_expert = jnp.minimum(jnp.sum((tile_start[:, None] >= ends[None, :]).astype(I32), axis=1), n_exp - 1)

    smem_rows = lambda n: pl.BlockSpec((TOP_K * n,), lambda i: (i,), memory_space=pltpu.SMEM)
    xs = pl.pallas_call(
        _dispatch_kernel,
        grid=(t // tm,),
        in_specs=[smem_rows(tm),
                  pl.BlockSpec((tm, d), lambda i: (i, 0)),
                  pl.BlockSpec(memory_space=pl.ANY)],
        out_specs=pl.BlockSpec(memory_space=pl.ANY),
        out_shape=jax.ShapeDtypeStruct((rows, d), F32),
        scratch_shapes=[pltpu.SemaphoreType.DMA(())],
        input_output_aliases={2: 0},
        compiler_params=_params("arbitrary"),
        name="moe_dispatch",
    )(dest, hn, jnp.zeros((rows, d), F32))

    ys = pl.pallas_call(
        _expert_ffn_kernel,
        grid_spec=pltpu.PrefetchScalarGridSpec(
            num_scalar_prefetch=1,
            grid=(n_tiles, nj),
            in_specs=[pl.BlockSpec((tmx, d), lambda i, j, te: (i, 0)),
                      pl.BlockSpec((None, d, tf), lambda i, j, te: (te[i], 0, j)),
                      pl.BlockSpec((None, d, tf), lambda i, j, te: (te[i], 0, nj + j)),
                      pl.BlockSpec((None, tf, d), lambda i, j, te: (te[i], j, 0))],
            out_specs=pl.BlockSpec((tmx, d), lambda i, j, te: (i, 0)),
            scratch_shapes=[pltpu.VMEM((tmx, d), BF16), pltpu.VMEM((tmx, d), F32)]),
        out_shape=jax.ShapeDtypeStruct((rows, d), F32),
        compiler_params=_params("parallel", "arbitrary"),
        name="moe_expert_ffn",
    )(tile_expert, xs, w13, w13, w2)

    return pl.pallas_call(
        _moe_combine_kernel,
        grid=(t // tm,),
        in_specs=[smem_rows(tm),
                  pl.BlockSpec((tm, LANES), lambda i: (i, 0)),
                  pl.BlockSpec((tm, d), lambda i: (i, 0)),
                  pl.BlockSpec(memory_space=pl.ANY)],
        out_specs=pl.BlockSpec((tm, d), lambda i: (i, 0)),
        out_shape=jax.ShapeDtypeStruct((t, d), F32),
        scratch_shapes=[pltpu.VMEM((TOP_K, tm, d), F32), pltpu.SemaphoreType.DMA(())],
        compiler_params=_params("arbitrary"),
        name="moe_combine",
    )(dest, route, x, ys)


def kernel(x, positions, ln_mix, ln_ffn, a_wqkv, a_qk_norm, a_wo, b_mu, b_wrkv, b_w0, b_w1, b_w2, b_a0, b_a1, b_a2, b_g1, b_g2, b_kk, b_ka, b_rk, b_lnx, b_wo, c_wa, c_qa_norm, c_kva_norm, c_wq_b, c_wkv_b, c_qk_norm, c_wo, f_w13, f_w2, m_router, m_w13, m_w2):
    batch, s_len, d = x.shape
    t = batch * s_len
    cs, sn = _rope_tables(positions)
    cs, sn = cs.reshape(t, LANES), sn.reshape(t, LANES)
    h = x.reshape(t, d)
    n_mixers = 3
    for i in range(ln_mix.shape[0]):
        j, kind = divmod(i, n_mixers)
        if kind == 0:
            h = mixer_a(h, ln_mix[i], a_wqkv[j], a_qk_norm[j], a_wo[j], cs, sn, batch, s_len)
        elif kind == 1:
            h = mixer_b(h, ln_mix[i], b_mu[j], b_wrkv[j], b_w0[j], b_w1[j], b_w2[j], b_a0[j], b_a1[j], b_a2[j],
                        b_g1[j], b_g2[j], b_kk[j], b_ka[j], b_rk[j], b_lnx[j], b_wo[j], batch, s_len)
        else:
            h = mixer_c(h, ln_mix[i], c_wa[j], c_qa_norm[j], c_kva_norm[j], c_wq_b[j], c_wkv_b[j], c_qk_norm[j],
                        c_wo[j], cs, sn, batch, s_len)
        f = i // 2
        if i % 2 == 0:
            h = dense_ffn(h, ln_ffn[i], f_w13[f].astype(BF16), f_w2[f].astype(BF16))
        else:
            h = moe_ffn(h, ln_ffn[i], m_router[f], m_w13[f].astype(BF16), m_w2[f].astype(BF16))
    return h.reshape(batch, s_len, d)
```

```python
import functools
import math

import jax
import jax.numpy as jnp
from jax import lax
from jax.experimental import pallas as pl
from jax.experimental.pallas import tpu as pltpu

F32 = jnp.float32
BF16 = jnp.bfloat16
I32 = jnp.int32

NORM_EPS = 1e-6
ROPE_THETA = 500000.0
NEG_INF = -1e30
LANES = 128
ROT_DIM = 32
ROT_HALF = ROT_DIM // 2
DILATED_PAIRS = ((128, 1), (512, 4), (2048, 16))
A_HEAD_DIM = 128
VMEM_LIMIT = 48 * 1024 * 1024


def _params(*sem):
    return pltpu.CompilerParams(dimension_semantics=sem, vmem_limit_bytes=VMEM_LIMIT)


def _rms(x, g):
    return x * lax.rsqrt(jnp.mean(x * x, axis=-1, keepdims=True) + NORM_EPS) * g


def _row_tile(t, want):
    tm = min(t, want)
    assert t % tm == 0
    return tm


def _norm_matmul_kernel(x_ref, g_ref, w_ref, o_ref, xn_ref):
    @pl.when(pl.program_id(1) == 0)
    def _():
        xn_ref[...] = _rms(x_ref[...], g_ref[...]).astype(BF16)

    o_ref[...] = jnp.dot(xn_ref[...], w_ref[...], preferred_element_type=F32).astype(o_ref.dtype)


def norm_matmul(x, g, w, out_dtype, tn):
    t, k = x.shape
    n = w.shape[1]
    tm = _row_tile(t, 1024)
    assert n % tn == 0
    return pl.pallas_call(
        _norm_matmul_kernel,
        grid=(t // tm, n // tn),
        in_specs=[pl.BlockSpec((tm, k), lambda i, j: (i, 0)),
                  pl.BlockSpec((1, k), lambda i, j: (0, 0)),
                  pl.BlockSpec((k, tn), lambda i, j: (0, j))],
        out_specs=pl.BlockSpec((tm, tn), lambda i, j: (i, j)),
        out_shape=jax.ShapeDtypeStruct((t, n), out_dtype),
        scratch_shapes=[pltpu.VMEM((tm, k), BF16)],
        compiler_params=_params("parallel", "arbitrary"),
        name="norm_matmul",
    )(x, g.reshape(1, k), w)


def _matmul_res_kernel(a_ref, w_ref, x_ref, o_ref):
    o_ref[...] = x_ref[...] + jnp.dot(a_ref[...], w_ref[...], preferred_element_type=F32)


def matmul_residual(a, w, x):
    t, k = a.shape
    n = w.shape[1]
    tm = _row_tile(t, 1024)
    return pl.pallas_call(
        _matmul_res_kernel,
        grid=(t // tm,),
        in_specs=[pl.BlockSpec((tm, k), lambda i: (i, 0)),
                  pl.BlockSpec((k, n), lambda i: (0, 0)),
                  pl.BlockSpec((tm, n), lambda i: (i, 0))],
        out_specs=pl.BlockSpec((tm, n), lambda i: (i, 0)),
        out_shape=jax.ShapeDtypeStruct((t, n), F32),
        compiler_params=_params("parallel"),
        name="matmul_residual",
    )(a, w, x)


def _rope_tables(positions):
    b, s = positions.shape
    inv_freq = ROPE_THETA ** (-jnp.arange(0, ROT_DIM, 2, dtype=F32) / ROT_DIM)
    ang = positions.astype(F32)[..., None] * inv_freq
    c, sn = jnp.cos(ang), jnp.sin(ang)
    rest = (b, s, LANES - ROT_DIM)
    cs_t = jnp.concatenate([c, c, jnp.ones(rest, F32)], axis=-1)
    sn_t = jnp.concatenate([-sn, sn, jnp.zeros(rest, F32)], axis=-1)
    return cs_t, sn_t


def _rope(x, cs, sn, lane):
    n = x.shape[-1]
    partner = jnp.where(lane < ROT_HALF, pltpu.roll(x, n - ROT_HALF, 1), pltpu.roll(x, ROT_HALF, 1))
    return x * cs + partner * sn


def _dil_attn_kernel(q_ref, k_ref, v_ref, o_ref, lse_ref, *, seq, qt, kw, half, heads):
    hd = A_HEAD_DIM
    qi = pl.program_id(2)
    q0 = pl.multiple_of(qi * qt, qt)
    start = jnp.clip(q0 - half, 0, seq - kw)
    start = pl.multiple_of(start, math.gcd(half, qt))
    qpos = q0 + lax.broadcasted_iota(I32, (qt, kw), 0)
    kpos = start + lax.broadcasted_iota(I32, (qt, kw), 1)
    valid = jnp.abs(qpos - kpos) <= half
    head_of_lane = lax.broadcasted_iota(I32, (1, LANES), 1) // (LANES // heads)
    sls = [slice(h * hd, (h + 1) * hd) for h in range(heads)]
    ones = jnp.ones((kw, hd), BF16)
    s = [lax.dot_general(q_ref[0, :, sl], k_ref[0, pl.ds(start, kw), sl], (((1,), (1,)), ((), ())),
                         preferred_element_type=F32) for sl in sls]
    s = [jnp.where(valid, t, NEG_INF) for t in s]
    m = [jnp.max(t, axis=-1, keepdims=True) for t in s]
    p = [jnp.exp(t - mm).astype(BF16) for t, mm in zip(s, m)]
    ol = [jnp.dot(p[h], jnp.concatenate([v_ref[0, pl.ds(start, kw), sls[h]], ones], axis=1),
                  preferred_element_type=F32) for h in range(heads)]
    lse_row = jnp.zeros((qt, LANES), F32)
    for h in range(heads):
        o_ref[0, :, sls[h]] = (ol[h][:, :hd] / ol[h][:, hd:]).astype(o_ref.dtype)
        lse_row = jnp.where(head_of_lane == h, m[h] + jnp.log(ol[h][:, hd:hd + 1]), lse_row)
    lse_ref[0] = lse_row


A_PAIR = 2 * A_HEAD_DIM


def _head_helper_matrices():
    i = jnp.arange(A_PAIR)
    same_head = (i[:, None] // A_HEAD_DIM) == (i[None, :] // A_HEAD_DIM)
    dst = i % A_HEAD_DIM
    src = jnp.where(dst < ROT_HALF, i + ROT_HALF, i - ROT_HALF)
    perm = (i[:, None] == src[None, :]) & (dst[None, :] < ROT_DIM)
    return same_head.astype(BF16), perm.astype(BF16)


def _qkv_group_kernel(x_ref, g_ref, w_ref, cs_ref, sn_ref, gain_ref, ones_ref, perm_ref, o_ref, xn_ref, acc_ref, *, dil):
    j = pl.program_id(1)

    @pl.when(j == 0)
    def _():
        xn_ref[...] = _rms(x_ref[...], g_ref[...]).astype(BF16)

    acc = jnp.dot(xn_ref[...], w_ref[...], preferred_element_type=F32)
    tm, width = acc.shape

    def emit(vals):
        if dil == 1:
            o_ref[0, 0] = vals.astype(o_ref.dtype)
            return
        chunks = width // LANES
        for c in range(chunks):
            acc_ref[c] = vals[:, c * LANES:(c + 1) * LANES]
        for r in range(dil):
            o_ref[0, r] = jnp.concatenate([acc_ref[c, pl.ds(r, tm // dil, stride=dil), :] for c in range(chunks)],
                                          axis=-1).astype(o_ref.dtype)

    @pl.when(j < 2)
    def _():
        cs = jnp.concatenate([cs_ref[...]] * 2, axis=1)
        sn = jnp.concatenate([sn_ref[...]] * 2, axis=1)
        gain = gain_ref[pl.ds(j, 1), :]
        post = jnp.where(j == 0, A_HEAD_DIM ** -0.5, 1.0)
        parts = []
        for p in range(width // A_PAIR):
            a = acc[:, p * A_PAIR:(p + 1) * A_PAIR]
            sumsq = jnp.dot((a * a).astype(BF16), ones_ref[...], preferred_element_type=F32)
            ag = a * gain[:, p * A_PAIR:(p + 1) * A_PAIR]
            pair = jnp.dot(ag.astype(BF16), perm_ref[...], preferred_element_type=F32)
            parts.append((ag * cs + pair * sn) * (lax.rsqrt(sumsq * (1.0 / A_HEAD_DIM) + NORM_EPS) * post))
        emit(jnp.concatenate(parts, axis=1))

    @pl.when(j == 2)
    def _():
        emit(acc)


def qkv_group_projection(x, ln, wqkv, cs, sn, gq, gk, group, batch, s_len, width):
    t, d = x.shape
    dil = DILATED_PAIRS[group][1]
    n_groups = len(DILATED_PAIRS)
    heads = width // A_HEAD_DIM
    tm = _row_tile(s_len, 1024)
    tiles = s_len // tm
    assert tm % (16 * dil) == 0
    gains = jnp.stack([jnp.tile(gq, heads), jnp.tile(gk, heads)])
    ones, perm = _head_helper_matrices()
    const = lambda a: pl.BlockSpec(a.shape, lambda i, j: (0, 0))
    return pl.pallas_call(
        functools.partial(_qkv_group_kernel, dil=dil),
        grid=(t // tm, 3),
        in_specs=[pl.BlockSpec((tm, d), lambda i, j: (i, 0)),
                  pl.BlockSpec((1, d), lambda i, j: (0, 0)),
                  pl.BlockSpec((d, width), lambda i, j: (0, j * n_groups + group)),
                  pl.BlockSpec((tm, LANES), lambda i, j: (i, 0)),
                  pl.BlockSpec((tm, LANES), lambda i, j: (i, 0)),
                  const(gains), const(ones), const(perm)],
        out_specs=pl.BlockSpec((1, dil, tm // dil, width), lambda i, j: (i // tiles, 0, i % tiles, j)),
        out_shape=jax.ShapeDtypeStruct((batch, dil, s_len // dil, 3 * width), BF16),
        scratch_shapes=[pltpu.VMEM((tm, d), BF16), pltpu.VMEM((width // LANES, tm, LANES), F32)],
        compiler_params=_params("parallel", "arbitrary"),
        name=f"qkv_projection_g{group}",
    )(x, ln.reshape(1, d), wqkv, cs, sn, gains, ones, perm)


def dilated_group_attention(qkv, group, batch, s_len, heads):
    window, dil = DILATED_PAIRS[group]
    half = window // (2 * dil)
    seq = s_len // dil
    width = heads * A_HEAD_DIM
    qt = min(128, seq)
    kw = min(qt + 2 * half, seq)
    assert seq % qt == 0 and (seq - kw) % math.gcd(half, qt) == 0
    whole = lambda c: pl.BlockSpec((None, 1, seq, width), lambda b, r, qi: (b, r, 0, c))
    tile = lambda n: pl.BlockSpec((None, 1, qt, n), lambda b, r, qi: (b, r, qi, 0))
    kern = functools.partial(_dil_attn_kernel, seq=seq, qt=qt, kw=kw, half=half, heads=heads)
    return pl.pallas_call(
        kern,
        grid=(batch, dil, seq // qt),
        in_specs=[tile(width), whole(1), whole(2)],
        out_specs=[tile(width), tile(LANES)],
        out_shape=[jax.ShapeDtypeStruct((batch, dil, seq, width), BF16),
                   jax.ShapeDtypeStruct((batch, dil, seq, LANES), F32)],
        compiler_params=_params("parallel", "parallel", "parallel"),
        name=f"dilated_attention_g{group}",
    )(qkv, qkv, qkv)


def _combine_wo_kernel(x_ref, o0_ref, o1_ref, o2_ref, l0_ref, l1_ref, l2_ref, w_ref, out_ref, ot_ref, lt_ref, *, heads):
    hd = A_HEAD_DIM
    rep = LANES // heads
    tm = x_ref.shape[0]

    def token_order(src_ref, dst_ref, g):
        dil = src_ref.shape[0]
        if dil == 1:
            return src_ref[0].astype(F32)
        chunks = src_ref.shape[-1] // LANES
        for r in range(dil):
            blk = src_ref[r].astype(F32)
            for c in range(chunks):
                dst_ref[g, c, pl.ds(r, tm // dil, stride=dil), :] = blk[:, c * LANES:(c + 1) * LANES]
        return jnp.concatenate([dst_ref[g, c] for c in range(chunks)], axis=-1)

    ls = [token_order(ref, lt_ref, g) for g, ref in enumerate((l0_ref, l1_ref, l2_ref))]
    os_ = [token_order(ref, ot_ref, g) for g, ref in enumerate((o0_ref, o1_ref, o2_ref))]
    m = jnp.maximum(jnp.maximum(ls[0], ls[1]), ls[2])
    es = [jnp.exp(l - m) for l in ls]
    inv = 1.0 / (es[0] + es[1] + es[2])
    al = [e * inv for e in es]
    parts = []
    for h in range(heads):
        sl = slice(h * hd, (h + 1) * hd)
        c = slice(h * rep, h * rep + 1)
        parts.append((al[0][:, c] * os_[0][:, sl] + al[1][:, c] * os_[1][:, sl] + al[2][:, c] * os_[2][:, sl]).astype(BF16))
    o = jnp.concatenate(parts, axis=-1)
    out_ref[...] = x_ref[...] + jnp.dot(o, w_ref[...], preferred_element_type=F32)


def combine_wo(x, outs, lses, wo, heads, batch, s_len):
    t, d = x.shape
    width = wo.shape[0]
    n_groups = len(DILATED_PAIRS)
    tm = _row_tile(s_len, 512)
    tiles = s_len // tm
    assert all(tm % (16 * dil) == 0 for _, dil in DILATED_PAIRS)
    row = lambda n: pl.BlockSpec((tm, n), lambda i: (i, 0))
    res = lambda g, n: pl.BlockSpec((None, DILATED_PAIRS[g][1], tm // DILATED_PAIRS[g][1], n),
                                    lambda i: (i // tiles, 0, i % tiles, 0))
    return pl.pallas_call(
        functools.partial(_combine_wo_kernel, heads=heads),
        grid=(t // tm,),
        in_specs=[row(d)] + [res(g, width) for g in range(n_groups)] + [res(g, LANES) for g in range(n_groups)]
                 + [pl.BlockSpec((width, d), lambda i: (0, 0))],
        out_specs=row(d),
        out_shape=jax.ShapeDtypeStruct((t, d), F32),
        scratch_shapes=[pltpu.VMEM((n_groups, width // LANES, tm, LANES), F32),
                        pltpu.VMEM((n_groups, 1, tm, LANES), F32)],
        compiler_params=_params("parallel"),
        name="combine_wo",
    )(x, *outs, *lses, wo)


def mixer_a(x, ln, wqkv, qk_norm, wo, cs, sn, batch, s_len):
    heads = wo.shape[0] // A_HEAD_DIM
    wqkv = wqkv.astype(BF16)
    outs, lses = [], []
    for g in range(len(DILATED_PAIRS)):
        qkv = qkv_group_projection(x, ln, wqkv, cs, sn, qk_norm[0, g], qk_norm[1, g], g, batch, s_len, wo.shape[0])
        o, lse = dilated_group_attention(qkv, g, batch, s_len, heads)
        outs.append(o)
        lses.append(lse)
    return combine_wo(x, outs, lses, wo.astype(BF16), heads, batch, s_len)


def _ffn_kernel(x_ref, g_ref, w1_ref, w3_ref, w2_ref, o_ref, xn_ref, acc_ref):
    j = pl.program_id(1)

    @pl.when(j == 0)
    def _():
        xn_ref[...] = _rms(x_ref[...], g_ref[...]).astype(BF16)
        acc_ref[...] = jnp.zeros_like(acc_ref)

    xn = xn_ref[...]
    gate = jnp.dot(xn, w1_ref[...], preferred_element_type=F32)
    up = jnp.dot(xn, w3_ref[...], preferred_element_type=F32)
    act = (gate * jax.nn.sigmoid(gate) * up).astype(BF16)
    acc_ref[...] += jnp.dot(act, w2_ref[...], preferred_element_type=F32)

    @pl.when(j == pl.num_programs(1) - 1)
    def _():
        o_ref[...] = x_ref[...] + acc_ref[...]


def dense_ffn(x, ln, w13, w2, tf=1408):
    t, d = x.shape
    dff = w2.shape[0]
    assert dff % tf == 0
    nj = dff // tf
    tm = _row_tile(t, 512)
    return pl.pallas_call(
        _ffn_kernel,
        grid=(t // tm, nj),
        in_specs=[pl.BlockSpec((tm, d), lambda i, j: (i, 0)),
                  pl.BlockSpec((1, d), lambda i, j: (0, 0)),
                  pl.BlockSpec((d, tf), lambda i, j: (0, j)),
                  pl.BlockSpec((d, tf), lambda i, j: (0, nj + j)),
                  pl.BlockSpec((tf, d), lambda i, j: (j, 0))],
        out_specs=pl.BlockSpec((tm, d), lambda i, j: (i, 0)),
        out_shape=jax.ShapeDtypeStruct((t, d), F32),
        scratch_shapes=[pltpu.VMEM((tm, d), BF16), pltpu.VMEM((tm, d), F32)],
        compiler_params=_params("parallel", "arbitrary"),
        name="dense_ffn",
    )(x, ln.reshape(1, d), w13, w13, w2)


C_ROPE = ROT_DIM
C_NOPE = 64
C_V = 64
C_QK = C_ROPE + C_NOPE


def _mla_qkv_kernel(lat_ref, qa_ref, kva_ref, gq_ref, gk_ref, wq_ref, wkv_ref, cs_ref, sn_ref,
                    q_ref, k_ref, v_ref, *, heads, q_lora, kv_lora):
    lane = lax.broadcasted_iota(I32, (1, LANES), 1)
    cs, sn = cs_ref[...], sn_ref[...]
    lat = lat_ref[...]
    qn = _rms(lat[:, :q_lora], qa_ref[...]).astype(BF16)
    kvn = _rms(lat[:, q_lora:q_lora + kv_lora], kva_ref[...]).astype(BF16)
    k_rope = lat[:, q_lora + kv_lora:]
    q = jnp.dot(qn, wq_ref[...], preferred_element_type=F32)
    kv = jnp.dot(kvn, wkv_ref[...], preferred_element_type=F32)
    scale = C_QK ** -0.5

    def head_norm(xh, g):
        ms = jnp.sum(xh * xh, axis=-1, keepdims=True) * (1.0 / C_QK)
        return _rope(xh * lax.rsqrt(ms + NORM_EPS) * g, cs, sn, lane)

    for h in range(heads):
        sl = slice(h * LANES, (h + 1) * LANES)
        q_ref[:, sl] = (head_norm(q[:, sl], gq_ref[...]) * scale).astype(BF16)
        k_ref[:, sl] = head_norm(kv[:, sl] + k_rope, gk_ref[...]).astype(BF16)
    v_ref[...] = kv[:, heads * LANES:].astype(BF16)


def _mla_attn_kernel(q_ref, k_ref, v_ref, o_ref, *, pack):
    outs = []
    for i in range(pack):
        q = q_ref[0, :, i * LANES:(i + 1) * LANES]
        k = k_ref[0, :, i * LANES:(i + 1) * LANES]
        s = lax.dot_general(q, k, (((1,), (1,)), ((), ())), preferred_element_type=F32)
        m = jnp.max(s, axis=-1, keepdims=True)
        p = jnp.exp(s - m)
        l = jnp.sum(p, axis=-1, keepdims=True)
        v = v_ref[0, :, i * C_V:(i + 1) * C_V]
        outs.append(jnp.dot(p.astype(BF16), v, preferred_element_type=F32) / l)
    o_ref[0] = jnp.concatenate(outs, axis=-1).astype(o_ref.dtype)


def _pad_cols(w, heads, real, slot):
    k = w.shape[0]
    w = w.reshape(k, heads, real)
    return jnp.pad(w, ((0, 0), (0, 0), (0, slot - real))).reshape(k, heads * slot)


def mixer_c(x, ln, wa, qa_norm, kva_norm, wq_b, wkv_b, qk_norm, wo, cs, sn, batch, s_len):
    t, d = x.shape
    q_lora, kv_lora = qa_norm.shape[0], kva_norm.shape[0]
    heads = wo.shape[0] // C_V
    lat_w = q_lora + kv_lora + LANES
    wa_p = jnp.pad(wa, ((0, 0), (0, lat_w - wa.shape[1]))).astype(BF16)
    lat = norm_matmul(x, ln, wa_p, F32, tn=lat_w)
    wq_p = _pad_cols(wq_b, heads, C_QK, LANES).astype(BF16)
    wkv = wkv_b.reshape(kv_lora, heads, C_NOPE + C_V)
    wk_p = jnp.pad(wkv[:, :, :C_NOPE], ((0, 0), (0, 0), (C_ROPE, LANES - C_QK))).reshape(kv_lora, heads * LANES)
    wv = wkv[:, :, C_NOPE:].reshape(kv_lora, heads * C_V)
    wkv_p = jnp.concatenate([wk_p, wv], axis=1).astype(BF16)
    gq = jnp.pad(qk_norm[0], (0, LANES - C_QK)).reshape(1, LANES)
    gk = jnp.pad(qk_norm[1], (0, LANES - C_QK)).reshape(1, LANES)
    tm = _row_tile(t, 512)
    row = lambda n: pl.BlockSpec((tm, n), lambda i: (i, 0))
    full = lambda a: pl.BlockSpec(a.shape, lambda i: (0, 0))
    qa2, kva2 = qa_norm.reshape(1, -1), kva_norm.reshape(1, -1)
    q, k, v = pl.pallas_call(
        functools.partial(_mla_qkv_kernel, heads=heads, q_lora=q_lora, kv_lora=kv_lora),
        grid=(t // tm,),
        in_specs=[row(lat_w), full(qa2), full(kva2), full(gq), full(gk), full(wq_p), full(wkv_p),
                  row(LANES), row(LANES)],
        out_specs=[row(heads * LANES), row(heads * LANES), row(heads * C_V)],
        out_shape=[jax.ShapeDtypeStruct((t, heads * LANES), BF16),
                   jax.ShapeDtypeStruct((t, heads * LANES), BF16),
                   jax.ShapeDtypeStruct((t, heads * C_V), BF16)],
        compiler_params=_params("parallel"),
        name="mla_qkv",
    )(lat, qa2, kva2, gq, gk, wq_p, wkv_p, cs, sn)
    pack = LANES // C_V
    tq = min(256, s_len)
    o = pl.pallas_call(
        functools.partial(_mla_attn_kernel, pack=pack),
        grid=(batch, heads // pack, s_len // tq),
        in_specs=[pl.BlockSpec((1, tq, pack * LANES), lambda b, h, i: (b, i, h)),
                  pl.BlockSpec((1, s_len, pack * LANES), lambda b, h, i: (b, 0, h)),
                  pl.BlockSpec((1, s_len, pack * C_V), lambda b, h, i: (b, 0, h))],
        out_specs=pl.BlockSpec((1, tq, pack * C_V), lambda b, h, i: (b, i, h)),
        out_shape=jax.ShapeDtypeStruct((batch, s_len, heads * C_V), BF16),
        compiler_params=_params("parallel", "parallel", "arbitrary"),
        name="mla_attention",
    )(q.reshape(batch, s_len, -1), k.reshape(batch, s_len, -1), v.reshape(batch, s_len, -1))
    return matmul_residual(o.reshape(t, heads * C_V), wo.astype(BF16), x)


B_HEAD_DIM = 64
B_DECAY_SCALE = math.exp(-0.5)
B_GN_EPS = 64e-5
WKV_CHUNK = 64
WKV_HEADS_PER_STEP = 16
WKV_PACK = 4


def _bdot(a, b):
    return jnp.dot(a.astype(BF16), b.astype(BF16), preferred_element_type=F32)


def _rwkv_prep_kernel(x_ref, xp_ref, xn_ref, ln_ref, mu_ref, wrkv_ref, g1_ref, g2_ref, w0_ref, w1_ref, w2_ref,
                      a0_ref, a1_ref, a2_ref, r_ref, k_ref, v_ref, g_ref, lw_ref, as_ref, *, tiles_per_seq):
    i = pl.program_id(0)
    tm = x_ref.shape[0]
    ln = ln_ref[...]
    h = _rms(x_ref[...], ln)
    pos = i % tiles_per_seq
    prev_row = jnp.where(pos == 0, 0.0, _rms(xp_ref[7:8, :], ln))
    next_row = jnp.where(pos == tiles_per_seq - 1, 0.0, _rms(xn_ref[0:1, :], ln))
    row = lax.broadcasted_iota(I32, (tm, 1), 0)
    d_f = jnp.where(row == 0, prev_row, pltpu.roll(h, 1, 0)) - h
    d_b = jnp.where(row == tm - 1, next_row, pltpu.roll(h, tm - 1, 0)) - h
    d_c = 0.5 * (d_f + d_b)
    mu = mu_ref[...]
    mix = lambda dd, n: h + dd * mu[n:n + 1]
    r_ref[...] = _bdot(mix(d_c, 0), wrkv_ref[0])
    k_ref[...] = _bdot(mix(d_c, 1), wrkv_ref[1])
    v_ref[...] = _bdot(mix(d_c, 2), wrkv_ref[2])
    g_ref[...] = _bdot(jax.nn.sigmoid(_bdot(mix(d_c, 3), g1_ref[...])), g2_ref[...])
    for dr, dd in enumerate((d_f, d_b)):
        z = w0_ref[dr:dr + 1, :] + _bdot(jnp.tanh(_bdot(mix(dd, 4 + dr), w1_ref[dr])), w2_ref[dr])
        lw_ref[dr] = -B_DECAY_SCALE * jax.nn.sigmoid(z)
        aa = a0_ref[dr:dr + 1, :] + _bdot(_bdot(mix(dd, 6 + dr), a1_ref[dr]), a2_ref[dr])
        as_ref[dr] = jax.nn.sigmoid(aa)


def _wkv_kernel(rf_ref, kf_ref, vf_ref, rb_ref, kb_ref, vb_ref, lwf_ref, lwb_ref, asf_ref, asb_ref, kk_ref, ka_ref,
                yf_ref, yb_ref, st_ref, *, hb):
    n = B_HEAD_DIM
    c = rf_ref.shape[0]

    @pl.when(pl.program_id(1) == 0)
    def _():
        st_ref[...] = jnp.zeros_like(st_ref)

    nt = (((1,), (1,)), ((), ()))
    tn = (((0,), (0,)), ((), ()))
    mm = (((1,), (0,)), ((), ()))
    low = lax.broadcasted_iota(I32, (1, LANES), 1) < n
    pk = WKV_PACK
    gw = pk * n
    per_dir = hb // pk
    gsl = [slice(g * gw, (g + 1) * gw) for g in range(per_dir)]
    cut = lambda a: [a[:, sl] for sl in gsl]
    row_p = lax.broadcasted_iota(I32, (c, gw), 0)
    col_p = lax.broadcasted_iota(I32, (c, gw), 1) % n

    def operands(sgn, r_ref, k_ref, v_ref, lw_ref, as_ref):
        ahead = (lax.broadcasted_iota(I32, (c, c), 0) - lax.broadcasted_iota(I32, (c, c), 1)) * sgn
        lw = lw_ref[...]
        cl = jnp.dot((ahead >= 0).astype(F32), lw, preferred_element_type=F32, precision=lax.Precision.HIGHEST)
        total = jnp.sum(lw, axis=0, keepdims=True)
        e_in, e_ex, e_ng, e_end = jnp.exp(cl), jnp.exp(cl - lw), jnp.exp(-cl), jnp.exp(total - cl)
        rr, kk_raw, vv, asig = r_ref[...], k_ref[...], v_ref[...], as_ref[...]
        kk_all = kk_raw * kk_ref[...]
        kdir = kk_raw * (1.0 + (asig - 1.0) * ka_ref[...])
        sq = kk_all * kk_all
        norms = []
        for p in range(sq.shape[1] // LANES):
            slab = sq[:, p * LANES:(p + 1) * LANES]
            s_lo = jnp.sum(jnp.where(low, slab, 0.0), axis=-1, keepdims=True)
            s_hi = jnp.sum(jnp.where(low, 0.0, slab), axis=-1, keepdims=True)
            norms.append(jnp.where(low, s_lo, s_hi))
        kkn_all = kk_all * lax.rsqrt(jnp.maximum(jnp.concatenate(norms, axis=-1), 1e-24))
        beta_all = kkn_all * asig
        ahead_p = (row_p - col_p) * sgn
        return dict(
            a_t=cut((-kkn_all * e_ex).astype(BF16)), b_t=cut((beta_all * e_ng).astype(BF16)),
            rt=cut((rr * e_in).astype(BF16)), kt=cut((kdir * e_ng).astype(BF16)), v=cut(vv.astype(BF16)),
            b_h=cut((beta_all * e_end).astype(BF16)), k_hb=cut((kdir * e_end).astype(BF16)),
            p_end=cut(jnp.exp(total)), strict=[ahead_p > 0] * per_dir, incl=[ahead_p >= 0] * per_dir)

    fwd = operands(1, rf_ref, kf_ref, vf_ref, lwf_ref, asf_ref)
    bwd = operands(-1, rb_ref, kb_ref, vb_ref, lwb_ref, asb_ref)
    both = {key: fwd[key] + bwd[key] for key in fwd}
    a_t, b_t, rt, kt, v, b_h, k_hb = (both[key] for key in ("a_t", "b_t", "rt", "kt", "v", "b_h", "k_hb"))
    p_end, strict_p, incl_p = both["p_end"], both["strict"], both["incl"]
    groups = range(2 * per_dir)

    diag_block = ((lax.broadcasted_iota(I32, (gw, gw), 0) // n)
                  == (lax.broadcasted_iota(I32, (gw, gw), 1) // n))
    diag_bf = diag_block.astype(BF16)

    def bd(p):
        return jnp.concatenate([p] * pk, axis=0) * diag_bf

    def fold(full):
        kept = jnp.where(diag_block, full, 0.0)
        out = kept[:n]
        for h in range(1, pk):
            out = out + kept[h * n:(h + 1) * n]
        return out

    eye_p = (row_p == col_p).astype(F32)
    same_block = lambda m: (row_p // m) == (col_p // m)
    dots = lambda a, b, dims: [lax.dot_general(p, q, dims, preferred_element_type=F32) for p, q in zip(a, b)]
    rows = lambda *parts: [jnp.concatenate(list(ps), axis=0) for ps in zip(*parts)]
    bf = lambda xs: [t.astype(BF16) for t in xs]
    bds = lambda xs: [bd(t) for t in xs]
    gram = dots(rows(a_t, rt), rows(bds(b_t), bds(kt)), nt)
    l_ab = [jnp.where(s, g[:c, :gw], 0.0) for s, g in zip(strict_p, gram)]
    l_ak = [jnp.where(s, g[:c, gw:], 0.0).astype(BF16) for s, g in zip(strict_p, gram)]
    m_r = [jnp.concatenate([jnp.where(i, g[c:, :gw], 0.0), jnp.where(i, g[c:, gw:], 0.0)],
                           axis=1).astype(BF16) for i, g in zip(incl_p, gram)]
    in_pair = same_block(2)
    x = [eye_p + jnp.where(in_pair, l, 0.0) for l in l_ab]
    m = 2
    while m < c:
        level = same_block(2 * m) & jnp.logical_not(same_block(m))
        xb = bf(x)
        l_m = [jnp.where(level, l, 0.0).astype(BF16) for l in l_ab]
        xl = bf(dots(xb, bds(l_m), mm))
        x = [xx + t for xx, t in zip(x, dots(xl, bds(xb), mm))]
        m *= 2
    s0 = [st_ref[g] for g in groups]
    bd_s = bds(bf(s0))
    bd_v = bds(v)
    w = [p + q for p, q in zip(dots(a_t, bd_s, nt), dots(l_ak, bd_v, mm))]
    u = bf(dots(bf(x), bds(bf(w)), mm))
    y = [p + q for p, q in zip(dots(rt, bd_s, nt), dots(m_r, rows(bds(u), bd_v), mm))]
    s_new = dots(rows(u, v), rows(b_h, k_hb), tn)
    for g in groups:
        st_ref[g] = s0[g] * p_end[g] + fold(s_new[g])
        out_ref = yf_ref if g < per_dir else yb_ref
        out_ref[:, gsl[g % per_dir]] = y[g]


def _rwkv_out_kernel(x_ref, yf_ref, yb_ref, r_ref, k_ref, v_ref, g_ref, as_ref, ka_ref, rk_ref, lnx_ref, wo_ref, o_ref):
    d = x_ref.shape[1]
    n = B_HEAD_DIM
    y = yf_ref[...] + yb_ref[...]
    kd = k_ref[...] * (2.0 + (as_ref[0] + as_ref[1] - 2.0) * ka_ref[...])
    prod = r_ref[...] * kd * rk_ref[...]
    low = lax.broadcasted_iota(I32, (1, LANES), 1) < n

    def head_sum(a):
        s_lo = jnp.sum(jnp.where(low, a, 0.0), axis=-1, keepdims=True)
        s_hi = jnp.sum(jnp.where(low, 0.0, a), axis=-1, keepdims=True)
        return jnp.where(low, s_lo, s_hi)

    parts = []
    for p in range(d // LANES):
        sl = slice(p * LANES, (p + 1) * LANES)
        ys = y[:, sl]
        cen = ys - head_sum(ys) * (1.0 / n)
        var = head_sum(cen * cen) * (1.0 / n)
        yn = cen * lax.rsqrt(var + B_GN_EPS) * lnx_ref[0:1, sl] + lnx_ref[1:2, sl]
        bonus = head_sum(prod[:, sl]) * v_ref[:, sl]
        parts.append(((yn + bonus) * g_ref[:, sl]).astype(BF16))
    o_ref[...] = x_ref[...] + jnp.dot(jnp.concatenate(parts, axis=-1), wo_ref[...], preferred_element_type=F32)


def mixer_b(x, ln, mu, w_rkv, w0, w1, w2, a0, a1, a2, g1, g2, k_k, k_a, r_k, ln_x, wo, batch, s_len):
    t, d = x.shape
    bf = lambda a: a.astype(BF16)
    tm = _row_tile(s_len, 256)
    tiles_per_seq = s_len // tm
    halo = 8
    row = lambda n: pl.BlockSpec((tm, n), lambda i: (i, 0))
    row2 = lambda n: pl.BlockSpec((2, tm, n), lambda i: (0, i, 0))
    full = lambda a: pl.BlockSpec(a.shape, lambda i: (0,) * a.ndim)
    ln2 = ln.reshape(1, d)
    weights = (ln2, mu, bf(w_rkv), bf(g1), bf(g2), w0, bf(w1), bf(w2), a0, bf(a1), bf(a2))
    tok = jax.ShapeDtypeStruct((t, d), F32)
    tok2 = jax.ShapeDtypeStruct((2, t, d), F32)
    r, k, v, g, lw, asig = pl.pallas_call(
        functools.partial(_rwkv_prep_kernel, tiles_per_seq=tiles_per_seq),
        grid=(t // tm,),
        in_specs=[row(d),
                  pl.BlockSpec((halo, d), lambda i: (jnp.maximum(i * (tm // halo) - 1, 0), 0)),
                  pl.BlockSpec((halo, d), lambda i: (jnp.minimum((i + 1) * (tm // halo), t // halo - 1), 0))]
                 + [full(a) for a in weights],
        out_specs=[row(d), row(d), row(d), row(d), row2(d), row2(d)],
        out_shape=[tok, tok, tok, tok, tok2, tok2],
        compiler_params=_params("parallel"),
        name="rwkv_prep",
    )(x, x, x, *weights)

    hb = WKV_HEADS_PER_STEP
    width = hb * B_HEAD_DIM
    c = min(WKV_CHUNK, s_len)
    nc = s_len // c
    assert width == d
    tok_f = pl.BlockSpec((None, c, d), lambda b, ci: (b, ci, 0))
    tok_b = pl.BlockSpec((None, c, d), lambda b, ci: (b, nc - 1 - ci, 0))
    dir_f = pl.BlockSpec((None, None, c, d), lambda b, ci: (0, b, ci, 0))
    dir_b = pl.BlockSpec((None, None, c, d), lambda b, ci: (1, b, nc - 1 - ci, 0))
    par = pl.BlockSpec((1, d), lambda b, ci: (0, 0))
    v3 = lambda a: a.reshape(batch, s_len, d)
    v4 = lambda a: a.reshape(2, batch, s_len, d)
    seq_out = jax.ShapeDtypeStruct((batch, s_len, d), F32)
    y_f, y_b = pl.pallas_call(
        functools.partial(_wkv_kernel, hb=hb),
        grid=(batch, nc),
        in_specs=[tok_f, tok_f, tok_f, tok_b, tok_b, tok_b, dir_f, dir_b, dir_f, dir_b, par, par],
        out_specs=[tok_f, tok_b],
        out_shape=[seq_out, seq_out],
        scratch_shapes=[pltpu.VMEM((2 * hb // WKV_PACK, B_HEAD_DIM, WKV_PACK * B_HEAD_DIM), F32)],
        compiler_params=_params("parallel", "arbitrary"),
        name="wkv_scan",
    )(v3(r), v3(k), v3(v), v3(r), v3(k), v3(v), v4(lw), v4(lw), v4(asig), v4(asig),
      k_k.reshape(1, d), k_a.reshape(1, d))

    params = (k_a.reshape(1, d), r_k.reshape(1, d), ln_x, bf(wo))
    return pl.pallas_call(
        _rwkv_out_kernel,
        grid=(t // tm,),
        in_specs=[row(d), row(d), row(d), row(d), row(d), row(d), row(d), row2(d)] + [full(a) for a in params],
        out_specs=row(d),
        out_shape=tok,
        compiler_params=_params("parallel"),
        name="rwkv_out",
    )(x, y_f.reshape(t, d), y_b.reshape(t, d), r, k, v, g, asig, *params)


TOP_K = 2
ROUTE_EXPERT, ROUTE_GATE, ROUTE_RANK = 0, TOP_K, 2 * TOP_K
MOE_ROW_TILE = 512
ROW_DMA_UNROLL = 8


def _router_kernel(x_ref, g_ref, wr_ref, hn_ref, route_ref, cnt_ref, run_ref, *, n_exp):
    @pl.when(pl.program_id(0) == 0)
    def _():
        run_ref[...] = jnp.zeros_like(run_ref)

    tm = x_ref.shape[0]
    h = _rms(x_ref[...], g_ref[...])
    hn_ref[...] = h
    logits = jnp.dot(h, wr_ref[...], preferred_element_type=F32, precision=lax.Precision.HIGHEST)
    lane = lax.broadcasted_iota(I32, (tm, LANES), 1)
    logits = jnp.where(lane < n_exp, logits, NEG_INF)

    def top1(v):
        m = jnp.max(v, axis=-1, keepdims=True)
        return m, jnp.min(jnp.where(v == m, lane, LANES), axis=-1, keepdims=True)

    m1, i1 = top1(logits)
    m2, i2 = top1(jnp.where(lane == i1, NEG_INF, logits))
    e = jnp.exp(m2 - m1)
    g1 = 1.0 / (1.0 + e)
    g2 = e * g1
    sel = ((lane == i1) | (lane == i2)).astype(F32)
    earlier = (lax.broadcasted_iota(I32, (tm, tm), 0) > lax.broadcasted_iota(I32, (tm, tm), 1)).astype(BF16)
    before = jnp.dot(earlier, sel.astype(BF16), preferred_element_type=F32) + run_ref[...]
    r1 = jnp.sum(jnp.where(lane == i1, before, 0.0), axis=-1, keepdims=True)
    r2 = jnp.sum(jnp.where(lane == i2, before, 0.0), axis=-1, keepdims=True)
    run_ref[...] += jnp.sum(sel, axis=0, keepdims=True)
    cnt_ref[...] = run_ref[...]
    route = jnp.zeros((tm, LANES), F32)
    for off, vals in ((ROUTE_EXPERT, (i1.astype(F32), i2.astype(F32))), (ROUTE_GATE, (g1, g2)), (ROUTE_RANK, (r1, r2))):
        for kk, v in enumerate(vals):
            route = jnp.where(lane == off + kk, v, route)
    route_ref[...] = route


def _dispatch_kernel(dest_ref, hn_ref, xs_in_ref, xs_ref, sem):
    del xs_in_ref
    tm = hn_ref.shape[0]

    def row_copy(r, slot):
        return pltpu.make_async_copy(hn_ref.at[pl.ds(r, 1)], xs_ref.at[pl.ds(dest_ref[TOP_K * r + slot], 1)], sem)

    def start(r, c):
        for slot in range(TOP_K):
            row_copy(r, slot).start()
        return c

    def wait(r, c):
        for slot in range(TOP_K):
            row_copy(r, slot).wait()
        return c

    lax.fori_loop(0, tm, start, 0, unroll=ROW_DMA_UNROLL)
    lax.fori_loop(0, tm, wait, 0, unroll=ROW_DMA_UNROLL)


def _expert_ffn_kernel(te_ref, x_ref, w1_ref, w3_ref, w2_ref, o_ref, xb_ref, acc_ref):
    del te_ref
    j = pl.program_id(1)

    @pl.when(j == 0)
    def _():
        xb_ref[...] = x_ref[...].astype(BF16)
        acc_ref[...] = jnp.zeros_like(acc_ref)

    xb = xb_ref[...]
    gate = jnp.dot(xb, w1_ref[...], preferred_element_type=F32)
    up = jnp.dot(xb, w3_ref[...], preferred_element_type=F32)
    act = (gate * jax.nn.sigmoid(gate) * up).astype(BF16)
    acc_ref[...] += jnp.dot(act, w2_ref[...], preferred_element_type=F32)

    @pl.when(j == pl.num_programs(1) - 1)
    def _():
        o_ref[...] = acc_ref[...]


def _moe_combine_kernel(dest_ref, route_ref, x_ref, ys_ref, o_ref, buf_ref, sem):
    tm = x_ref.shape[0]

    def row_copy(r, slot):
        return pltpu.make_async_copy(ys_ref.at[pl.ds(dest_ref[TOP_K * r + slot], 1)], buf_ref.at[slot, pl.ds(r, 1)], sem)

    def start(r, c):
        for slot in range(TOP_K):
            row_copy(r, slot).start()
        return c

    def wait(r, c):
        for slot in range(TOP_K):
            row_copy(r, slot).wait()
        return c

    lax.fori_loop(0, tm, start, 0, unroll=ROW_DMA_UNROLL)
    lax.fori_loop(0, tm, wait, 0, unroll=ROW_DMA_UNROLL)
    route = route_ref[...]
    out = x_ref[...]
    for slot in range(TOP_K):
        out = out + route[:, ROUTE_GATE + slot:ROUTE_GATE + slot + 1] * buf_ref[slot]
    o_ref[...] = out


def moe_ffn(x, ln, router, w13, w2, tf=1792):
    t, d = x.shape
    n_exp, dff = w2.shape[0], w2.shape[1]
    assert dff % tf == 0
    nj = dff // tf
    tmx = min(MOE_ROW_TILE, t)
    tm = _row_tile(t, 512)
    wr = jnp.pad(router, ((0, 0), (0, LANES - n_exp)))
    hn, route, cnt = pl.pallas_call(
        functools.partial(_router_kernel, n_exp=n_exp),
        grid=(t // tm,),
        in_specs=[pl.BlockSpec((tm, d), lambda i: (i, 0)),
                  pl.BlockSpec((1, d), lambda i: (0, 0)),
                  pl.BlockSpec((d, LANES), lambda i: (0, 0))],
        out_specs=[pl.BlockSpec((tm, d), lambda i: (i, 0)),
                   pl.BlockSpec((tm, LANES), lambda i: (i, 0)),
                   pl.BlockSpec((1, LANES), lambda i: (0, 0))],
        out_shape=[jax.ShapeDtypeStruct((t, d), F32),
                   jax.ShapeDtypeStruct((t, LANES), F32),
                   jax.ShapeDtypeStruct((1, LANES), F32)],
        scratch_shapes=[pltpu.VMEM((1, LANES), F32)],
        compiler_params=_params("arbitrary"),
        name="moe_router",
    )(x, ln.reshape(1, d), wr)

    expert = route[:, ROUTE_EXPERT:ROUTE_EXPERT + TOP_K].astype(I32)
    rank = route[:, ROUTE_RANK:ROUTE_RANK + TOP_K].astype(I32)
    counts = cnt[0, :n_exp].astype(I32)
    padded = (counts + tmx - 1) // tmx * tmx
    ends = jnp.cumsum(padded)
    dest = ((ends - padded)[expert] + rank).reshape(TOP_K * t)
    n_tiles = (TOP_K * t) // tmx + n_exp
    rows = n_tiles * tmx
    tile_start = jnp.arange(n_tiles, dtype=I32) * tmx
    tile_expert = jnp.minimum(jnp.sum((tile_start[:, None] >= ends[None, :]).astype(I32), axis=1), n_exp - 1)

    smem_rows = lambda n: pl.BlockSpec((TOP_K * n,), lambda i: (i,), memory_space=pltpu.SMEM)
    xs = pl.pallas_call(
        _dispatch_kernel,
        grid=(t // tm,),
        in_specs=[smem_rows(tm),
                  pl.BlockSpec((tm, d), lambda i: (i, 0)),
                  pl.BlockSpec(memory_space=pl.ANY)],
        out_specs=pl.BlockSpec(memory_space=pl.ANY),
        out_shape=jax.ShapeDtypeStruct((rows, d), F32),
        scratch_shapes=[pltpu.SemaphoreType.DMA(())],
        input_output_aliases={2: 0},
        compiler_params=_params("arbitrary"),
        name="moe_dispatch",
    )(dest, hn, jnp.zeros((rows, d), F32))

    ys = pl.pallas_call(
        _expert_ffn_kernel,
        grid_spec=pltpu.PrefetchScalarGridSpec(
            num_scalar_prefetch=1,
            grid=(n_tiles, nj),
            in_specs=[pl.BlockSpec((tmx, d), lambda i, j, te: (i, 0)),
                      pl.BlockSpec((None, d, tf), lambda i, j, te: (te[i], 0, j)),
                      pl.BlockSpec((None, d, tf), lambda i, j, te: (te[i], 0, nj + j)),
                      pl.BlockSpec((None, tf, d), lambda i, j, te: (te[i], j, 0))],
            out_specs=pl.BlockSpec((tmx, d), lambda i, j, te: (i, 0)),
            scratch_shapes=[pltpu.VMEM((tmx, d), BF16), pltpu.VMEM((tmx, d), F32)]),
        out_shape=jax.ShapeDtypeStruct((rows, d), F32),
        compiler_params=_params("parallel", "arbitrary"),
        name="moe_expert_ffn",
    )(tile_expert, xs, w13, w13, w2)

    return pl.pallas_call(
        _moe_combine_kernel,
        grid=(t // tm,),
        in_specs=[smem_rows(tm),
                  pl.BlockSpec((tm, LANES), lambda i: (i, 0)),
                  pl.BlockSpec((tm, d), lambda i: (i, 0)),
                  pl.BlockSpec(memory_space=pl.ANY)],
        out_specs=pl.BlockSpec((tm, d), lambda i: (i, 0)),
        out_shape=jax.ShapeDtypeStruct((t, d), F32),
        scratch_shapes=[pltpu.VMEM((TOP_K, tm, d), F32), pltpu.SemaphoreType.DMA(())],
        compiler_params=_params("arbitrary"),
        name="moe_combine",
    )(dest, route, x, ys)


def kernel(x, positions, ln_mix, ln_ffn, a_wqkv, a_qk_norm, a_wo, b_mu, b_wrkv, b_w0, b_w1, b_w2, b_a0, b_a1, b_a2, b_g1, b_g2, b_kk, b_ka, b_rk, b_lnx, b_wo, c_wa, c_qa_norm, c_kva_norm, c_wq_b, c_wkv_b, c_qk_norm, c_wo, f_w13, f_w2, m_router, m_w13, m_w2):
    batch, s_len, d = x.shape
    t = batch * s_len
    cs, sn = _rope_tables(positions)
    cs, sn = cs.reshape(t, LANES), sn.reshape(t, LANES)
    h = x.reshape(t, d)
    n_mixers = 3
    for i in range(ln_mix.shape[0]):
        j, kind = divmod(i, n_mixers)
        if kind == 0:
            h = mixer_a(h, ln_mix[i], a_wqkv[j], a_qk_norm[j], a_wo[j], cs, sn, batch, s_len)
        elif kind == 1:
            h = mixer_b(h, ln_mix[i], b_mu[j], b_wrkv[j], b_w0[j], b_w1[j], b_w2[j], b_a0[j], b_a1[j], b_a2[j],
                        b_g1[j], b_g2[j], b_kk[j], b_ka[j], b_rk[j], b_lnx[j], b_wo[j], batch, s_len)
        else:
            h = mixer_c(h, ln_mix[i], c_wa[j], c_qa_norm[j], c_kva_norm[j], c_wq_b[j], c_wkv_b[j], c_qk_norm[j],
                        c_wo[j], cs, sn, batch, s_len)
        f = i // 2
        if i % 2 == 0:
            h = dense_ffn(h, ln_ffn[i], f_w13[f].astype(BF16), f_w2[f].astype(BF16))
        else:
            h = moe_ffn(h, ln_ffn[i], m_router[f], m_w13[f].astype(BF16), m_w2[f].astype(BF16))
    return h.reshape(batch, s_len, d)
```

```python
import functools
import math

import jax
import jax.numpy as jnp
from jax import lax
from jax.experimental import pallas as pl
from jax.experimental.pallas import tpu as pltpu

F32 = jnp.float32
BF16 = jnp.bfloat16
I32 = jnp.int32

NORM_EPS = 1e-6
ROPE_THETA = 500000.0
NEG_INF = -1e30
LANES = 128
ROT_DIM = 32
ROT_HALF = ROT_DIM // 2
DILATED_PAIRS = ((128, 1), (512, 4), (2048, 16))
A_HEAD_DIM = 128
VMEM_LIMIT = 48 * 1024 * 1024


def _params(*sem):
    return pltpu.CompilerParams(dimension_semantics=sem, vmem_limit_bytes=VMEM_LIMIT)


def _rms(x, g):
    return x * lax.rsqrt(jnp.mean(x * x, axis=-1, keepdims=True) + NORM_EPS) * g


def _row_tile(t, want):
    tm = min(t, want)
    assert t % tm == 0
    return tm


def _norm_matmul_kernel(x_ref, g_ref, w_ref, o_ref, xn_ref):
    @pl.when(pl.program_id(1) == 0)
    def _():
        xn_ref[...] = _rms(x_ref[...], g_ref[...]).astype(BF16)

    o_ref[...] = jnp.dot(xn_ref[...], w_ref[...], preferred_element_type=F32).astype(o_ref.dtype)


def norm_matmul(x, g, w, out_dtype, tn):
    t, k = x.shape
    n = w.shape[1]
    tm = _row_tile(t, 1024)
    assert n % tn == 0
    return pl.pallas_call(
        _norm_matmul_kernel,
        grid=(t // tm, n // tn),
        in_specs=[pl.BlockSpec((tm, k), lambda i, j: (i, 0)),
                  pl.BlockSpec((1, k), lambda i, j: (0, 0)),
                  pl.BlockSpec((k, tn), lambda i, j: (0, j))],
        out_specs=pl.BlockSpec((tm, tn), lambda i, j: (i, j)),
        out_shape=jax.ShapeDtypeStruct((t, n), out_dtype),
        scratch_shapes=[pltpu.VMEM((tm, k), BF16)],
        compiler_params=_params("parallel", "arbitrary"),
        name="norm_matmul",
    )(x, g.reshape(1, k), w)


def _matmul_res_kernel(a_ref, w_ref, x_ref, o_ref):
    o_ref[...] = x_ref[...] + jnp.dot(a_ref[...], w_ref[...], preferred_element_type=F32)


def matmul_residual(a, w, x):
    t, k = a.shape
    n = w.shape[1]
    tm = _row_tile(t, 1024)
    return pl.pallas_call(
        _matmul_res_kernel,
        grid=(t // tm,),
        in_specs=[pl.BlockSpec((tm, k), lambda i: (i, 0)),
                  pl.BlockSpec((k, n), lambda i: (0, 0)),
                  pl.BlockSpec((tm, n), lambda i: (i, 0))],
        out_specs=pl.BlockSpec((tm, n), lambda i: (i, 0)),
        out_shape=jax.ShapeDtypeStruct((t, n), F32),
        compiler_params=_params("parallel"),
        name="matmul_residual",
    )(a, w, x)


def _rope_tables(positions):
    b, s = positions.shape
    inv_freq = ROPE_THETA ** (-jnp.arange(0, ROT_DIM, 2, dtype=F32) / ROT_DIM)
    ang = positions.astype(F32)[..., None] * inv_freq
    c, sn = jnp.cos(ang), jnp.sin(ang)
    rest = (b, s, LANES - ROT_DIM)
    cs_t = jnp.concatenate([c, c, jnp.ones(rest, F32)], axis=-1)
    sn_t = jnp.concatenate([-sn, sn, jnp.zeros(rest, F32)], axis=-1)
    return cs_t, sn_t


def _rope(x, cs, sn, lane):
    n = x.shape[-1]
    partner = jnp.where(lane < ROT_HALF, pltpu.roll(x, n - ROT_HALF, 1), pltpu.roll(x, ROT_HALF, 1))
    return x * cs + partner * sn


def _dil_attn_kernel(q_ref, k_ref, v_ref, o_ref, lse_ref, *, seq, qt, kw, half, heads):
    hd = A_HEAD_DIM
    qi = pl.program_id(2)
    q0 = pl.multiple_of(qi * qt, qt)
    start = jnp.clip(q0 - half, 0, seq - kw)
    start = pl.multiple_of(start, math.gcd(half, qt))
    qpos = q0 + lax.broadcasted_iota(I32, (qt, kw), 0)
    kpos = start + lax.broadcasted_iota(I32, (qt, kw), 1)
    valid = jnp.abs(qpos - kpos) <= half
    head_of_lane = lax.broadcasted_iota(I32, (1, LANES), 1) // (LANES // heads)
    sls = [slice(h * hd, (h + 1) * hd) for h in range(heads)]
    ones = jnp.ones((kw, hd), BF16)
    s = [lax.dot_general(q_ref[0, :, sl], k_ref[0, pl.ds(start, kw), sl], (((1,), (1,)), ((), ())),
                         preferred_element_type=F32) for sl in sls]
    s = [jnp.where(valid, t, NEG_INF) for t in s]
    m = [jnp.max(t, axis=-1, keepdims=True) for t in s]
    p = [jnp.exp(t - mm).astype(BF16) for t, mm in zip(s, m)]
    ol = [jnp.dot(p[h], jnp.concatenate([v_ref[0, pl.ds(start, kw), sls[h]], ones], axis=1),
                  preferred_element_type=F32) for h in range(heads)]
    lse_row = jnp.zeros((qt, LANES), F32)
    for h in range(heads):
        o_ref[0, :, sls[h]] = (ol[h][:, :hd] / ol[h][:, hd:]).astype(o_ref.dtype)
        lse_row = jnp.where(head_of_lane == h, m[h] + jnp.log(ol[h][:, hd:hd + 1]), lse_row)
    lse_ref[0] = lse_row


A_PAIR = 2 * A_HEAD_DIM


def _head_helper_matrices():
    i = jnp.arange(A_PAIR)
    same_head = (i[:, None] // A_HEAD_DIM) == (i[None, :] // A_HEAD_DIM)
    dst = i % A_HEAD_DIM
    src = jnp.where(dst < ROT_HALF, i + ROT_HALF, i - ROT_HALF)
    perm = (i[:, None] == src[None, :]) & (dst[None, :] < ROT_DIM)
    return same_head.astype(BF16), perm.astype(BF16)


def _qkv_group_kernel(x_ref, g_ref, w_ref, cs_ref, sn_ref, gain_ref, ones_ref, perm_ref, o_ref, xn_ref, acc_ref, *, dil):
    j = pl.program_id(1)

    @pl.when(j == 0)
    def _():
        xn_ref[...] = _rms(x_ref[...], g_ref[...]).astype(BF16)

    tm = x_ref.shape[0]
    pairs = w_ref.shape[1] // A_PAIR
    psl = [slice(p * A_PAIR, (p + 1) * A_PAIR) for p in range(pairs)]
    project = lambda p: jnp.dot(xn_ref[...], w_ref[:, psl[p]], preferred_element_type=F32)

    def emit(p, vals):
        if dil == 1:
            o_ref[0, 0, :, psl[p]] = vals.astype(o_ref.dtype)
            return
        chunks = A_PAIR // LANES
        for c in range(chunks):
            acc_ref[p * chunks + c] = vals[:, c * LANES:(c + 1) * LANES]
        for r in range(dil):
            o_ref[0, r, :, psl[p]] = jnp.concatenate(
                [acc_ref[p * chunks + c, pl.ds(r, tm // dil, stride=dil), :] for c in range(chunks)],
                axis=-1).astype(o_ref.dtype)

    @pl.when(j < 2)
    def _():
        cs = jnp.concatenate([cs_ref[...]] * 2, axis=1)
        sn = jnp.concatenate([sn_ref[...]] * 2, axis=1)
        gain = gain_ref[pl.ds(j, 1), :]
        post = jnp.where(j == 0, A_HEAD_DIM ** -0.5, 1.0)
        for p in range(pairs):
            a = project(p)
            sumsq = jnp.dot((a * a).astype(BF16), ones_ref[...], preferred_element_type=F32)
            ag = a * gain[:, psl[p]]
            pair = jnp.dot(ag.astype(BF16), perm_ref[...], preferred_element_type=F32)
            emit(p, (ag * cs + pair * sn) * (lax.rsqrt(sumsq * (1.0 / A_HEAD_DIM) + NORM_EPS) * post))

    @pl.when(j == 2)
    def _():
        for p in range(pairs):
            emit(p, project(p))


def qkv_group_projection(x, ln, wqkv, cs, sn, gq, gk, group, batch, s_len, width):
    t, d = x.shape
    dil = DILATED_PAIRS[group][1]
    n_groups = len(DILATED_PAIRS)
    heads = width // A_HEAD_DIM
    tm = _row_tile(s_len, 1024)
    tiles = s_len // tm
    assert tm % (16 * dil) == 0
    gains = jnp.stack([jnp.tile(gq, heads), jnp.tile(gk, heads)])
    ones, perm = _head_helper_matrices()
    const = lambda a: pl.BlockSpec(a.shape, lambda i, j: (0, 0))
    return pl.pallas_call(
        functools.partial(_qkv_group_kernel, dil=dil),
        grid=(t // tm, 3),
        in_specs=[pl.BlockSpec((tm, d), lambda i, j: (i, 0)),
                  pl.BlockSpec((1, d), lambda i, j: (0, 0)),
                  pl.BlockSpec((d, width), lambda i, j: (0, j * n_groups + group)),
                  pl.BlockSpec((tm, LANES), lambda i, j: (i, 0)),
                  pl.BlockSpec((tm, LANES), lambda i, j: (i, 0)),
                  const(gains), const(ones), const(perm)],
        out_specs=pl.BlockSpec((1, dil, tm // dil, width), lambda i, j: (i // tiles, 0, i % tiles, j)),
        out_shape=jax.ShapeDtypeStruct((batch, dil, s_len // dil, 3 * width), BF16),
        scratch_shapes=[pltpu.VMEM((tm, d), BF16), pltpu.VMEM((width // LANES, tm, LANES), F32)],
        compiler_params=_params("parallel", "arbitrary"),
        name=f"qkv_projection_g{group}",
    )(x, ln.reshape(1, d), wqkv, cs, sn, gains, ones, perm)


def dilated_group_attention(qkv, group, batch, s_len, heads):
    window, dil = DILATED_PAIRS[group]
    half = window // (2 * dil)
    seq = s_len // dil
    width = heads * A_HEAD_DIM
    qt = min(256, seq)
    kw = min(qt + 2 * half, seq)
    assert seq % qt == 0 and (seq - kw) % math.gcd(half, qt) == 0
    whole = lambda c: pl.BlockSpec((None, 1, seq, width), lambda b, r, qi: (b, r, 0, c))
    tile = lambda n: pl.BlockSpec((None, 1, qt, n), lambda b, r, qi: (b, r, qi, 0))
    kern = functools.partial(_dil_attn_kernel, seq=seq, qt=qt, kw=kw, half=half, heads=heads)
    return pl.pallas_call(
        kern,
        grid=(batch, dil, seq // qt),
        in_specs=[tile(width), whole(1), whole(2)],
        out_specs=[tile(width), tile(LANES)],
        out_shape=[jax.ShapeDtypeStruct((batch, dil, seq, width), BF16),
                   jax.ShapeDtypeStruct((batch, dil, seq, LANES), F32)],
        compiler_params=_params("parallel", "parallel", "parallel"),
        name=f"dilated_attention_g{group}",
    )(qkv, qkv, qkv)


def _combine_wo_kernel(x_ref, o0_ref, o1_ref, o2_ref, l0_ref, l1_ref, l2_ref, w_ref, out_ref, ot_ref, lt_ref, *, heads):
    hd = A_HEAD_DIM
    rep = LANES // heads
    tm = x_ref.shape[0]

    def token_order(src_ref, dst_ref, g):
        dil = src_ref.shape[0]
        if dil == 1:
            return src_ref[0].astype(F32)
        chunks = src_ref.shape[-1] // LANES
        for r in range(dil):
            blk = src_ref[r].astype(F32)
            for c in range(chunks):
                dst_ref[g, c, pl.ds(r, tm // dil, stride=dil), :] = blk[:, c * LANES:(c + 1) * LANES]
        return jnp.concatenate([dst_ref[g, c] for c in range(chunks)], axis=-1)

    ls = [token_order(ref, lt_ref, g) for g, ref in enumerate((l0_ref, l1_ref, l2_ref))]
    os_ = [token_order(ref, ot_ref, g) for g, ref in enumerate((o0_ref, o1_ref, o2_ref))]
    m = jnp.maximum(jnp.maximum(ls[0], ls[1]), ls[2])
    es = [jnp.exp(l - m) for l in ls]
    inv = 1.0 / (es[0] + es[1] + es[2])
    al = [e * inv for e in es]
    parts = []
    for h in range(heads):
        sl = slice(h * hd, (h + 1) * hd)
        c = slice(h * rep, h * rep + 1)
        parts.append((al[0][:, c] * os_[0][:, sl] + al[1][:, c] * os_[1][:, sl] + al[2][:, c] * os_[2][:, sl]).astype(BF16))
    o = jnp.concatenate(parts, axis=-1)
    out_ref[...] = x_ref[...] + jnp.dot(o, w_ref[...], preferred_element_type=F32)


def combine_wo(x, outs, lses, wo, heads, batch, s_len):
    t, d = x.shape
    width = wo.shape[0]
    n_groups = len(DILATED_PAIRS)
    tm = _row_tile(s_len, 512)
    tiles = s_len // tm
    assert all(tm % (16 * dil) == 0 for _, dil in DILATED_PAIRS)
    row = lambda n: pl.BlockSpec((tm, n), lambda i: (i, 0))
    res = lambda g, n: pl.BlockSpec((None, DILATED_PAIRS[g][1], tm // DILATED_PAIRS[g][1], n),
                                    lambda i: (i // tiles, 0, i % tiles, 0))
    return pl.pallas_call(
        functools.partial(_combine_wo_kernel, heads=heads),
        grid=(t // tm,),
        in_specs=[row(d)] + [res(g, width) for g in range(n_groups)] + [res(g, LANES) for g in range(n_groups)]
                 + [pl.BlockSpec((width, d), lambda i: (0, 0))],
        out_specs=row(d),
        out_shape=jax.ShapeDtypeStruct((t, d), F32),
        scratch_shapes=[pltpu.VMEM((n_groups, width // LANES, tm, LANES), F32),
                        pltpu.VMEM((n_groups, 1, tm, LANES), F32)],
        compiler_params=_params("parallel"),
        name="combine_wo",
    )(x, *outs, *lses, wo)


def mixer_a(x, ln, wqkv, qk_norm, wo, cs, sn, batch, s_len):
    heads = wo.shape[0] // A_HEAD_DIM
    wqkv = wqkv.astype(BF16)
    outs, lses = [], []
    for g in range(len(DILATED_PAIRS)):
        qkv = qkv_group_projection(x, ln, wqkv, cs, sn, qk_norm[0, g], qk_norm[1, g], g, batch, s_len, wo.shape[0])
        o, lse = dilated_group_attention(qkv, g, batch, s_len, heads)
        outs.append(o)
        lses.append(lse)
    return combine_wo(x, outs, lses, wo.astype(BF16), heads, batch, s_len)


def _ffn_kernel(x_ref, g_ref, w1_ref, w3_ref, w2_ref, o_ref, xn_ref, acc_ref):
    j = pl.program_id(1)

    @pl.when(j == 0)
    def _():
        xn_ref[...] = _rms(x_ref[...], g_ref[...]).astype(BF16)
        acc_ref[...] = jnp.zeros_like(acc_ref)

    acc_ref[...] += _swiglu_chunks(xn_ref[...], w1_ref, w3_ref, w2_ref)

    @pl.when(j == pl.num_programs(1) - 1)
    def _():
        o_ref[...] = x_ref[...] + acc_ref[...]


FFN_SUB = 256


def _swiglu_chunks(xn, w1_ref, w3_ref, w2_ref):
    tf = w2_ref.shape[0]
    assert tf % FFN_SUB == 0
    total = None
    for s in range(tf // FFN_SUB):
        sl = slice(s * FFN_SUB, (s + 1) * FFN_SUB)
        gate = jnp.dot(xn, w1_ref[:, sl], preferred_element_type=F32)
        up = jnp.dot(xn, w3_ref[:, sl], preferred_element_type=F32)
        act = (gate * jax.nn.sigmoid(gate) * up).astype(BF16)
        part = jnp.dot(act, w2_ref[sl, :], preferred_element_type=F32)
        total = part if total is None else total + part
    return total


def dense_ffn(x, ln, w13, w2):
    t, d = x.shape
    tf = dff = w2.shape[0]
    nj = dff // tf
    tm = _row_tile(t, 512)
    resident = dict(pipeline_mode=pl.Buffered(1))
    return pl.pallas_call(
        _ffn_kernel,
        grid=(t // tm, nj),
        in_specs=[pl.BlockSpec((tm, d), lambda i, j: (i, 0)),
                  pl.BlockSpec((1, d), lambda i, j: (0, 0)),
                  pl.BlockSpec((d, tf), lambda i, j: (0, j), **resident),
                  pl.BlockSpec((d, tf), lambda i, j: (0, nj + j), **resident),
                  pl.BlockSpec((tf, d), lambda i, j: (j, 0), **resident)],
        out_specs=pl.BlockSpec((tm, d), lambda i, j: (i, 0)),
        out_shape=jax.ShapeDtypeStruct((t, d), F32),
        scratch_shapes=[pltpu.VMEM((tm, d), BF16), pltpu.VMEM((tm, d), F32)],
        compiler_params=_params("parallel", "arbitrary"),
        name="dense_ffn",
    )(x, ln.reshape(1, d), w13, w13, w2)


C_ROPE = ROT_DIM
C_NOPE = 64
C_V = 64
C_QK = C_ROPE + C_NOPE
LOG2_E = math.log2(math.e)


def _mla_qkv_kernel(lat_ref, qa_ref, kva_ref, gq_ref, gk_ref, wq_ref, wkv_ref, cs_ref, sn_ref,
                    ones_ref, perm_ref, q_ref, k_ref, vt_ref, *, heads, q_lora, kv_lora):
    two = lambda a: jnp.concatenate([a, a], axis=1)
    cs, sn = two(cs_ref[...]), two(sn_ref[...])
    lat = lat_ref[...]
    qn = _rms(lat[:, :q_lora], qa_ref[...]).astype(BF16)
    kvn = _rms(lat[:, q_lora:q_lora + kv_lora], kva_ref[...]).astype(BF16)
    k_rope = two(lat[:, q_lora + kv_lora:])

    def head_norm(a, g, post):
        sumsq = jnp.dot((a * a).astype(BF16), ones_ref[...], preferred_element_type=F32)
        ag = a * g
        pair = jnp.dot(ag.astype(BF16), perm_ref[...], preferred_element_type=F32)
        return (ag * cs + pair * sn) * (lax.rsqrt(sumsq * (1.0 / C_QK) + NORM_EPS) * post)

    gq, gk = two(gq_ref[...]), two(gk_ref[...])
    kv = jnp.dot(kvn, wkv_ref[...], preferred_element_type=F32)
    for p in range(heads * LANES // A_PAIR):
        sl = slice(p * A_PAIR, (p + 1) * A_PAIR)
        q = jnp.dot(qn, wq_ref[:, sl], preferred_element_type=F32)
        q_ref[:, sl] = head_norm(q, gq, C_QK ** -0.5 * LOG2_E).astype(BF16)
        k_ref[:, sl] = head_norm(kv[:, sl] + k_rope, gk, 1.0).astype(BF16)
    vt_ref[...] = kv[:, heads * LANES:].T.astype(BF16)


def _mla_attn_kernel(q_ref, k_ref, vt_ref, o_ref, *, pack):
    outs = []
    for i in range(pack):
        q = q_ref[0, :, i * LANES:(i + 1) * LANES]
        k = k_ref[0, :, i * LANES:(i + 1) * LANES]
        st = lax.dot_general(k, q, (((1,), (1,)), ((), ())), preferred_element_type=F32)
        m = jnp.max(st, axis=0, keepdims=True)
        p = jnp.exp2(st - m)
        l = jnp.sum(p, axis=0, keepdims=True)
        vt = vt_ref[0, i * C_V:(i + 1) * C_V, :]
        outs.append(jnp.dot(vt, p.astype(BF16), preferred_element_type=F32) / l)
    o_ref[0] = jnp.concatenate(outs, axis=0).astype(o_ref.dtype)


def _matmul_res_t_kernel(at_ref, w_ref, x_ref, o_ref):
    o_ref[...] = x_ref[...] + lax.dot_general(at_ref[...], w_ref[...], (((0,), (0,)), ((), ())),
                                              preferred_element_type=F32)


def matmul_residual_t(a_t, w, x, batch, s_len):
    k, n = w.shape
    tm = _row_tile(s_len, 512)
    tiles = s_len // tm
    return pl.pallas_call(
        _matmul_res_t_kernel,
        grid=(batch * tiles,),
        in_specs=[pl.BlockSpec((None, k, tm), lambda i: (i // tiles, 0, i % tiles)),
                  pl.BlockSpec((k, n), lambda i: (0, 0)),
                  pl.BlockSpec((tm, n), lambda i: (i, 0))],
        out_specs=pl.BlockSpec((tm, n), lambda i: (i, 0)),
        out_shape=jax.ShapeDtypeStruct(x.shape, F32),
        compiler_params=_params("parallel"),
        name="matmul_residual_t",
    )(a_t, w, x)


def _pad_cols(w, heads, real, slot):
    k = w.shape[0]
    w = w.reshape(k, heads, real)
    return jnp.pad(w, ((0, 0), (0, 0), (0, slot - real))).reshape(k, heads * slot)


def mixer_c(x, ln, wa, qa_norm, kva_norm, wq_b, wkv_b, qk_norm, wo, cs, sn, batch, s_len):
    t, d = x.shape
    q_lora, kv_lora = qa_norm.shape[0], kva_norm.shape[0]
    heads = wo.shape[0] // C_V
    lat_w = q_lora + kv_lora + LANES
    wa_p = jnp.pad(wa, ((0, 0), (0, lat_w - wa.shape[1]))).astype(BF16)
    lat = norm_matmul(x, ln, wa_p, F32, tn=lat_w)
    wq_p = _pad_cols(wq_b, heads, C_QK, LANES).astype(BF16)
    wkv = wkv_b.reshape(kv_lora, heads, C_NOPE + C_V)
    wk_p = jnp.pad(wkv[:, :, :C_NOPE], ((0, 0), (0, 0), (C_ROPE, LANES - C_QK))).reshape(kv_lora, heads * LANES)
    wv = wkv[:, :, C_NOPE:].reshape(kv_lora, heads * C_V)
    wkv_p = jnp.concatenate([wk_p, wv], axis=1).astype(BF16)
    gq = jnp.pad(qk_norm[0], (0, LANES - C_QK)).reshape(1, LANES)
    gk = jnp.pad(qk_norm[1], (0, LANES - C_QK)).reshape(1, LANES)
    tm = _row_tile(s_len, 512)
    tiles = s_len // tm
    row = lambda n: pl.BlockSpec((tm, n), lambda i: (i, 0))
    full = lambda a: pl.BlockSpec(a.shape, lambda i: (0, 0))
    qa2, kva2 = qa_norm.reshape(1, -1), kva_norm.reshape(1, -1)
    ones, perm = _head_helper_matrices()
    q, k, vt = pl.pallas_call(
        functools.partial(_mla_qkv_kernel, heads=heads, q_lora=q_lora, kv_lora=kv_lora),
        grid=(t // tm,),
        in_specs=[row(lat_w), full(qa2), full(kva2), full(gq), full(gk), full(wq_p), full(wkv_p),
                  row(LANES), row(LANES), full(ones), full(perm)],
        out_specs=[row(heads * LANES), row(heads * LANES),
                   pl.BlockSpec((None, heads * C_V, tm), lambda i: (i // tiles, 0, i % tiles))],
        out_shape=[jax.ShapeDtypeStruct((t, heads * LANES), BF16),
                   jax.ShapeDtypeStruct((t, heads * LANES), BF16),
                   jax.ShapeDtypeStruct((batch, heads * C_V, s_len), BF16)],
        compiler_params=_params("parallel"),
        name="mla_qkv",
    )(lat, qa2, kva2, gq, gk, wq_p, wkv_p, cs, sn, ones, perm)
    pack = LANES // C_V
    tq = min(512, s_len)
    o_t = pl.pallas_call(
        functools.partial(_mla_attn_kernel, pack=pack),
        grid=(batch, heads // pack, s_len // tq),
        in_specs=[pl.BlockSpec((1, tq, pack * LANES), lambda b, h, i: (b, i, h)),
                  pl.BlockSpec((1, s_len, pack * LANES), lambda b, h, i: (b, 0, h)),
                  pl.BlockSpec((1, pack * C_V, s_len), lambda b, h, i: (b, h, 0))],
        out_specs=pl.BlockSpec((1, pack * C_V, tq), lambda b, h, i: (b, h, i)),
        out_shape=jax.ShapeDtypeStruct((batch, heads * C_V, s_len), BF16),
        compiler_params=_params("parallel", "parallel", "arbitrary"),
        name="mla_attention",
    )(q.reshape(batch, s_len, -1), k.reshape(batch, s_len, -1), vt)
    return matmul_residual_t(o_t, wo.astype(BF16), x, batch, s_len)


B_HEAD_DIM = 64
B_DECAY_SCALE = math.exp(-0.5)
B_GN_EPS = 64e-5
WKV_CHUNK = 64
WKV_HEADS_PER_STEP = 16
WKV_PACK = 4


def _bdot(a, b):
    return jnp.dot(a.astype(BF16), b.astype(BF16), preferred_element_type=F32)


def _rwkv_prep_kernel(x_ref, xp_ref, xn_ref, ln_ref, mu_ref, wrkv_ref, g1_ref, g2_ref, w0_ref, w1_ref, w2_ref,
                      a0_ref, a1_ref, a2_ref, r_ref, k_ref, v_ref, g_ref, lw_ref, as_ref, *, tiles_per_seq):
    i = pl.program_id(0)
    tm = x_ref.shape[0]
    ln = ln_ref[...]
    h = _rms(x_ref[...], ln)
    pos = i % tiles_per_seq
    prev_row = jnp.where(pos == 0, 0.0, _rms(xp_ref[7:8, :], ln))
    next_row = jnp.where(pos == tiles_per_seq - 1, 0.0, _rms(xn_ref[0:1, :], ln))
    row = lax.broadcasted_iota(I32, (tm, 1), 0)
    d_f = jnp.where(row == 0, prev_row, pltpu.roll(h, 1, 0)) - h
    d_b = jnp.where(row == tm - 1, next_row, pltpu.roll(h, tm - 1, 0)) - h
    d_c = 0.5 * (d_f + d_b)
    mu = mu_ref[...]
    mix = lambda dd, n: h + dd * mu[n:n + 1]
    r_ref[...] = _bdot(mix(d_c, 0), wrkv_ref[0])
    k_ref[...] = _bdot(mix(d_c, 1), wrkv_ref[1])
    v_ref[...] = _bdot(mix(d_c, 2), wrkv_ref[2])
    g_ref[...] = _bdot(jax.nn.sigmoid(_bdot(mix(d_c, 3), g1_ref[...])), g2_ref[...])
    for dr, dd in enumerate((d_f, d_b)):
        z = w0_ref[dr:dr + 1, :] + _bdot(jnp.tanh(_bdot(mix(dd, 4 + dr), w1_ref[dr])), w2_ref[dr])
        lw_ref[dr] = -B_DECAY_SCALE * jax.nn.sigmoid(z)
        aa = a0_ref[dr:dr + 1, :] + _bdot(_bdot(mix(dd, 6 + dr), a1_ref[dr]), a2_ref[dr])
        as_ref[dr] = jax.nn.sigmoid(aa)


def _wkv_kernel(rf_ref, kf_ref, vf_ref, rb_ref, kb_ref, vb_ref, lwf_ref, lwb_ref, asf_ref, asb_ref, kk_ref, ka_ref,
                yf_ref, yb_ref, st_ref, *, hb):
    n = B_HEAD_DIM
    c = rf_ref.shape[0]

    @pl.when(pl.program_id(1) == 0)
    def _():
        st_ref[...] = jnp.zeros_like(st_ref)

    nt = (((1,), (1,)), ((), ()))
    tn = (((0,), (0,)), ((), ()))
    mm = (((1,), (0,)), ((), ()))
    low = lax.broadcasted_iota(I32, (1, LANES), 1) < n
    pk = WKV_PACK
    gw = pk * n
    per_dir = hb // pk
    gsl = [slice(g * gw, (g + 1) * gw) for g in range(per_dir)]
    cut = lambda a: [a[:, sl] for sl in gsl]
    row_p = lax.broadcasted_iota(I32, (c, gw), 0)
    col_p = lax.broadcasted_iota(I32, (c, gw), 1) % n

    def operands(sgn, r_ref, k_ref, v_ref, lw_ref, as_ref):
        ahead = (lax.broadcasted_iota(I32, (c, c), 0) - lax.broadcasted_iota(I32, (c, c), 1)) * sgn
        lw = lw_ref[...]
        cl = jnp.dot((ahead >= 0).astype(F32), lw, preferred_element_type=F32, precision=lax.Precision.HIGHEST)
        total = jnp.sum(lw, axis=0, keepdims=True)
        e_in, e_ex, e_ng, e_end = jnp.exp(cl), jnp.exp(cl - lw), jnp.exp(-cl), jnp.exp(total - cl)
        rr, kk_raw, vv, asig = r_ref[...], k_ref[...], v_ref[...], as_ref[...]
        kk_all = kk_raw * kk_ref[...]
        kdir = kk_raw * (1.0 + (asig - 1.0) * ka_ref[...])
        sq = kk_all * kk_all
        norms = []
        for p in range(sq.shape[1] // LANES):
            slab = sq[:, p * LANES:(p + 1) * LANES]
            s_lo = jnp.sum(jnp.where(low, slab, 0.0), axis=-1, keepdims=True)
            s_hi = jnp.sum(jnp.where(low, 0.0, slab), axis=-1, keepdims=True)
            norms.append(jnp.where(low, s_lo, s_hi))
        kkn_all = kk_all * lax.rsqrt(jnp.maximum(jnp.concatenate(norms, axis=-1), 1e-24))
        beta_all = kkn_all * asig
        ahead_p = (row_p - col_p) * sgn
        return dict(
            a_t=cut((-kkn_all * e_ex).astype(BF16)), b_t=cut((beta_all * e_ng).astype(BF16)),
            rt=cut((rr * e_in).astype(BF16)), kt=cut((kdir * e_ng).astype(BF16)), v=cut(vv.astype(BF16)),
            b_h=cut((beta_all * e_end).astype(BF16)), k_hb=cut((kdir * e_end).astype(BF16)),
            p_end=cut(jnp.exp(total)), strict=[ahead_p > 0] * per_dir, incl=[ahead_p >= 0] * per_dir)

    fwd = operands(1, rf_ref, kf_ref, vf_ref, lwf_ref, asf_ref)
    bwd = operands(-1, rb_ref, kb_ref, vb_ref, lwb_ref, asb_ref)
    both = {key: fwd[key] + bwd[key] for key in fwd}
    a_t, b_t, rt, kt, v, b_h, k_hb = (both[key] for key in ("a_t", "b_t", "rt", "kt", "v", "b_h", "k_hb"))
    p_end, strict_p, incl_p = both["p_end"], both["strict"], both["incl"]
    groups = range(2 * per_dir)

    diag_block = ((lax.broadcasted_iota(I32, (gw, gw), 0) // n)
                  == (lax.broadcasted_iota(I32, (gw, gw), 1) // n))
    diag_bf = diag_block.astype(BF16)

    def bd(p):
        return jnp.concatenate([p] * pk, axis=0) * diag_bf

    def fold(full):
        kept = jnp.where(diag_block, full, 0.0)
        out = kept[:n]
        for h in range(1, pk):
            out = out + kept[h * n:(h + 1) * n]
        return out

    eye_p = (row_p == col_p).astype(F32)
    same_block = lambda m: (row_p // m) == (col_p // m)
    dots = lambda a, b, dims: [lax.dot_general(p, q, dims, preferred_element_type=F32) for p, q in zip(a, b)]
    rows = lambda *parts: [jnp.concatenate(list(ps), axis=0) for ps in zip(*parts)]
    bf = lambda xs: [t.astype(BF16) for t in xs]
    bds = lambda xs: [bd(t) for t in xs]
    gram = dots(rows(a_t, rt), rows(bds(b_t), bds(kt)), nt)
    l_ab = [jnp.where(s, g[:c, :gw], 0.0) for s, g in zip(strict_p, gram)]
    l_ak = [jnp.where(s, g[:c, gw:], 0.0).astype(BF16) for s, g in zip(strict_p, gram)]
    m_r = [jnp.concatenate([jnp.where(i, g[c:, :gw], 0.0), jnp.where(i, g[c:, gw:], 0.0)],
                           axis=1).astype(BF16) for i, g in zip(incl_p, gram)]
    in_pair = same_block(2)
    x = [eye_p + jnp.where(in_pair, l, 0.0) for l in l_ab]
    m = 2
    while m < c:
        level = same_block(2 * m) & jnp.logical_not(same_block(m))
        xb = bf(x)
        l_m = [jnp.where(level, l, 0.0).astype(BF16) for l in l_ab]
        xl = bf(dots(xb, bds(l_m), mm))
        x = [xx + t for xx, t in zip(x, dots(xl, bds(xb), mm))]
        m *= 2
    s0 = [st_ref[g] for g in groups]
    bd_s = bds(bf(s0))
    bd_v = bds(v)
    w = [p + q for p, q in zip(dots(a_t, bd_s, nt), dots(l_ak, bd_v, mm))]
    u = bf(dots(bf(x), bds(bf(w)), mm))
    y = [p + q for p, q in zip(dots(rt, bd_s, nt), dots(m_r, rows(bds(u), bd_v), mm))]
    s_new = dots(rows(u, v), rows(b_h, k_hb), tn)
    for g in groups:
        st_ref[g] = s0[g] * p_end[g] + fold(s_new[g])
        out_ref = yf_ref if g < per_dir else yb_ref
        out_ref[:, gsl[g % per_dir]] = y[g]


def _rwkv_out_kernel(x_ref, yf_ref, yb_ref, r_ref, k_ref, v_ref, g_ref, as_ref, ka_ref, rk_ref, lnx_ref, wo_ref, o_ref):
    d = x_ref.shape[1]
    n = B_HEAD_DIM
    y = yf_ref[...] + yb_ref[...]
    kd = k_ref[...] * (2.0 + (as_ref[0] + as_ref[1] - 2.0) * ka_ref[...])
    prod = r_ref[...] * kd * rk_ref[...]
    low = lax.broadcasted_iota(I32, (1, LANES), 1) < n

    def head_sum(a):
        s_lo = jnp.sum(jnp.where(low, a, 0.0), axis=-1, keepdims=True)
        s_hi = jnp.sum(jnp.where(low, 0.0, a), axis=-1, keepdims=True)
        return jnp.where(low, s_lo, s_hi)

    parts = []
    for p in range(d // LANES):
        sl = slice(p * LANES, (p + 1) * LANES)
        ys = y[:, sl]
        cen = ys - head_sum(ys) * (1.0 / n)
        var = head_sum(cen * cen) * (1.0 / n)
        yn = cen * lax.rsqrt(var + B_GN_EPS) * lnx_ref[0:1, sl] + lnx_ref[1:2, sl]
        bonus = head_sum(prod[:, sl]) * v_ref[:, sl]
        parts.append(((yn + bonus) * g_ref[:, sl]).astype(BF16))
    o_ref[...] = x_ref[...] + jnp.dot(jnp.concatenate(parts, axis=-1), wo_ref[...], preferred_element_type=F32)


def mixer_b(x, ln, mu, w_rkv, w0, w1, w2, a0, a1, a2, g1, g2, k_k, k_a, r_k, ln_x, wo, batch, s_len):
    t, d = x.shape
    bf = lambda a: a.astype(BF16)
    tm = _row_tile(s_len, 256)
    tiles_per_seq = s_len // tm
    halo = 8
    row = lambda n: pl.BlockSpec((tm, n), lambda i: (i, 0))
    row2 = lambda n: pl.BlockSpec((2, tm, n), lambda i: (0, i, 0))
    full = lambda a: pl.BlockSpec(a.shape, lambda i: (0,) * a.ndim)
    ln2 = ln.reshape(1, d)
    weights = (ln2, mu, bf(w_rkv), bf(g1), bf(g2), w0, bf(w1), bf(w2), a0, bf(a1), bf(a2))
    tok = jax.ShapeDtypeStruct((t, d), F32)
    tok2 = jax.ShapeDtypeStruct((2, t, d), F32)
    r, k, v, g, lw, asig = pl.pallas_call(
        functools.partial(_rwkv_prep_kernel, tiles_per_seq=tiles_per_seq),
        grid=(t // tm,),
        in_specs=[row(d),
                  pl.BlockSpec((halo, d), lambda i: (jnp.maximum(i * (tm // halo) - 1, 0), 0)),
                  pl.BlockSpec((halo, d), lambda i: (jnp.minimum((i + 1) * (tm // halo), t // halo - 1), 0))]
                 + [full(a) for a in weights],
        out_specs=[row(d), row(d), row(d), row(d), row2(d), row2(d)],
        out_shape=[tok, tok, tok, tok, tok2, tok2],
        compiler_params=_params("parallel"),
        name="rwkv_prep",
    )(x, x, x, *weights)

    hb = WKV_HEADS_PER_STEP
    width = hb * B_HEAD_DIM
    c = min(WKV_CHUNK, s_len)
    nc = s_len // c
    assert width == d
    tok_f = pl.BlockSpec((None, c, d), lambda b, ci: (b, ci, 0))
    tok_b = pl.BlockSpec((None, c, d), lambda b, ci: (b, nc - 1 - ci, 0))
    dir_f = pl.BlockSpec((None, None, c, d), lambda b, ci: (0, b, ci, 0))
    dir_b = pl.BlockSpec((None, None, c, d), lambda b, ci: (1, b, nc - 1 - ci, 0))
    par = pl.BlockSpec((1, d), lambda b, ci: (0, 0))
    v3 = lambda a: a.reshape(batch, s_len, d)
    v4 = lambda a: a.reshape(2, batch, s_len, d)
    seq_out = jax.ShapeDtypeStruct((batch, s_len, d), F32)
    y_f, y_b = pl.pallas_call(
        functools.partial(_wkv_kernel, hb=hb),
        grid=(batch, nc),
        in_specs=[tok_f, tok_f, tok_f, tok_b, tok_b, tok_b, dir_f, dir_b, dir_f, dir_b, par, par],
        out_specs=[tok_f, tok_b],
        out_shape=[seq_out, seq_out],
        scratch_shapes=[pltpu.VMEM((2 * hb // WKV_PACK, B_HEAD_DIM, WKV_PACK * B_HEAD_DIM), F32)],
        compiler_params=_params("parallel", "arbitrary"),
        name="wkv_scan",
    )(v3(r), v3(k), v3(v), v3(r), v3(k), v3(v), v4(lw), v4(lw), v4(asig), v4(asig),
      k_k.reshape(1, d), k_a.reshape(1, d))

    params = (k_a.reshape(1, d), r_k.reshape(1, d), ln_x, bf(wo))
    return pl.pallas_call(
        _rwkv_out_kernel,
        grid=(t // tm,),
        in_specs=[row(d), row(d), row(d), row(d), row(d), row(d), row(d), row2(d)] + [full(a) for a in params],
        out_specs=row(d),
        out_shape=tok,
        compiler_params=_params("parallel"),
        name="rwkv_out",
    )(x, y_f.reshape(t, d), y_b.reshape(t, d), r, k, v, g, asig, *params)


TOP_K = 2
ROUTE_EXPERT, ROUTE_GATE, ROUTE_RANK = 0, TOP_K, 2 * TOP_K
MOE_ROW_TILE = 512
ROW_DMA_UNROLL = 8


def _router_kernel(x_ref, g_ref, wr_ref, hn_ref, route_ref, cnt_ref, run_ref, *, n_exp):
    @pl.when(pl.program_id(0) == 0)
    def _():
        run_ref[...] = jnp.zeros_like(run_ref)

    tm = x_ref.shape[0]
    h = _rms(x_ref[...], g_ref[...])
    hn_ref[...] = h
    logits = jnp.dot(h, wr_ref[...], preferred_element_type=F32, precision=lax.Precision.HIGHEST)
    lane = lax.broadcasted_iota(I32, (tm, LANES), 1)
    logits = jnp.where(lane < n_exp, logits, NEG_INF)

    def top1(v):
        m = jnp.max(v, axis=-1, keepdims=True)
        return m, jnp.min(jnp.where(v == m, lane, LANES), axis=-1, keepdims=True)

    m1, i1 = top1(logits)
    m2, i2 = top1(jnp.where(lane == i1, NEG_INF, logits))
    e = jnp.exp(m2 - m1)
    g1 = 1.0 / (1.0 + e)
    g2 = e * g1
    sel = ((lane == i1) | (lane == i2)).astype(F32)
    earlier = (lax.broadcasted_iota(I32, (tm, tm), 0) > lax.broadcasted_iota(I32, (tm, tm), 1)).astype(BF16)
    before = jnp.dot(earlier, sel.astype(BF16), preferred_element_type=F32) + run_ref[...]
    r1 = jnp.sum(jnp.where(lane == i1, before, 0.0), axis=-1, keepdims=True)
    r2 = jnp.sum(jnp.where(lane == i2, before, 0.0), axis=-1, keepdims=True)
    run_ref[...] += jnp.sum(sel, axis=0, keepdims=True)
    cnt_ref[...] = run_ref[...]
    route = jnp.zeros((tm, LANES), F32)
    for off, vals in ((ROUTE_EXPERT, (i1.astype(F32), i2.astype(F32))), (ROUTE_GATE, (g1, g2)), (ROUTE_RANK, (r1, r2))):
        for kk, v in enumerate(vals):
            route = jnp.where(lane == off + kk, v, route)
    route_ref[...] = route


def _dispatch_kernel(dest_ref, hn_ref, xs_in_ref, xs_ref, sem):
    del xs_in_ref
    tm = hn_ref.shape[0]

    def row_copy(r, slot):
        return pltpu.make_async_copy(hn_ref.at[pl.ds(r, 1)], xs_ref.at[pl.ds(dest_ref[TOP_K * r + slot], 1)], sem)

    def start(r, c):
        for slot in range(TOP_K):
            row_copy(r, slot).start()
        return c

    def wait(r, c):
        for slot in range(TOP_K):
            row_copy(r, slot).wait()
        return c

    lax.fori_loop(0, tm, start, 0, unroll=ROW_DMA_UNROLL)
    lax.fori_loop(0, tm, wait, 0, unroll=ROW_DMA_UNROLL)


def _expert_ffn_kernel(te_ref, x_ref, w1_ref, w3_ref, w2_ref, o_ref, xb_ref, acc_ref):
    del te_ref
    j = pl.program_id(1)

    @pl.when(j == 0)
    def _():
        xb_ref[...] = x_ref[...].astype(BF16)
        acc_ref[...] = jnp.zeros_like(acc_ref)

    acc_ref[...] += _swiglu_chunks(xb_ref[...], w1_ref, w3_ref, w2_ref)

    @pl.when(j == pl.num_programs(1) - 1)
    def _():
        o_ref[...] = acc_ref[...]


def _moe_combine_kernel(dest_ref, route_ref, x_ref, ys_ref, o_ref, buf_ref, sem):
    tm = x_ref.shape[0]

    def row_copy(r, slot):
        return pltpu.make_async_copy(ys_ref.at[pl.ds(dest_ref[TOP_K * r + slot], 1)], buf_ref.at[slot, pl.ds(r, 1)], sem)

    def start(r, c):
        for slot in range(TOP_K):
            row_copy(r, slot).start()
        return c

    def wait(r, c):
        for slot in range(TOP_K):
            row_copy(r, slot).wait()
        return c

    lax.fori_loop(0, tm, start, 0, unroll=ROW_DMA_UNROLL)
    lax.fori_loop(0, tm, wait, 0, unroll=ROW_DMA_UNROLL)
    route = route_ref[...]
    out = x_ref[...]
    for slot in range(TOP_K):
        out = out + route[:, ROUTE_GATE + slot:ROUTE_GATE + slot + 1] * buf_ref[slot]
    o_ref[...] = out


def moe_ffn(x, ln, router, w13, w2, tf=1792):
    t, d = x.shape
    n_exp, dff = w2.shape[0], w2.shape[1]
    assert dff % tf == 0
    nj = dff // tf
    tmx = min(MOE_ROW_TILE, t)
    tm = _row_tile(t, 512)
    wr = jnp.pad(router, ((0, 0), (0, LANES - n_exp)))
    hn, route, cnt = pl.pallas_call(
        functools.partial(_router_kernel, n_exp=n_exp),
        grid=(t // tm,),
        in_specs=[pl.BlockSpec((tm, d), lambda i: (i, 0)),
                  pl.BlockSpec((1, d), lambda i: (0, 0)),
                  pl.BlockSpec((d, LANES), lambda i: (0, 0))],
        out_specs=[pl.BlockSpec((tm, d), lambda i: (i, 0)),
                   pl.BlockSpec((tm, LANES), lambda i: (i, 0)),
                   pl.BlockSpec((1, LANES), lambda i: (0, 0))],
        out_shape=[jax.ShapeDtypeStruct((t, d), F32),
                   jax.ShapeDtypeStruct((t, LANES), F32),
                   jax.ShapeDtypeStruct((1, LANES), F32)],
        scratch_shapes=[pltpu.VMEM((1, LANES), F32)],
        compiler_params=_params("arbitrary"),
        name="moe_router",
    )(x, ln.reshape(1, d), wr)

    expert = route[:, ROUTE_EXPERT:ROUTE_EXPERT + TOP_K].astype(I32)
    rank = route[:, ROUTE_RANK:ROUTE_RANK + TOP_K].astype(I32)
    counts = cnt[0, :n_exp].astype(I32)
    padded = (counts + tmx - 1) // tmx * tmx
    ends = jnp.cumsum(padded)
    dest = ((ends - padded)[expert] + rank).reshape(TOP_K * t)
    n_tiles = (TOP_K * t) // tmx + n_exp
    rows = n_tiles * tmx
    tile_start = jnp.arange(n_tiles, dtype=I32) * tmx
    tile_expert = jnp.minimum(jnp.sum((tile_start[:, None] >= ends[None, :]).astype(I32), axis=1), n_exp - 1)

    smem_rows = lambda n: pl.BlockSpec((TOP_K * n,), lambda i: (i,), memory_space=pltpu.SMEM)
    xs = pl.pallas_call(
        _dispatch_kernel,
        grid=(t // tm,),
        in_specs=[smem_rows(tm),
                  pl.BlockSpec((tm, d), lambda i: (i, 0)),
                  pl.BlockSpec(memory_space=pl.ANY)],
        out_specs=pl.BlockSpec(memory_space=pl.ANY),
        out_shape=jax.ShapeDtypeStruct((rows, d), F32),
        scratch_shapes=[pltpu.SemaphoreType.DMA(())],
        input_output_aliases={2: 0},
        compiler_params=_params("arbitrary"),
        name="moe_dispatch",
    )(dest, hn, jnp.zeros((rows, d), F32))

    ys = pl.pallas_call(
        _expert_ffn_kernel,
        grid_spec=pltpu.PrefetchScalarGridSpec(
            num_scalar_prefetch=1,
            grid=(n_tiles, nj),
            in_specs=[pl.BlockSpec((tmx, d), lambda i, j, te: (i, 0)),
                      pl.BlockSpec((None, d, tf), lambda i, j, te: (te[i], 0, j)),
                      pl.BlockSpec((None, d, tf), lambda i, j, te: (te[i], 0, nj + j)),
                      pl.BlockSpec((None, tf, d), lambda i, j, te: (te[i], j, 0))],
            out_specs=pl.BlockSpec((tmx, d), lambda i, j, te: (i, 0)),
            scratch_shapes=[pltpu.VMEM((tmx, d), BF16), pltpu.VMEM((tmx, d), F32)]),
        out_shape=jax.ShapeDtypeStruct((rows, d), F32),
        compiler_params=_params("parallel", "arbitrary"),
        name="moe_expert_ffn",
    )(tile_expert, xs, w13, w13, w2)

    return pl.pallas_call(
        _moe_combine_kernel,
        grid=(t // tm,),
        in_specs=[smem_rows(tm),
                  pl.BlockSpec((tm, LANES), lambda i: (i, 0)),
                  pl.BlockSpec((tm, d), lambda i: (i, 0)),
                  pl.BlockSpec(memory_space=pl.ANY)],
        out_specs=pl.BlockSpec((tm, d), lambda i: (i, 0)),
        out_shape=jax.ShapeDtypeStruct((t, d), F32),
        scratch_shapes=[pltpu.VMEM((TOP_K, tm, d), F32), pltpu.SemaphoreType.DMA(())],
        compiler_params=_params("arbitrary"),
        name="moe_combine",
    )(dest, route, x, ys)


def kernel(x, positions, ln_mix, ln_ffn, a_wqkv, a_qk_norm, a_wo, b_mu, b_wrkv, b_w0, b_w1, b_w2, b_a0, b_a1, b_a2, b_g1, b_g2, b_kk, b_ka, b_rk, b_lnx, b_wo, c_wa, c_qa_norm, c_kva_norm, c_wq_b, c_wkv_b, c_qk_norm, c_wo, f_w13, f_w2, m_router, m_w13, m_w2):
    batch, s_len, d = x.shape
    t = batch * s_len
    cs, sn = _rope_tables(positions)
    cs, sn = cs.reshape(t, LANES), sn.reshape(t, LANES)
    h = x.reshape(t, d)
    n_mixers = 3
    for i in range(ln_mix.shape[0]):
        j, kind = divmod(i, n_mixers)
        if kind == 0:
            h = mixer_a(h, ln_mix[i], a_wqkv[j], a_qk_norm[j], a_wo[j], cs, sn, batch, s_len)
        elif kind == 1:
            h = mixer_b(h, ln_mix[i], b_mu[j], b_wrkv[j], b_w0[j], b_w1[j], b_w2[j], b_a0[j], b_a1[j], b_a2[j],
                        b_g1[j], b_g2[j], b_kk[j], b_ka[j], b_rk[j], b_lnx[j], b_wo[j], batch, s_len)
        else:
            h = mixer_c(h, ln_mix[i], c_wa[j], c_qa_norm[j], c_kva_norm[j], c_wq_b[j], c_wkv_b[j], c_qk_norm[j],
                        c_wo[j], cs, sn, batch, s_len)
        f = i // 2
        if i % 2 == 0:
            h = dense_ffn(h, ln_ffn[i], f_w13[f].astype(BF16), f_w2[f].astype(BF16))
        else:
            h = moe_ffn(h, ln_ffn[i], m_router[f], m_w13[f].astype(BF16), m_w2[f].astype(BF16))
    return h.reshape(batch, s_len, d)
```

```python
import functools
import math

import jax
import jax.numpy as jnp
from jax import lax
from jax.experimental import pallas as pl
from jax.experimental.pallas import tpu as pltpu

F32 = jnp.float32
BF16 = jnp.bfloat16
I32 = jnp.int32

NORM_EPS = 1e-6
ROPE_THETA = 500000.0
NEG_INF = -1e30
LANES = 128
ROT_DIM = 32
ROT_HALF = ROT_DIM // 2
DILATED_PAIRS = ((128, 1), (512, 4), (2048, 16))
A_HEAD_DIM = 128
VMEM_LIMIT = 48 * 1024 * 1024


def _params(*sem):
    return pltpu.CompilerParams(dimension_semantics=sem, vmem_limit_bytes=VMEM_LIMIT)


def _rms(x, g):
    return x * lax.rsqrt(jnp.mean(x * x, axis=-1, keepdims=True) + NORM_EPS) * g


def _row_tile(t, want):
    tm = min(t, want)
    assert t % tm == 0
    return tm


def _norm_matmul_kernel(x_ref, g_ref, w_ref, o_ref, xn_ref):
    @pl.when(pl.program_id(1) == 0)
    def _():
        xn_ref[...] = _rms(x_ref[...], g_ref[...]).astype(BF16)

    o_ref[...] = jnp.dot(xn_ref[...], w_ref[...], preferred_element_type=F32).astype(o_ref.dtype)


def norm_matmul(x, g, w, out_dtype, tn):
    t, k = x.shape
    n = w.shape[1]
    tm = _row_tile(t, 1024)
    assert n % tn == 0
    return pl.pallas_call(
        _norm_matmul_kernel,
        grid=(t // tm, n // tn),
        in_specs=[pl.BlockSpec((tm, k), lambda i, j: (i, 0)),
                  pl.BlockSpec((1, k), lambda i, j: (0, 0)),
                  pl.BlockSpec((k, tn), lambda i, j: (0, j))],
        out_specs=pl.BlockSpec((tm, tn), lambda i, j: (i, j)),
        out_shape=jax.ShapeDtypeStruct((t, n), out_dtype),
        scratch_shapes=[pltpu.VMEM((tm, k), BF16)],
        compiler_params=_params("parallel", "arbitrary"),
        name="norm_matmul",
    )(x, g.reshape(1, k), w)


def _matmul_res_kernel(a_ref, w_ref, x_ref, o_ref):
    o_ref[...] = x_ref[...] + jnp.dot(a_ref[...], w_ref[...], preferred_element_type=F32)


def matmul_residual(a, w, x):
    t, k = a.shape
    n = w.shape[1]
    tm = _row_tile(t, 1024)
    return pl.pallas_call(
        _matmul_res_kernel,
        grid=(t // tm,),
        in_specs=[pl.BlockSpec((tm, k), lambda i: (i, 0)),
                  pl.BlockSpec((k, n), lambda i: (0, 0)),
                  pl.BlockSpec((tm, n), lambda i: (i, 0))],
        out_specs=pl.BlockSpec((tm, n), lambda i: (i, 0)),
        out_shape=jax.ShapeDtypeStruct((t, n), F32),
        compiler_params=_params("parallel"),
        name="matmul_residual",
    )(a, w, x)


def _rope_tables(positions):
    b, s = positions.shape
    inv_freq = ROPE_THETA ** (-jnp.arange(0, ROT_DIM, 2, dtype=F32) / ROT_DIM)
    ang = positions.astype(F32)[..., None] * inv_freq
    c, sn = jnp.cos(ang), jnp.sin(ang)
    rest = (b, s, LANES - ROT_DIM)
    cs_t = jnp.concatenate([c, c, jnp.ones(rest, F32)], axis=-1)
    sn_t = jnp.concatenate([-sn, sn, jnp.zeros(rest, F32)], axis=-1)
    return cs_t, sn_t


def _rope(x, cs, sn, lane):
    n = x.shape[-1]
    partner = jnp.where(lane < ROT_HALF, pltpu.roll(x, n - ROT_HALF, 1), pltpu.roll(x, ROT_HALF, 1))
    return x * cs + partner * sn


def _dil_attn_kernel(q_ref, k_ref, v_ref, o_ref, lse_ref, *, seq, qt, kw, half, heads):
    hd = A_HEAD_DIM
    qi = pl.program_id(2)
    q0 = pl.multiple_of(qi * qt, qt)
    start = jnp.clip(q0 - half, 0, seq - kw)
    start = pl.multiple_of(start, math.gcd(half, qt))
    qpos = q0 + lax.broadcasted_iota(I32, (qt, kw), 0)
    kpos = start + lax.broadcasted_iota(I32, (qt, kw), 1)
    valid = jnp.abs(qpos - kpos) <= half
    head_of_lane = lax.broadcasted_iota(I32, (1, LANES), 1) // (LANES // heads)
    ones = jnp.ones((kw, hd), BF16)
    work = [(r, slice(h * hd, (h + 1) * hd)) for r in range(q_ref.shape[0]) for h in range(heads)]
    s = [lax.dot_general(q_ref[r, :, sl], k_ref[r, pl.ds(start, kw), sl], (((1,), (1,)), ((), ())),
                         preferred_element_type=F32) for r, sl in work]
    s = [jnp.where(valid, t, NEG_INF) for t in s]
    m = [jnp.max(t, axis=-1, keepdims=True) for t in s]
    p = [jnp.exp(t - mm).astype(BF16) for t, mm in zip(s, m)]
    ol = [jnp.dot(pp, jnp.concatenate([v_ref[r, pl.ds(start, kw), sl], ones], axis=1), preferred_element_type=F32)
          for pp, (r, sl) in zip(p, work)]
    for r in range(q_ref.shape[0]):
        lse_row = jnp.zeros((qt, LANES), F32)
        for h in range(heads):
            i = r * heads + h
            o_ref[r, :, work[i][1]] = (ol[i][:, :hd] / ol[i][:, hd:]).astype(o_ref.dtype)
            lse_row = jnp.where(head_of_lane == h, m[i] + jnp.log(ol[i][:, hd:hd + 1]), lse_row)
        lse_ref[r] = lse_row


DIL_ATTN_ROWS = 512
A_PAIR = 2 * A_HEAD_DIM


def _head_helper_matrices():
    i = jnp.arange(A_PAIR)
    same_head = (i[:, None] // A_HEAD_DIM) == (i[None, :] // A_HEAD_DIM)
    dst = i % A_HEAD_DIM
    src = jnp.where(dst < ROT_HALF, i + ROT_HALF, i - ROT_HALF)
    perm = (i[:, None] == src[None, :]) & (dst[None, :] < ROT_DIM)
    return same_head.astype(BF16), perm.astype(BF16)


def _qkv_group_kernel(x_ref, g_ref, w_ref, cs_ref, sn_ref, gain_ref, mean_ref, perm_ref, o_ref, xn_ref, acc_ref, *, dil):
    j = pl.program_id(1)

    @pl.when(j == 0)
    def _():
        xn_ref[...] = _rms(x_ref[...], g_ref[...]).astype(BF16)

    acc = jnp.dot(xn_ref[...], w_ref[...], preferred_element_type=F32)
    tm, width = acc.shape

    def emit(vals):
        if dil == 1:
            o_ref[0, 0] = vals.astype(o_ref.dtype)
            return
        chunks = width // LANES
        for c in range(chunks):
            acc_ref[c] = vals[:, c * LANES:(c + 1) * LANES]
        for r in range(dil):
            o_ref[0, r] = jnp.concatenate([acc_ref[c, pl.ds(r, tm // dil, stride=dil), :] for c in range(chunks)],
                                          axis=-1).astype(o_ref.dtype)

    @pl.when(j < 2)
    def _():
        cs = jnp.concatenate([cs_ref[...]] * 2, axis=1)
        sn = jnp.concatenate([sn_ref[...]] * 2, axis=1)
        gain = gain_ref[pl.ds(j, 1), :]
        parts = []
        for p in range(width // A_PAIR):
            a = acc[:, p * A_PAIR:(p + 1) * A_PAIR]
            meansq = jnp.dot((a * a).astype(BF16), mean_ref[...], preferred_element_type=F32)
            ag = a * gain[:, p * A_PAIR:(p + 1) * A_PAIR]
            pair = jnp.dot(ag.astype(BF16), perm_ref[...], preferred_element_type=F32)
            parts.append((ag * cs + pair * sn) * lax.rsqrt(meansq + NORM_EPS))
        emit(jnp.concatenate(parts, axis=1))

    @pl.when(j == 2)
    def _():
        emit(acc)


def qkv_group_projection(x, ln, wqkv, cs, sn, gq, gk, group, batch, s_len, width):
    t, d = x.shape
    dil = DILATED_PAIRS[group][1]
    n_groups = len(DILATED_PAIRS)
    heads = width // A_HEAD_DIM
    tm = _row_tile(s_len, 1024)
    tiles = s_len // tm
    assert tm % (16 * dil) == 0
    gains = jnp.stack([jnp.tile(gq, heads) * A_HEAD_DIM ** -0.5, jnp.tile(gk, heads)])
    ones, perm = _head_helper_matrices()
    ones = (ones.astype(F32) * (1.0 / A_HEAD_DIM)).astype(BF16)
    const = lambda a: pl.BlockSpec(a.shape, lambda i, j: (0, 0))
    return pl.pallas_call(
        functools.partial(_qkv_group_kernel, dil=dil),
        grid=(t // tm, 3),
        in_specs=[pl.BlockSpec((tm, d), lambda i, j: (i, 0)),
                  pl.BlockSpec((1, d), lambda i, j: (0, 0)),
                  pl.BlockSpec((d, width), lambda i, j: (0, j * n_groups + group)),
                  pl.BlockSpec((tm, LANES), lambda i, j: (i, 0)),
                  pl.BlockSpec((tm, LANES), lambda i, j: (i, 0)),
                  const(gains), const(ones), const(perm)],
        out_specs=pl.BlockSpec((1, dil, tm // dil, width), lambda i, j: (i // tiles, 0, i % tiles, j)),
        out_shape=jax.ShapeDtypeStruct((batch, dil, s_len // dil, 3 * width), BF16),
        scratch_shapes=[pltpu.VMEM((tm, d), BF16), pltpu.VMEM((width // LANES, tm, LANES), F32)],
        compiler_params=_params("parallel", "arbitrary"),
        name=f"qkv_projection_g{group}",
    )(x, ln.reshape(1, d), wqkv, cs, sn, gains, ones, perm)


def dilated_group_attention(qkv, group, batch, s_len, heads):
    window, dil = DILATED_PAIRS[group]
    half = window // (2 * dil)
    seq = s_len // dil
    width = heads * A_HEAD_DIM
    qt = min(256, seq)
    kw = min(qt + 2 * half, seq)
    assert seq % qt == 0 and (seq - kw) % math.gcd(half, qt) == 0
    rb = max(1, min(dil, DIL_ATTN_ROWS // qt))
    assert dil % rb == 0
    whole = lambda c: pl.BlockSpec((None, rb, seq, width), lambda b, r, qi: (b, r, 0, c))
    tile = lambda n: pl.BlockSpec((None, rb, qt, n), lambda b, r, qi: (b, r, qi, 0))
    kern = functools.partial(_dil_attn_kernel, seq=seq, qt=qt, kw=kw, half=half, heads=heads)
    return pl.pallas_call(
        kern,
        grid=(batch, dil // rb, seq // qt),
        in_specs=[tile(width), whole(1), whole(2)],
        out_specs=[tile(width), tile(LANES)],
        out_shape=[jax.ShapeDtypeStruct((batch, dil, seq, width), BF16),
                   jax.ShapeDtypeStruct((batch, dil, seq, LANES), F32)],
        compiler_params=_params("parallel", "parallel", "parallel"),
        name=f"dilated_attention_g{group}",
    )(qkv, qkv, qkv)


def _combine_wo_kernel(x_ref, o0_ref, o1_ref, o2_ref, l0_ref, l1_ref, l2_ref, w_ref, out_ref, ot_ref, lt_ref, *, heads):
    hd = A_HEAD_DIM
    rep = LANES // heads
    tm = x_ref.shape[0]

    def token_order(src_ref, dst_ref, g):
        dil = src_ref.shape[0]
        if dil == 1:
            return src_ref[0].astype(F32)
        chunks = src_ref.shape[-1] // LANES
        for r in range(dil):
            blk = src_ref[r].astype(F32)
            for c in range(chunks):
                dst_ref[g, c, pl.ds(r, tm // dil, stride=dil), :] = blk[:, c * LANES:(c + 1) * LANES]
        return jnp.concatenate([dst_ref[g, c] for c in range(chunks)], axis=-1)

    ls = [token_order(ref, lt_ref, g) for g, ref in enumerate((l0_ref, l1_ref, l2_ref))]
    os_ = [token_order(ref, ot_ref, g) for g, ref in enumerate((o0_ref, o1_ref, o2_ref))]
    m = jnp.maximum(jnp.maximum(ls[0], ls[1]), ls[2])
    es = [jnp.exp(l - m) for l in ls]
    inv = 1.0 / (es[0] + es[1] + es[2])
    al = [e * inv for e in es]
    parts = []
    for h in range(heads):
        sl = slice(h * hd, (h + 1) * hd)
        c = slice(h * rep, h * rep + 1)
        parts.append((al[0][:, c] * os_[0][:, sl] + al[1][:, c] * os_[1][:, sl] + al[2][:, c] * os_[2][:, sl]).astype(BF16))
    o = jnp.concatenate(parts, axis=-1)
    out_ref[...] = x_ref[...] + jnp.dot(o, w_ref[...], preferred_element_type=F32)


def combine_wo(x, outs, lses, wo, heads, batch, s_len):
    t, d = x.shape
    width = wo.shape[0]
    n_groups = len(DILATED_PAIRS)
    tm = _row_tile(s_len, 512)
    tiles = s_len // tm
    assert all(tm % (16 * dil) == 0 for _, dil in DILATED_PAIRS)
    row = lambda n: pl.BlockSpec((tm, n), lambda i: (i, 0))
    res = lambda g, n: pl.BlockSpec((None, DILATED_PAIRS[g][1], tm // DILATED_PAIRS[g][1], n),
                                    lambda i: (i // tiles, 0, i % tiles, 0))
    return pl.pallas_call(
        functools.partial(_combine_wo_kernel, heads=heads),
        grid=(t // tm,),
        in_specs=[row(d)] + [res(g, width) for g in range(n_groups)] + [res(g, LANES) for g in range(n_groups)]
                 + [pl.BlockSpec((width, d), lambda i: (0, 0))],
        out_specs=row(d),
        out_shape=jax.ShapeDtypeStruct((t, d), F32),
        scratch_shapes=[pltpu.VMEM((n_groups, width // LANES, tm, LANES), F32),
                        pltpu.VMEM((n_groups, 1, tm, LANES), F32)],
        compiler_params=_params("parallel"),
        name="combine_wo",
    )(x, *outs, *lses, wo)


def mixer_a(x, ln, wqkv, qk_norm, wo, cs, sn, batch, s_len):
    heads = wo.shape[0] // A_HEAD_DIM
    wqkv = wqkv.astype(BF16)
    outs, lses = [], []
    for g in range(len(DILATED_PAIRS)):
        qkv = qkv_group_projection(x, ln, wqkv, cs, sn, qk_norm[0, g], qk_norm[1, g], g, batch, s_len, wo.shape[0])
        o, lse = dilated_group_attention(qkv, g, batch, s_len, heads)
        outs.append(o)
        lses.append(lse)
    return combine_wo(x, outs, lses, wo.astype(BF16), heads, batch, s_len)


def _ffn_kernel(x_ref, g_ref, w1_ref, w3_ref, w2_ref, o_ref, xn_ref, acc_ref):
    j = pl.program_id(1)

    @pl.when(j == 0)
    def _():
        xn_ref[...] = _rms(x_ref[...], g_ref[...]).astype(BF16)
        acc_ref[...] = jnp.zeros_like(acc_ref)

    acc_ref[...] += _swiglu_chunks(xn_ref[...], w1_ref, w3_ref, w2_ref)

    @pl.when(j == pl.num_programs(1) - 1)
    def _():
        o_ref[...] = x_ref[...] + acc_ref[...]


FFN_SUB = 256


def _swiglu_chunks(xn, w1_ref, w3_ref, w2_ref):
    tf = w2_ref.shape[0]
    assert tf % FFN_SUB == 0
    total = None
    for s in range(tf // FFN_SUB):
        sl = slice(s * FFN_SUB, (s + 1) * FFN_SUB)
        gate = jnp.dot(xn, w1_ref[:, sl], preferred_element_type=F32)
        up = jnp.dot(xn, w3_ref[:, sl], preferred_element_type=F32)
        act = (gate * jax.nn.sigmoid(gate) * up).astype(BF16)
        part = jnp.dot(act, w2_ref[sl, :], preferred_element_type=F32)
        total = part if total is None else total + part
    return total


def dense_ffn(x, ln, w13, w2):
    t, d = x.shape
    tf = dff = w2.shape[0]
    nj = dff // tf
    tm = _row_tile(t, 512)
    resident = dict(pipeline_mode=pl.Buffered(1))
    return pl.pallas_call(
        _ffn_kernel,
        grid=(t // tm, nj),
        in_specs=[pl.BlockSpec((tm, d), lambda i, j: (i, 0)),
                  pl.BlockSpec((1, d), lambda i, j: (0, 0)),
                  pl.BlockSpec((d, tf), lambda i, j: (0, j), **resident),
                  pl.BlockSpec((d, tf), lambda i, j: (0, nj + j), **resident),
                  pl.BlockSpec((tf, d), lambda i, j: (j, 0), **resident)],
        out_specs=pl.BlockSpec((tm, d), lambda i, j: (i, 0)),
        out_shape=jax.ShapeDtypeStruct((t, d), F32),
        scratch_shapes=[pltpu.VMEM((tm, d), BF16), pltpu.VMEM((tm, d), F32)],
        compiler_params=_params("parallel", "arbitrary"),
        name="dense_ffn",
    )(x, ln.reshape(1, d), w13, w13, w2)


C_ROPE = ROT_DIM
C_NOPE = 64
C_V = 64
C_QK = C_ROPE + C_NOPE
LOG2_E = math.log2(math.e)


def _mla_qkv_kernel(lat_ref, qa_ref, kva_ref, gq_ref, gk_ref, wq_ref, wkv_ref, cs_ref, sn_ref,
                    ones_ref, perm_ref, q_ref, k_ref, vt_ref, *, heads, q_lora, kv_lora):
    two = lambda a: jnp.concatenate([a, a], axis=1)
    cs, sn = two(cs_ref[...]), two(sn_ref[...])
    lat = lat_ref[...]
    qn = _rms(lat[:, :q_lora], qa_ref[...]).astype(BF16)
    kvn = _rms(lat[:, q_lora:q_lora + kv_lora], kva_ref[...]).astype(BF16)
    k_rope = two(lat[:, q_lora + kv_lora:])

    def head_norm(a, g, post):
        sumsq = jnp.dot((a * a).astype(BF16), ones_ref[...], preferred_element_type=F32)
        ag = a * g
        pair = jnp.dot(ag.astype(BF16), perm_ref[...], preferred_element_type=F32)
        return (ag * cs + pair * sn) * (lax.rsqrt(sumsq * (1.0 / C_QK) + NORM_EPS) * post)

    gq, gk = two(gq_ref[...]), two(gk_ref[...])
    kv = jnp.dot(kvn, wkv_ref[...], preferred_element_type=F32)
    for p in range(heads * LANES // A_PAIR):
        sl = slice(p * A_PAIR, (p + 1) * A_PAIR)
        q = jnp.dot(qn, wq_ref[:, sl], preferred_element_type=F32)
        q_ref[:, sl] = head_norm(q, gq, C_QK ** -0.5 * LOG2_E).astype(BF16)
        k_ref[:, sl] = head_norm(kv[:, sl] + k_rope, gk, 1.0).astype(BF16)
    vt_ref[...] = kv[:, heads * LANES:].T.astype(BF16)


def _mla_attn_kernel(q_ref, k_ref, vt_ref, o_ref, *, pack):
    outs = []
    for i in range(pack):
        q = q_ref[0, :, i * LANES:(i + 1) * LANES]
        k = k_ref[0, :, i * LANES:(i + 1) * LANES]
        st = lax.dot_general(k, q, (((1,), (1,)), ((), ())), preferred_element_type=F32)
        m = jnp.max(st, axis=0, keepdims=True)
        p = jnp.exp2(st - m)
        l = jnp.sum(p, axis=0, keepdims=True)
        vt = vt_ref[0, i * C_V:(i + 1) * C_V, :]
        outs.append(jnp.dot(vt, p.astype(BF16), preferred_element_type=F32) / l)
    o_ref[0] = jnp.concatenate(outs, axis=0).astype(o_ref.dtype)


def _matmul_res_t_kernel(at_ref, w_ref, x_ref, o_ref):
    o_ref[...] = x_ref[...] + lax.dot_general(at_ref[...], w_ref[...], (((0,), (0,)), ((), ())),
                                              preferred_element_type=F32)


def matmul_residual_t(a_t, w, x, batch, s_len):
    k, n = w.shape
    tm = _row_tile(s_len, 512)
    tiles = s_len // tm
    return pl.pallas_call(
        _matmul_res_t_kernel,
        grid=(batch * tiles,),
        in_specs=[pl.BlockSpec((None, k, tm), lambda i: (i // tiles, 0, i % tiles)),
                  pl.BlockSpec((k, n), lambda i: (0, 0)),
                  pl.BlockSpec((tm, n), lambda i: (i, 0))],
        out_specs=pl.BlockSpec((tm, n), lambda i: (i, 0)),
        out_shape=jax.ShapeDtypeStruct(x.shape, F32),
        compiler_params=_params("parallel"),
        name="matmul_residual_t",
    )(a_t, w, x)


def _pad_cols(w, heads, real, slot):
    k = w.shape[0]
    w = w.reshape(k, heads, real)
    return jnp.pad(w, ((0, 0), (0, 0), (0, slot - real))).reshape(k, heads * slot)


def mixer_c(x, ln, wa, qa_norm, kva_norm, wq_b, wkv_b, qk_norm, wo, cs, sn, batch, s_len):
    t, d = x.shape
    q_lora, kv_lora = qa_norm.shape[0], kva_norm.shape[0]
    heads = wo.shape[0] // C_V
    lat_w = q_lora + kv_lora + LANES
    wa_p = jnp.pad(wa, ((0, 0), (0, lat_w - wa.shape[1]))).astype(BF16)
    lat = norm_matmul(x, ln, wa_p, F32, tn=lat_w)
    wq_p = _pad_cols(wq_b, heads, C_QK, LANES).astype(BF16)
    wkv = wkv_b.reshape(kv_lora, heads, C_NOPE + C_V)
    wk_p = jnp.pad(wkv[:, :, :C_NOPE], ((0, 0), (0, 0), (C_ROPE, LANES - C_QK))).reshape(kv_lora, heads * LANES)
    wv = wkv[:, :, C_NOPE:].reshape(kv_lora, heads * C_V)
    wkv_p = jnp.concatenate([wk_p, wv], axis=1).astype(BF16)
    gq = jnp.pad(qk_norm[0], (0, LANES - C_QK)).reshape(1, LANES)
    gk = jnp.pad(qk_norm[1], (0, LANES - C_QK)).reshape(1, LANES)
    tm = _row_tile(s_len, 512)
    tiles = s_len // tm
    row = lambda n: pl.BlockSpec((tm, n), lambda i: (i, 0))
    full = lambda a: pl.BlockSpec(a.shape, lambda i: (0, 0))
    qa2, kva2 = qa_norm.reshape(1, -1), kva_norm.reshape(1, -1)
    ones, perm = _head_helper_matrices()
    q, k, vt = pl.pallas_call(
        functools.partial(_mla_qkv_kernel, heads=heads, q_lora=q_lora, kv_lora=kv_lora),
        grid=(t // tm,),
        in_specs=[row(lat_w), full(qa2), full(kva2), full(gq), full(gk), full(wq_p), full(wkv_p),
                  row(LANES), row(LANES), full(ones), full(perm)],
        out_specs=[row(heads * LANES), row(heads * LANES),
                   pl.BlockSpec((None, heads * C_V, tm), lambda i: (i // tiles, 0, i % tiles))],
        out_shape=[jax.ShapeDtypeStruct((t, heads * LANES), BF16),
                   jax.ShapeDtypeStruct((t, heads * LANES), BF16),
                   jax.ShapeDtypeStruct((batch, heads * C_V, s_len), BF16)],
        compiler_params=_params("parallel"),
        name="mla_qkv",
    )(lat, qa2, kva2, gq, gk, wq_p, wkv_p, cs, sn, ones, perm)
    pack = LANES // C_V
    tq = min(512, s_len)
    o_t = pl.pallas_call(
        functools.partial(_mla_attn_kernel, pack=pack),
        grid=(batch, heads // pack, s_len // tq),
        in_specs=[pl.BlockSpec((1, tq, pack * LANES), lambda b, h, i: (b, i, h)),
                  pl.BlockSpec((1, s_len, pack * LANES), lambda b, h, i: (b, 0, h)),
                  pl.BlockSpec((1, pack * C_V, s_len), lambda b, h, i: (b, h, 0))],
        out_specs=pl.BlockSpec((1, pack * C_V, tq), lambda b, h, i: (b, h, i)),
        out_shape=jax.ShapeDtypeStruct((batch, heads * C_V, s_len), BF16),
        compiler_params=_params("parallel", "parallel", "arbitrary"),
        name="mla_attention",
    )(q.reshape(batch, s_len, -1), k.reshape(batch, s_len, -1), vt)
    return matmul_residual_t(o_t, wo.astype(BF16), x, batch, s_len)


B_HEAD_DIM = 64
B_DECAY_SCALE = math.exp(-0.5)
B_GN_EPS = 64e-5
WKV_CHUNK = 64
WKV_HEADS_PER_STEP = 16
WKV_PACK = 4


def _bdot(a, b):
    return jnp.dot(a.astype(BF16), b.astype(BF16), preferred_element_type=F32)


def _rwkv_prep_kernel(x_ref, xp_ref, xn_ref, ln_ref, mu_ref, wrkv_ref, g1_ref, g2_ref, w0_ref, w1_ref, w2_ref,
                      a0_ref, a1_ref, a2_ref, r_ref, k_ref, v_ref, g_ref, lw_ref, as_ref, *, tiles_per_seq):
    i = pl.program_id(0)
    tm = x_ref.shape[0]
    ln = ln_ref[...]
    h = _rms(x_ref[...], ln)
    pos = i % tiles_per_seq
    prev_row = jnp.where(pos == 0, 0.0, _rms(xp_ref[7:8, :], ln))
    next_row = jnp.where(pos == tiles_per_seq - 1, 0.0, _rms(xn_ref[0:1, :], ln))
    row = lax.broadcasted_iota(I32, (tm, 1), 0)
    d_f = jnp.where(row == 0, prev_row, pltpu.roll(h, 1, 0)) - h
    d_b = jnp.where(row == tm - 1, next_row, pltpu.roll(h, tm - 1, 0)) - h
    d_c = 0.5 * (d_f + d_b)
    mu = mu_ref[...]
    mix = lambda dd, n: h + dd * mu[n:n + 1]
    r_ref[...] = _bdot(mix(d_c, 0), wrkv_ref[0]).astype(r_ref.dtype)
    k_ref[...] = _bdot(mix(d_c, 1), wrkv_ref[1]).astype(k_ref.dtype)
    v_ref[...] = _bdot(mix(d_c, 2), wrkv_ref[2]).astype(v_ref.dtype)
    g_ref[...] = _bdot(jax.nn.sigmoid(_bdot(mix(d_c, 3), g1_ref[...])), g2_ref[...]).astype(g_ref.dtype)
    for dr, dd in enumerate((d_f, d_b)):
        z = w0_ref[dr:dr + 1, :] + _bdot(jnp.tanh(_bdot(mix(dd, 4 + dr), w1_ref[dr])), w2_ref[dr])
        lw_ref[dr] = -B_DECAY_SCALE * jax.nn.sigmoid(z)
        aa = a0_ref[dr:dr + 1, :] + _bdot(_bdot(mix(dd, 6 + dr), a1_ref[dr]), a2_ref[dr])
        as_ref[dr] = jax.nn.sigmoid(aa).astype(as_ref.dtype)


def _wkv_kernel(rf_ref, kf_ref, vf_ref, rb_ref, kb_ref, vb_ref, lwf_ref, lwb_ref, asf_ref, asb_ref, kk_ref, ka_ref,
                yf_ref, yb_ref, st_ref, *, hb):
    n = B_HEAD_DIM
    c = rf_ref.shape[0]

    @pl.when(pl.program_id(1) == 0)
    def _():
        st_ref[...] = jnp.zeros_like(st_ref)

    nt = (((1,), (1,)), ((), ()))
    tn = (((0,), (0,)), ((), ()))
    mm = (((1,), (0,)), ((), ()))
    low = lax.broadcasted_iota(I32, (1, LANES), 1) < n
    pk = WKV_PACK
    gw = pk * n
    per_dir = hb // pk
    gsl = [slice(g * gw, (g + 1) * gw) for g in range(per_dir)]
    cut = lambda a: [a[:, sl] for sl in gsl]
    row_p = lax.broadcasted_iota(I32, (c, gw), 0)
    col_p = lax.broadcasted_iota(I32, (c, gw), 1) % n

    def operands(sgn, r_ref, k_ref, v_ref, lw_ref, as_ref):
        ahead = (lax.broadcasted_iota(I32, (c, c), 0) - lax.broadcasted_iota(I32, (c, c), 1)) * sgn
        lw = lw_ref[...]
        cl = jnp.dot((ahead >= 0).astype(F32), lw, preferred_element_type=F32, precision=lax.Precision.HIGHEST)
        total = jnp.sum(lw, axis=0, keepdims=True)
        e_in, e_ex, e_ng, e_end = jnp.exp(cl), jnp.exp(cl - lw), jnp.exp(-cl), jnp.exp(total - cl)
        rr, kk_raw, vv, asig = (ref[...].astype(F32) for ref in (r_ref, k_ref, v_ref, as_ref))
        kk_all = kk_raw * kk_ref[...]
        kdir = kk_raw * (1.0 + (asig - 1.0) * ka_ref[...])
        sq = kk_all * kk_all
        norms = []
        for p in range(sq.shape[1] // LANES):
            slab = sq[:, p * LANES:(p + 1) * LANES]
            s_lo = jnp.sum(jnp.where(low, slab, 0.0), axis=-1, keepdims=True)
            s_hi = jnp.sum(jnp.where(low, 0.0, slab), axis=-1, keepdims=True)
            norms.append(jnp.where(low, s_lo, s_hi))
        kkn_all = kk_all * lax.rsqrt(jnp.maximum(jnp.concatenate(norms, axis=-1), 1e-24))
        beta_all = kkn_all * asig
        ahead_p = (row_p - col_p) * sgn
        return dict(
            a_t=cut((-kkn_all * e_ex).astype(BF16)), b_t=cut((beta_all * e_ng).astype(BF16)),
            rt=cut((rr * e_in).astype(BF16)), kt=cut((kdir * e_ng).astype(BF16)), v=cut(vv.astype(BF16)),
            b_h=cut((beta_all * e_end).astype(BF16)), k_hb=cut((kdir * e_end).astype(BF16)),
            p_end=cut(jnp.exp(total)), strict=[ahead_p > 0] * per_dir, incl=[ahead_p >= 0] * per_dir)

    fwd = operands(1, rf_ref, kf_ref, vf_ref, lwf_ref, asf_ref)
    bwd = operands(-1, rb_ref, kb_ref, vb_ref, lwb_ref, asb_ref)
    both = {key: fwd[key] + bwd[key] for key in fwd}
    a_t, b_t, rt, kt, v, b_h, k_hb = (both[key] for key in ("a_t", "b_t", "rt", "kt", "v", "b_h", "k_hb"))
    p_end, strict_p, incl_p = both["p_end"], both["strict"], both["incl"]
    groups = range(2 * per_dir)

    diag_block = ((lax.broadcasted_iota(I32, (gw, gw), 0) // n)
                  == (lax.broadcasted_iota(I32, (gw, gw), 1) // n))
    diag_bf = diag_block.astype(BF16)

    def bd(p):
        return jnp.concatenate([p] * pk, axis=0) * diag_bf

    def fold(full):
        kept = jnp.where(diag_block, full, 0.0)
        out = kept[:n]
        for h in range(1, pk):
            out = out + kept[h * n:(h + 1) * n]
        return out

    eye_p = (row_p == col_p).astype(F32)
    same_block = lambda m: (row_p // m) == (col_p // m)
    dots = lambda a, b, dims: [lax.dot_general(p, q, dims, preferred_element_type=F32) for p, q in zip(a, b)]
    rows = lambda *parts: [jnp.concatenate(list(ps), axis=0) for ps in zip(*parts)]
    bf = lambda xs: [t.astype(BF16) for t in xs]
    bds = lambda xs: [bd(t) for t in xs]
    gram = dots(rows(a_t, rt), rows(bds(b_t), bds(kt)), nt)
    l_ab = [jnp.where(s, g[:c, :gw], 0.0) for s, g in zip(strict_p, gram)]
    l_ak = [jnp.where(s, g[:c, gw:], 0.0).astype(BF16) for s, g in zip(strict_p, gram)]
    m_r = [jnp.concatenate([jnp.where(i, g[c:, :gw], 0.0), jnp.where(i, g[c:, gw:], 0.0)],
                           axis=1).astype(BF16) for i, g in zip(incl_p, gram)]
    in_pair = same_block(2)
    x = [eye_p + jnp.where(in_pair, l, 0.0) for l in l_ab]
    m = 2
    while m < c:
        level = same_block(2 * m) & jnp.logical_not(same_block(m))
        xb = bf(x)
        l_m = [jnp.where(level, l, 0.0).astype(BF16) for l in l_ab]
        xl = bf(dots(xb, bds(l_m), mm))
        x = [xx + t for xx, t in zip(x, dots(xl, bds(xb), mm))]
        m *= 2
    s0 = [st_ref[g] for g in groups]
    bd_s = bds(bf(s0))
    bd_v = bds(v)
    w = [p + q for p, q in zip(dots(a_t, bd_s, nt), dots(l_ak, bd_v, mm))]
    u = bf(dots(bf(x), bds(bf(w)), mm))
    y = [p + q for p, q in zip(dots(rt, bd_s, nt), dots(m_r, rows(bds(u), bd_v), mm))]
    s_new = dots(rows(u, v), rows(b_h, k_hb), tn)
    for g in groups:
        st_ref[g] = s0[g] * p_end[g] + fold(s_new[g])
        out_ref = yf_ref if g < per_dir else yb_ref
        out_ref[:, gsl[g % per_dir]] = y[g]


def _rwkv_out_kernel(x_ref, yf_ref, yb_ref, r_ref, k_ref, v_ref, g_ref, as_ref, ka_ref, rk_ref, lnx_ref, wo_ref,
                     sum_ref, o_ref):
    d = x_ref.shape[1]
    n = B_HEAD_DIM
    y = yf_ref[...] + yb_ref[...]
    a_sum = as_ref[0].astype(F32) + as_ref[1].astype(F32)
    kd = k_ref[...].astype(F32) * (2.0 + (a_sum - 2.0) * ka_ref[...])
    prod = r_ref[...].astype(F32) * kd * rk_ref[...]
    slab = sum_ref.shape[0]

    def head_sum(a):
        return jnp.dot(a.astype(BF16), sum_ref[...], preferred_element_type=F32)

    parts = []
    for p in range(d // slab):
        sl = slice(p * slab, (p + 1) * slab)
        ys = y[:, sl]
        cen = ys - head_sum(ys) * (1.0 / n)
        var = head_sum(cen * cen) * (1.0 / n)
        yn = cen * lax.rsqrt(var + B_GN_EPS) * lnx_ref[0:1, sl] + lnx_ref[1:2, sl]
        bonus = head_sum(prod[:, sl]) * v_ref[:, sl]
        parts.append(((yn + bonus) * g_ref[:, sl]).astype(BF16))
    o_ref[...] = x_ref[...] + jnp.dot(jnp.concatenate(parts, axis=-1), wo_ref[...], preferred_element_type=F32)


def mixer_b(x, ln, mu, w_rkv, w0, w1, w2, a0, a1, a2, g1, g2, k_k, k_a, r_k, ln_x, wo, batch, s_len):
    t, d = x.shape
    bf = lambda a: a.astype(BF16)
    halo = 8
    full = lambda a: pl.BlockSpec(a.shape, lambda i: (0,) * a.ndim)
    resident = lambda a: pl.BlockSpec(a.shape, lambda i: (0,) * a.ndim, pipeline_mode=pl.Buffered(1))
    ln2 = ln.reshape(1, d)
    weights = (ln2, mu, bf(w_rkv), bf(g1), bf(g2), w0, bf(w1), bf(w2), a0, bf(a1), bf(a2))
    tok = lambda dt: jax.ShapeDtypeStruct((t, d), dt)
    tok2 = lambda dt: jax.ShapeDtypeStruct((2, t, d), dt)
    tm = _row_tile(s_len, 512)
    tiles_per_seq = s_len // tm
    row = lambda n: pl.BlockSpec((tm, n), lambda i: (i, 0))
    row2 = lambda n: pl.BlockSpec((2, tm, n), lambda i: (0, i, 0))
    r, k, v, g, lw, asig = pl.pallas_call(
        functools.partial(_rwkv_prep_kernel, tiles_per_seq=tiles_per_seq),
        grid=(t // tm,),
        in_specs=[row(d),
                  pl.BlockSpec((halo, d), lambda i: (jnp.maximum(i * (tm // halo) - 1, 0), 0)),
                  pl.BlockSpec((halo, d), lambda i: (jnp.minimum((i + 1) * (tm // halo), t // halo - 1), 0))]
                 + [resident(a) for a in weights],
        out_specs=[row(d), row(d), row(d), row(d), row2(d), row2(d)],
        out_shape=[tok(BF16), tok(BF16), tok(BF16), tok(BF16), tok2(F32), tok2(BF16)],
        compiler_params=_params("parallel"),
        name="rwkv_prep",
    )(x, x, x, *weights)
    tm = _row_tile(s_len, 256)
    row = lambda n: pl.BlockSpec((tm, n), lambda i: (i, 0))
    row2 = lambda n: pl.BlockSpec((2, tm, n), lambda i: (0, i, 0))

    hb = WKV_HEADS_PER_STEP
    width = hb * B_HEAD_DIM
    c = min(WKV_CHUNK, s_len)
    nc = s_len // c
    assert width == d
    tok_f = pl.BlockSpec((None, c, d), lambda b, ci: (b, ci, 0))
    tok_b = pl.BlockSpec((None, c, d), lambda b, ci: (b, nc - 1 - ci, 0))
    dir_f = pl.BlockSpec((None, None, c, d), lambda b, ci: (0, b, ci, 0))
    dir_b = pl.BlockSpec((None, None, c, d), lambda b, ci: (1, b, nc - 1 - ci, 0))
    par = pl.BlockSpec((1, d), lambda b, ci: (0, 0))
    v3 = lambda a: a.reshape(batch, s_len, d)
    v4 = lambda a: a.reshape(2, batch, s_len, d)
    seq_out = jax.ShapeDtypeStruct((batch, s_len, d), F32)
    y_f, y_b = pl.pallas_call(
        functools.partial(_wkv_kernel, hb=hb),
        grid=(batch, nc),
        in_specs=[tok_f, tok_f, tok_f, tok_b, tok_b, tok_b, dir_f, dir_b, dir_f, dir_b, par, par],
        out_specs=[tok_f, tok_b],
        out_shape=[seq_out, seq_out],
        scratch_shapes=[pltpu.VMEM((2 * hb // WKV_PACK, B_HEAD_DIM, WKV_PACK * B_HEAD_DIM), F32)],
        compiler_params=_params("parallel", "arbitrary"),
        name="wkv_scan",
    )(v3(r), v3(k), v3(v), v3(r), v3(k), v3(v), v4(lw), v4(lw), v4(asig), v4(asig),
      k_k.reshape(1, d), k_a.reshape(1, d))

    lane = jnp.arange(WKV_PACK * B_HEAD_DIM)
    same_head = ((lane[:, None] // B_HEAD_DIM) == (lane[None, :] // B_HEAD_DIM)).astype(BF16)
    params = (k_a.reshape(1, d), r_k.reshape(1, d), ln_x, bf(wo), same_head)
    return pl.pallas_call(
        _rwkv_out_kernel,
        grid=(t // tm,),
        in_specs=[row(d), row(d), row(d), row(d), row(d), row(d), row(d), row2(d)] + [full(a) for a in params],
        out_specs=row(d),
        out_shape=tok(F32),
        compiler_params=_params("parallel"),
        name="rwkv_out",
    )(x, y_f.reshape(t, d), y_b.reshape(t, d), r, k, v, g, asig, *params)


TOP_K = 2
ROUTE_EXPERT, ROUTE_GATE, ROUTE_RANK = 0, TOP_K, 2 * TOP_K
MOE_ROW_TILE = 512
ROW_DMA_UNROLL = 8


def _router_kernel(x_ref, g_ref, wr_ref, hn_ref, route_ref, cnt_ref, run_ref, *, n_exp):
    @pl.when(pl.program_id(0) == 0)
    def _():
        run_ref[...] = jnp.zeros_like(run_ref)

    tm = x_ref.shape[0]
    h = _rms(x_ref[...], g_ref[...])
    hn_ref[...] = h
    logits = jnp.dot(h, wr_ref[...], preferred_element_type=F32, precision=lax.Precision.HIGHEST)
    lane = lax.broadcasted_iota(I32, (tm, LANES), 1)
    logits = jnp.where(lane < n_exp, logits, NEG_INF)

    def top1(v):
        m = jnp.max(v, axis=-1, keepdims=True)
        return m, jnp.min(jnp.where(v == m, lane, LANES), axis=-1, keepdims=True)

    m1, i1 = top1(logits)
    m2, i2 = top1(jnp.where(lane == i1, NEG_INF, logits))
    e = jnp.exp(m2 - m1)
    g1 = 1.0 / (1.0 + e)
    g2 = e * g1
    sel = ((lane == i1) | (lane == i2)).astype(F32)
    earlier = (lax.broadcasted_iota(I32, (tm, tm), 0) > lax.broadcasted_iota(I32, (tm, tm), 1)).astype(BF16)
    before = jnp.dot(earlier, sel.astype(BF16), preferred_element_type=F32) + run_ref[...]
    r1 = jnp.sum(jnp.where(lane == i1, before, 0.0), axis=-1, keepdims=True)
    r2 = jnp.sum(jnp.where(lane == i2, before, 0.0), axis=-1, keepdims=True)
    run_ref[...] += jnp.sum(sel, axis=0, keepdims=True)
    cnt_ref[...] = run_ref[...]
    route = jnp.zeros((tm, LANES), F32)
    for off, vals in ((ROUTE_EXPERT, (i1.astype(F32), i2.astype(F32))), (ROUTE_GATE, (g1, g2)), (ROUTE_RANK, (r1, r2))):
        for kk, v in enumerate(vals):
            route = jnp.where(lane == off + kk, v, route)
    route_ref[...] = route


def _dispatch_kernel(dest_ref, hn_ref, xs_in_ref, xs_ref, sem):
    del xs_in_ref
    tm = hn_ref.shape[0]

    def row_copy(r, slot):
        return pltpu.make_async_copy(hn_ref.at[pl.ds(r, 1)], xs_ref.at[pl.ds(dest_ref[TOP_K * r + slot], 1)], sem)

    def start(r, c):
        for slot in range(TOP_K):
            row_copy(r, slot).start()
        return c

    def wait(r, c):
        for slot in range(TOP_K):
            row_copy(r, slot).wait()
        return c

    lax.fori_loop(0, tm, start, 0, unroll=ROW_DMA_UNROLL)
    lax.fori_loop(0, tm, wait, 0, unroll=ROW_DMA_UNROLL)


def _expert_ffn_kernel(te_ref, x_ref, w1_ref, w3_ref, w2_ref, o_ref, xb_ref, acc_ref):
    del te_ref
    j = pl.program_id(1)

    @pl.when(j == 0)
    def _():
        xb_ref[...] = x_ref[...].astype(BF16)
        acc_ref[...] = jnp.zeros_like(acc_ref)

    acc_ref[...] += _swiglu_chunks(xb_ref[...], w1_ref, w3_ref, w2_ref)

    @pl.when(j == pl.num_programs(1) - 1)
    def _():
        o_ref[...] = acc_ref[...]


def _moe_combine_kernel(dest_ref, route_ref, x_ref, ys_ref, o_ref, buf_ref, sem):
    tm = x_ref.shape[0]

    def row_copy(r, slot):
        return pltpu.make_async_copy(ys_ref.at[pl.ds(dest_ref[TOP_K * r + slot], 1)], buf_ref.at[slot, pl.ds(r, 1)], sem)

    def start(r, c):
        for slot in range(TOP_K):
            row_copy(r, slot).start()
        return c

    def wait(r, c):
        for slot in range(TOP_K):
            row_copy(r, slot).wait()
        return c

    lax.fori_loop(0, tm, start, 0, unroll=ROW_DMA_UNROLL)
    lax.fori_loop(0, tm, wait, 0, unroll=ROW_DMA_UNROLL)
    route = route_ref[...]
    out = x_ref[...]
    for slot in range(TOP_K):
        out = out + route[:, ROUTE_GATE + slot:ROUTE_GATE + slot + 1] * buf_ref[slot]
    o_ref[...] = out


def moe_ffn(x, ln, router, w13, w2, tf=1792):
    t, d = x.shape
    n_exp, dff = w2.shape[0], w2.shape[1]
    assert dff % tf == 0
    nj = dff // tf
    tmx = min(MOE_ROW_TILE, t)
    tm = _row_tile(t, 512)
    wr = jnp.pad(router, ((0, 0), (0, LANES - n_exp)))
    hn, route, cnt = pl.pallas_call(
        functools.partial(_router_kernel, n_exp=n_exp),
        grid=(t // tm,),
        in_specs=[pl.BlockSpec((tm, d), lambda i: (i, 0)),
                  pl.BlockSpec((1, d), lambda i: (0, 0)),
                  pl.BlockSpec((d, LANES), lambda i: (0, 0))],
        out_specs=[pl.BlockSpec((tm, d), lambda i: (i, 0)),
                   pl.BlockSpec((tm, LANES), lambda i: (i, 0)),
                   pl.BlockSpec((1, LANES), lambda i: (0, 0))],
        out_shape=[jax.ShapeDtypeStruct((t, d), F32),
                   jax.ShapeDtypeStruct((t, LANES), F32),
                   jax.ShapeDtypeStruct((1, LANES), F32)],
        scratch_shapes=[pltpu.VMEM((1, LANES), F32)],
        compiler_params=_params("arbitrary"),
        name="moe_router",
    )(x, ln.reshape(1, d), wr)

    expert = route[:, ROUTE_EXPERT:ROUTE_EXPERT + TOP_K].astype(I32)
    rank = route[:, ROUTE_RANK:ROUTE_RANK + TOP_K].astype(I32)
    counts = cnt[0, :n_exp].astype(I32)
    padded = (counts + tmx - 1) // tmx * tmx
    ends = jnp.cumsum(padded)
    dest = ((ends - padded)[expert] + rank).reshape(TOP_K * t)
    n_tiles = (TOP_K * t) // tmx + n_exp
    rows = n_tiles * tmx
    tile_start = jnp.arange(n_tiles, dtype=I32) * tmx
    tile_expert = jnp.minimum(jnp.sum((tile_start[:, None] >= ends[None, :]).astype(I32), axis=1), n_exp - 1)

    smem_rows = lambda n: pl.BlockSpec((TOP_K * n,), lambda i: (i,), memory_space=pltpu.SMEM)
    xs = pl.pallas_call(
        _dispatch_kernel,
        grid=(t // tm,),
        in_specs=[smem_rows(tm),
                  pl.BlockSpec((tm, d), lambda i: (i, 0)),
                  pl.BlockSpec(memory_space=pl.ANY)],
        out_specs=pl.BlockSpec(memory_space=pl.ANY),
        out_shape=jax.ShapeDtypeStruct((rows, d), F32),
        scratch_shapes=[pltpu.SemaphoreType.DMA(())],
        input_output_aliases={2: 0},
        compiler_params=_params("arbitrary"),
        name="moe_dispatch",
    )(dest, hn, jnp.zeros((rows, d), F32))

    ys = pl.pallas_call(
        _expert_ffn_kernel,
        grid_spec=pltpu.PrefetchScalarGridSpec(
            num_scalar_prefetch=1,
            grid=(n_tiles, nj),
            in_specs=[pl.BlockSpec((tmx, d), lambda i, j, te: (i, 0)),
                      pl.BlockSpec((None, d, tf), lambda i, j, te: (te[i], 0, j)),
                      pl.BlockSpec((None, d, tf), lambda i, j, te: (te[i], 0, nj + j)),
                      pl.BlockSpec((None, tf, d), lambda i, j, te: (te[i], j, 0))],
            out_specs=pl.BlockSpec((tmx, d), lambda i, j, te: (i, 0)),
            scratch_shapes=[pltpu.VMEM((tmx, d), BF16), pltpu.VMEM((tmx, d), F32)]),
        out_shape=jax.ShapeDtypeStruct((rows, d), F32),
        compiler_params=_params("parallel", "arbitrary"),
        name="moe_expert_ffn",
    )(tile_expert, xs, w13, w13, w2)

    return pl.pallas_call(
        _moe_combine_kernel,
        grid=(t // tm,),
        in_specs=[smem_rows(tm),
                  pl.BlockSpec((tm, LANES), lambda i: (i, 0)),
                  pl.BlockSpec((tm, d), lambda i: (i, 0)),
                  pl.BlockSpec(memory_space=pl.ANY)],
        out_specs=pl.BlockSpec((tm, d), lambda i: (i, 0)),
        out_shape=jax.ShapeDtypeStruct((t, d), F32),
        scratch_shapes=[pltpu.VMEM((TOP_K, tm, d), F32), pltpu.SemaphoreType.DMA(())],
        compiler_params=_params("arbitrary"),
        name="moe_combine",
    )(dest, route, x, ys)


def kernel(x, positions, ln_mix, ln_ffn, a_wqkv, a_qk_norm, a_wo, b_mu, b_wrkv, b_w0, b_w1, b_w2, b_a0, b_a1, b_a2, b_g1, b_g2, b_kk, b_ka, b_rk, b_lnx, b_wo, c_wa, c_qa_norm, c_kva_norm, c_wq_b, c_wkv_b, c_qk_norm, c_wo, f_w13, f_w2, m_router, m_w13, m_w2):
    batch, s_len, d = x.shape
    t = batch * s_len
    cs, sn = _rope_tables(positions)
    cs, sn = cs.reshape(t, LANES), sn.reshape(t, LANES)
    h = x.reshape(t, d)
    n_mixers = 3
    for i in range(ln_mix.shape[0]):
        j, kind = divmod(i, n_mixers)
        if kind == 0:
            h = mixer_a(h, ln_mix[i], a_wqkv[j], a_qk_norm[j], a_wo[j], cs, sn, batch, s_len)
        elif kind == 1:
            h = mixer_b(h, ln_mix[i], b_mu[j], b_wrkv[j], b_w0[j], b_w1[j], b_w2[j], b_a0[j], b_a1[j], b_a2[j],
                        b_g1[j], b_g2[j], b_kk[j], b_ka[j], b_rk[j], b_lnx[j], b_wo[j], batch, s_len)
        else:
            h = mixer_c(h, ln_mix[i], c_wa[j], c_qa_norm[j], c_kva_norm[j], c_wq_b[j], c_wkv_b[j], c_qk_norm[j],
                        c_wo[j], cs, sn, batch, s_len)
        f = i // 2
        if i % 2 == 0:
            h = dense_ffn(h, ln_ffn[i], f_w13[f].astype(BF16), f_w2[f].astype(BF16))
        else:
            h = moe_ffn(h, ln_ffn[i], m_router[f], m_w13[f].astype(BF16), m_w2[f].astype(BF16))
    return h.reshape(batch, s_len, d)
```

```python
import functools
import math

import jax
import jax.numpy as jnp
from jax import lax
from jax.experimental import pallas as pl
from jax.experimental.pallas import tpu as pltpu

F32 = jnp.float32
BF16 = jnp.bfloat16
I32 = jnp.int32

NORM_EPS = 1e-6
ROPE_THETA = 500000.0
NEG_INF = -1e30
LANES = 128
ROT_DIM = 32
ROT_HALF = ROT_DIM // 2
DILATED_PAIRS = ((128, 1), (512, 4), (2048, 16))
A_HEAD_DIM = 128
VMEM_LIMIT = 48 * 1024 * 1024


def _params(*sem):
    return pltpu.CompilerParams(dimension_semantics=sem, vmem_limit_bytes=VMEM_LIMIT)


def _rms(x, g):
    return x * lax.rsqrt(jnp.mean(x * x, axis=-1, keepdims=True) + NORM_EPS) * g


def _row_tile(t, want):
    tm = min(t, want)
    assert t % tm == 0
    return tm


def _norm_matmul_kernel(x_ref, g_ref, w_ref, o_ref, xn_ref):
    @pl.when(pl.program_id(1) == 0)
    def _():
        xn_ref[...] = _rms(x_ref[...], g_ref[...]).astype(BF16)

    o_ref[...] = jnp.dot(xn_ref[...], w_ref[...], preferred_element_type=F32).astype(o_ref.dtype)


def norm_matmul(x, g, w, out_dtype, tn):
    t, k = x.shape
    n = w.shape[1]
    tm = _row_tile(t, 1024)
    assert n % tn == 0
    return pl.pallas_call(
        _norm_matmul_kernel,
        grid=(t // tm, n // tn),
        in_specs=[pl.BlockSpec((tm, k), lambda i, j: (i, 0)),
                  pl.BlockSpec((1, k), lambda i, j: (0, 0)),
                  pl.BlockSpec((k, tn), lambda i, j: (0, j))],
        out_specs=pl.BlockSpec((tm, tn), lambda i, j: (i, j)),
        out_shape=jax.ShapeDtypeStruct((t, n), out_dtype),
        scratch_shapes=[pltpu.VMEM((tm, k), BF16)],
        compiler_params=_params("parallel", "arbitrary"),
        name="norm_matmul",
    )(x, g.reshape(1, k), w)


def _matmul_res_kernel(a_ref, w_ref, x_ref, o_ref):
    o_ref[...] = x_ref[...] + jnp.dot(a_ref[...], w_ref[...], preferred_element_type=F32)


def matmul_residual(a, w, x):
    t, k = a.shape
    n = w.shape[1]
    tm = _row_tile(t, 1024)
    return pl.pallas_call(
        _matmul_res_kernel,
        grid=(t // tm,),
        in_specs=[pl.BlockSpec((tm, k), lambda i: (i, 0)),
                  pl.BlockSpec((k, n), lambda i: (0, 0)),
                  pl.BlockSpec((tm, n), lambda i: (i, 0))],
        out_specs=pl.BlockSpec((tm, n), lambda i: (i, 0)),
        out_shape=jax.ShapeDtypeStruct((t, n), F32),
        compiler_params=_params("parallel"),
        name="matmul_residual",
    )(a, w, x)


def _rope_tables(positions):
    b, s = positions.shape
    inv_freq = ROPE_THETA ** (-jnp.arange(0, ROT_DIM, 2, dtype=F32) / ROT_DIM)
    ang = positions.astype(F32)[..., None] * inv_freq
    c, sn = jnp.cos(ang), jnp.sin(ang)
    rest = (b, s, LANES - ROT_DIM)
    cs_t = jnp.concatenate([c, c, jnp.ones(rest, F32)], axis=-1)
    sn_t = jnp.concatenate([-sn, sn, jnp.zeros(rest, F32)], axis=-1)
    return cs_t, sn_t


def _rope(x, cs, sn, lane):
    n = x.shape[-1]
    partner = jnp.where(lane < ROT_HALF, pltpu.roll(x, n - ROT_HALF, 1), pltpu.roll(x, ROT_HALF, 1))
    return x * cs + partner * sn


def _dil_attn_kernel(q_ref, k_ref, v_ref, o_ref, lse_ref, *, seq, qt, kw, half, heads):
    hd = A_HEAD_DIM
    qi = pl.program_id(2)
    q0 = pl.multiple_of(qi * qt, qt)
    start = jnp.clip(q0 - half, 0, seq - kw)
    start = pl.multiple_of(start, math.gcd(half, qt))
    qpos = q0 + lax.broadcasted_iota(I32, (qt, kw), 0)
    kpos = start + lax.broadcasted_iota(I32, (qt, kw), 1)
    valid = jnp.abs(qpos - kpos) <= half
    head_of_lane = lax.broadcasted_iota(I32, (1, LANES), 1) // (LANES // heads)
    ones = jnp.ones((kw, hd), BF16)
    work = [(r, slice(h * hd, (h + 1) * hd)) for r in range(q_ref.shape[0]) for h in range(heads)]
    s = [lax.dot_general(q_ref[r, :, sl], k_ref[r, pl.ds(start, kw), sl], (((1,), (1,)), ((), ())),
                         preferred_element_type=F32) for r, sl in work]
    s = [jnp.where(valid, t, NEG_INF) for t in s]
    m = [jnp.max(t, axis=-1, keepdims=True) for t in s]
    p = [jnp.exp(t - mm).astype(BF16) for t, mm in zip(s, m)]
    ol = [jnp.dot(pp, jnp.concatenate([v_ref[r, pl.ds(start, kw), sl], ones], axis=1), preferred_element_type=F32)
          for pp, (r, sl) in zip(p, work)]
    for r in range(q_ref.shape[0]):
        lse_row = jnp.zeros((qt, LANES), F32)
        for h in range(heads):
            i = r * heads + h
            o_ref[r, :, work[i][1]] = (ol[i][:, :hd] / ol[i][:, hd:]).astype(o_ref.dtype)
            lse_row = jnp.where(head_of_lane == h, m[i] + jnp.log(ol[i][:, hd:hd + 1]), lse_row)
        lse_ref[r] = lse_row


DIL_ATTN_ROWS = 512
A_PAIR = 2 * A_HEAD_DIM


def _head_helper_matrices():
    i = jnp.arange(A_PAIR)
    same_head = (i[:, None] // A_HEAD_DIM) == (i[None, :] // A_HEAD_DIM)
    dst = i % A_HEAD_DIM
    src = jnp.where(dst < ROT_HALF, i + ROT_HALF, i - ROT_HALF)
    perm = (i[:, None] == src[None, :]) & (dst[None, :] < ROT_DIM)
    return same_head.astype(BF16), perm.astype(BF16)


def _qkv_group_kernel(x_ref, g_ref, w_ref, cs_ref, sn_ref, gain_ref, mean_ref, perm_ref, o_ref, xn_ref, acc_ref, *, dil):
    j = pl.program_id(1)

    @pl.when(j == 0)
    def _():
        xn_ref[...] = _rms(x_ref[...], g_ref[...]).astype(BF16)

    acc = jnp.dot(xn_ref[...], w_ref[...], preferred_element_type=F32)
    tm, width = acc.shape

    def emit(vals):
        if dil == 1:
            o_ref[0, 0] = vals.astype(o_ref.dtype)
            return
        chunks = width // LANES
        for c in range(chunks):
            acc_ref[c] = vals[:, c * LANES:(c + 1) * LANES]
        for r in range(dil):
            o_ref[0, r] = jnp.concatenate([acc_ref[c, pl.ds(r, tm // dil, stride=dil), :] for c in range(chunks)],
                                          axis=-1).astype(o_ref.dtype)

    @pl.when(j < 2)
    def _():
        cs = jnp.concatenate([cs_ref[...]] * 2, axis=1)
        sn = jnp.concatenate([sn_ref[...]] * 2, axis=1)
        gain = gain_ref[pl.ds(j, 1), :]
        parts = []
        for p in range(width // A_PAIR):
            a = acc[:, p * A_PAIR:(p + 1) * A_PAIR]
            meansq = jnp.dot((a * a).astype(BF16), mean_ref[...], preferred_element_type=F32)
            ag = a * gain[:, p * A_PAIR:(p + 1) * A_PAIR]
            pair = jnp.dot(ag.astype(BF16), perm_ref[...], preferred_element_type=F32)
            parts.append((ag * cs + pair * sn) * lax.rsqrt(meansq + NORM_EPS))
        emit(jnp.concatenate(parts, axis=1))

    @pl.when(j == 2)
    def _():
        emit(acc)


def qkv_group_projection(x, ln, wqkv, cs, sn, gq, gk, group, batch, s_len, width):
    t, d = x.shape
    dil = DILATED_PAIRS[group][1]
    n_groups = len(DILATED_PAIRS)
    heads = width // A_HEAD_DIM
    tm = _row_tile(s_len, 1024)
    tiles = s_len // tm
    assert tm % (16 * dil) == 0
    gains = jnp.stack([jnp.tile(gq, heads) * A_HEAD_DIM ** -0.5, jnp.tile(gk, heads)])
    ones, perm = _head_helper_matrices()
    ones = (ones.astype(F32) * (1.0 / A_HEAD_DIM)).astype(BF16)
    const = lambda a: pl.BlockSpec(a.shape, lambda i, j: (0, 0))
    return pl.pallas_call(
        functools.partial(_qkv_group_kernel, dil=dil),
        grid=(t // tm, 3),
        in_specs=[pl.BlockSpec((tm, d), lambda i, j: (i, 0)),
                  pl.BlockSpec((1, d), lambda i, j: (0, 0)),
                  pl.BlockSpec((d, width), lambda i, j: (0, j * n_groups + group)),
                  pl.BlockSpec((tm, LANES), lambda i, j: (i, 0)),
                  pl.BlockSpec((tm, LANES), lambda i, j: (i, 0)),
                  const(gains), const(ones), const(perm)],
        out_specs=pl.BlockSpec((1, dil, tm // dil, width), lambda i, j: (i // tiles, 0, i % tiles, j)),
        out_shape=jax.ShapeDtypeStruct((batch, dil, s_len // dil, 3 * width), BF16),
        scratch_shapes=[pltpu.VMEM((tm, d), BF16), pltpu.VMEM((width // LANES, tm, LANES), F32)],
        compiler_params=_params("parallel", "arbitrary"),
        name=f"qkv_projection_g{group}",
    )(x, ln.reshape(1, d), wqkv, cs, sn, gains, ones, perm)


def dilated_group_attention(qkv, group, batch, s_len, heads):
    window, dil = DILATED_PAIRS[group]
    half = window // (2 * dil)
    seq = s_len // dil
    width = heads * A_HEAD_DIM
    qt = min(256, seq)
    kw = min(qt + 2 * half, seq)
    assert seq % qt == 0 and (seq - kw) % math.gcd(half, qt) == 0
    rb = max(1, min(dil, DIL_ATTN_ROWS // qt))
    assert dil % rb == 0
    whole = lambda c: pl.BlockSpec((None, rb, seq, width), lambda b, r, qi: (b, r, 0, c))
    tile = lambda n: pl.BlockSpec((None, rb, qt, n), lambda b, r, qi: (b, r, qi, 0))
    kern = functools.partial(_dil_attn_kernel, seq=seq, qt=qt, kw=kw, half=half, heads=heads)
    return pl.pallas_call(
        kern,
        grid=(batch, dil // rb, seq // qt),
        in_specs=[tile(width), whole(1), whole(2)],
        out_specs=[tile(width), tile(LANES)],
        out_shape=[jax.ShapeDtypeStruct((batch, dil, seq, width), BF16),
                   jax.ShapeDtypeStruct((batch, dil, seq, LANES), F32)],
        compiler_params=_params("parallel", "parallel", "parallel"),
        name=f"dilated_attention_g{group}",
    )(qkv, qkv, qkv)


def _combine_wo_kernel(x_ref, o0_ref, o1_ref, o2_ref, l0_ref, l1_ref, l2_ref, w_ref, out_ref, ot_ref, lt_ref, *, heads):
    hd = A_HEAD_DIM
    rep = LANES // heads
    tm = x_ref.shape[0]

    def token_order(src_ref, dst_ref, g):
        dil = src_ref.shape[0]
        if dil == 1:
            return src_ref[0].astype(F32)
        chunks = src_ref.shape[-1] // LANES
        for r in range(dil):
            blk = src_ref[r].astype(F32)
            for c in range(chunks):
                dst_ref[g, c, pl.ds(r, tm // dil, stride=dil), :] = blk[:, c * LANES:(c + 1) * LANES]
        return jnp.concatenate([dst_ref[g, c] for c in range(chunks)], axis=-1)

    ls = [token_order(ref, lt_ref, g) for g, ref in enumerate((l0_ref, l1_ref, l2_ref))]
    os_ = [token_order(ref, ot_ref, g) for g, ref in enumerate((o0_ref, o1_ref, o2_ref))]
    m = jnp.maximum(jnp.maximum(ls[0], ls[1]), ls[2])
    es = [jnp.exp(l - m) for l in ls]
    inv = 1.0 / (es[0] + es[1] + es[2])
    al = [e * inv for e in es]
    parts = []
    for h in range(heads):
        sl = slice(h * hd, (h + 1) * hd)
        c = slice(h * rep, h * rep + 1)
        parts.append((al[0][:, c] * os_[0][:, sl] + al[1][:, c] * os_[1][:, sl] + al[2][:, c] * os_[2][:, sl]).astype(BF16))
    o = jnp.concatenate(parts, axis=-1)
    out_ref[...] = x_ref[...] + jnp.dot(o, w_ref[...], preferred_element_type=F32)


def combine_wo(x, outs, lses, wo, heads, batch, s_len):
    t, d = x.shape
    width = wo.shape[0]
    n_groups = len(DILATED_PAIRS)
    tm = _row_tile(s_len, 512)
    tiles = s_len // tm
    assert all(tm % (16 * dil) == 0 for _, dil in DILATED_PAIRS)
    row = lambda n: pl.BlockSpec((tm, n), lambda i: (i, 0))
    res = lambda g, n: pl.BlockSpec((None, DILATED_PAIRS[g][1], tm // DILATED_PAIRS[g][1], n),
                                    lambda i: (i // tiles, 0, i % tiles, 0))
    return pl.pallas_call(
        functools.partial(_combine_wo_kernel, heads=heads),
        grid=(t // tm,),
        in_specs=[row(d)] + [res(g, width) for g in range(n_groups)] + [res(g, LANES) for g in range(n_groups)]
                 + [pl.BlockSpec((width, d), lambda i: (0, 0))],
        out_specs=row(d),
        out_shape=jax.ShapeDtypeStruct((t, d), F32),
        scratch_shapes=[pltpu.VMEM((n_groups, width // LANES, tm, LANES), F32),
                        pltpu.VMEM((n_groups, 1, tm, LANES), F32)],
        compiler_params=_params("parallel"),
        name="combine_wo",
    )(x, *outs, *lses, wo)


def mixer_a(x, ln, wqkv, qk_norm, wo, cs, sn, batch, s_len):
    heads = wo.shape[0] // A_HEAD_DIM
    wqkv = wqkv.astype(BF16)
    outs, lses = [], []
    for g in range(len(DILATED_PAIRS)):
        qkv = qkv_group_projection(x, ln, wqkv, cs, sn, qk_norm[0, g], qk_norm[1, g], g, batch, s_len, wo.shape[0])
        o, lse = dilated_group_attention(qkv, g, batch, s_len, heads)
        outs.append(o)
        lses.append(lse)
    return combine_wo(x, outs, lses, wo.astype(BF16), heads, batch, s_len)


def _ffn_kernel(x_ref, g_ref, w1_ref, w3_ref, w2_ref, o_ref, xn_ref, acc_ref):
    j = pl.program_id(1)

    @pl.when(j == 0)
    def _():
        xn_ref[...] = _rms(x_ref[...], g_ref[...]).astype(BF16)
        acc_ref[...] = jnp.zeros_like(acc_ref)

    acc_ref[...] += _swiglu_chunks(xn_ref[...], w1_ref, w3_ref, w2_ref)

    @pl.when(j == pl.num_programs(1) - 1)
    def _():
        o_ref[...] = x_ref[...] + acc_ref[...]


FFN_SUB = 256


def _swiglu_chunks(xn, w1_ref, w3_ref, w2_ref):
    tf = w2_ref.shape[0]
    assert tf % FFN_SUB == 0
    total = None
    for s in range(tf // FFN_SUB):
        sl = slice(s * FFN_SUB, (s + 1) * FFN_SUB)
        gate = jnp.dot(xn, w1_ref[:, sl], preferred_element_type=F32)
        up = jnp.dot(xn, w3_ref[:, sl], preferred_element_type=F32)
        act = (gate * jax.nn.sigmoid(gate) * up).astype(BF16)
        part = jnp.dot(act, w2_ref[sl, :], preferred_element_type=F32)
        total = part if total is None else total + part
    return total


def dense_ffn(x, ln, w13, w2):
    t, d = x.shape
    tf = dff = w2.shape[0]
    nj = dff // tf
    tm = _row_tile(t, 512)
    resident = dict(pipeline_mode=pl.Buffered(1))
    return pl.pallas_call(
        _ffn_kernel,
        grid=(t // tm, nj),
        in_specs=[pl.BlockSpec((tm, d), lambda i, j: (i, 0)),
                  pl.BlockSpec((1, d), lambda i, j: (0, 0)),
                  pl.BlockSpec((d, tf), lambda i, j: (0, j), **resident),
                  pl.BlockSpec((d, tf), lambda i, j: (0, nj + j), **resident),
                  pl.BlockSpec((tf, d), lambda i, j: (j, 0), **resident)],
        out_specs=pl.BlockSpec((tm, d), lambda i, j: (i, 0)),
        out_shape=jax.ShapeDtypeStruct((t, d), F32),
        scratch_shapes=[pltpu.VMEM((tm, d), BF16), pltpu.VMEM((tm, d), F32)],
        compiler_params=_params("parallel", "arbitrary"),
        name="dense_ffn",
    )(x, ln.reshape(1, d), w13, w13, w2)


C_ROPE = ROT_DIM
C_NOPE = 64
C_V = 64
C_QK = C_ROPE + C_NOPE
LOG2_E = math.log2(math.e)


def _mla_qkv_kernel(lat_ref, qa_ref, kva_ref, gq_ref, gk_ref, wq_ref, wkv_ref, cs_ref, sn_ref,
                    ones_ref, perm_ref, q_ref, k_ref, vt_ref, *, heads, q_lora, kv_lora):
    two = lambda a: jnp.concatenate([a, a], axis=1)
    cs, sn = two(cs_ref[...]), two(sn_ref[...])
    lat = lat_ref[...]
    qn = _rms(lat[:, :q_lora], qa_ref[...]).astype(BF16)
    kvn = _rms(lat[:, q_lora:q_lora + kv_lora], kva_ref[...]).astype(BF16)
    k_rope = two(lat[:, q_lora + kv_lora:])

    def head_norm(a, g, post):
        sumsq = jnp.dot((a * a).astype(BF16), ones_ref[...], preferred_element_type=F32)
        ag = a * g
        pair = jnp.dot(ag.astype(BF16), perm_ref[...], preferred_element_type=F32)
        return (ag * cs + pair * sn) * (lax.rsqrt(sumsq * (1.0 / C_QK) + NORM_EPS) * post)

    gq, gk = two(gq_ref[...]), two(gk_ref[...])
    kv = jnp.dot(kvn, wkv_ref[...], preferred_element_type=F32)
    for p in range(heads * LANES // A_PAIR):
        sl = slice(p * A_PAIR, (p + 1) * A_PAIR)
        q = jnp.dot(qn, wq_ref[:, sl], preferred_element_type=F32)
        q_ref[:, sl] = head_norm(q, gq, C_QK ** -0.5 * LOG2_E).astype(BF16)
        k_ref[:, sl] = head_norm(kv[:, sl] + k_rope, gk, 1.0).astype(BF16)
    vt_ref[...] = kv[:, heads * LANES:].T.astype(BF16)


def _mla_attn_kernel(q_ref, k_ref, vt_ref, o_ref, *, pack):
    ones = jnp.ones((16, k_ref.shape[1]), BF16)
    tq = q_ref.shape[1]
    halves = 2 if tq % (2 * LANES) == 0 else 1
    qw = tq // halves
    work = [(i, j) for i in range(pack) for j in range(halves)]
    st = [lax.dot_general(k_ref[0, :, i * LANES:(i + 1) * LANES], q_ref[0, j * qw:(j + 1) * qw, i * LANES:(i + 1) * LANES],
                          (((1,), (1,)), ((), ())), preferred_element_type=F32).astype(BF16) for i, j in work]
    m = [jnp.max(t, axis=0, keepdims=True) for t in st]
    p = [jnp.exp2(t - mm) for t, mm in zip(st, m)]
    vts = [jnp.concatenate([vt_ref[0, i * C_V:(i + 1) * C_V, :], ones], axis=0) for i in range(pack)]
    ol = [jnp.dot(vts[i], pp, preferred_element_type=F32) for (i, j), pp in zip(work, p)]
    for (i, j), t in zip(work, ol):
        o_ref[0, i * C_V:(i + 1) * C_V, j * qw:(j + 1) * qw] = (t[:C_V] / t[C_V:C_V + 1]).astype(o_ref.dtype)


def _matmul_res_t_kernel(at_ref, w_ref, x_ref, o_ref):
    o_ref[...] = x_ref[...] + lax.dot_general(at_ref[...], w_ref[...], (((0,), (0,)), ((), ())),
                                              preferred_element_type=F32)


def matmul_residual_t(a_t, w, x, batch, s_len):
    k, n = w.shape
    tm = _row_tile(s_len, 512)
    tiles = s_len // tm
    return pl.pallas_call(
        _matmul_res_t_kernel,
        grid=(batch * tiles,),
        in_specs=[pl.BlockSpec((None, k, tm), lambda i: (i // tiles, 0, i % tiles)),
                  pl.BlockSpec((k, n), lambda i: (0, 0)),
                  pl.BlockSpec((tm, n), lambda i: (i, 0))],
        out_specs=pl.BlockSpec((tm, n), lambda i: (i, 0)),
        out_shape=jax.ShapeDtypeStruct(x.shape, F32),
        compiler_params=_params("parallel"),
        name="matmul_residual_t",
    )(a_t, w, x)


def _pad_cols(w, heads, real, slot):
    k = w.shape[0]
    w = w.reshape(k, heads, real)
    return jnp.pad(w, ((0, 0), (0, 0), (0, slot - real))).reshape(k, heads * slot)


def mixer_c(x, ln, wa, qa_norm, kva_norm, wq_b, wkv_b, qk_norm, wo, cs, sn, batch, s_len):
    t, d = x.shape
    q_lora, kv_lora = qa_norm.shape[0], kva_norm.shape[0]
    heads = wo.shape[0] // C_V
    lat_w = q_lora + kv_lora + LANES
    wa_p = jnp.pad(wa, ((0, 0), (0, lat_w - wa.shape[1]))).astype(BF16)
    lat = norm_matmul(x, ln, wa_p, F32, tn=lat_w)
    wq_p = _pad_cols(wq_b, heads, C_QK, LANES).astype(BF16)
    wkv = wkv_b.reshape(kv_lora, heads, C_NOPE + C_V)
    wk_p = jnp.pad(wkv[:, :, :C_NOPE], ((0, 0), (0, 0), (C_ROPE, LANES - C_QK))).reshape(kv_lora, heads * LANES)
    wv = wkv[:, :, C_NOPE:].reshape(kv_lora, heads * C_V)
    wkv_p = jnp.concatenate([wk_p, wv], axis=1).astype(BF16)
    gq = jnp.pad(qk_norm[0], (0, LANES - C_QK)).reshape(1, LANES)
    gk = jnp.pad(qk_norm[1], (0, LANES - C_QK)).reshape(1, LANES)
    tm = _row_tile(s_len, 512)
    tiles = s_len // tm
    row = lambda n: pl.BlockSpec((tm, n), lambda i: (i, 0))
    full = lambda a: pl.BlockSpec(a.shape, lambda i: (0, 0))
    qa2, kva2 = qa_norm.reshape(1, -1), kva_norm.reshape(1, -1)
    ones, perm = _head_helper_matrices()
    q, k, vt = pl.pallas_call(
        functools.partial(_mla_qkv_kernel, heads=heads, q_lora=q_lora, kv_lora=kv_lora),
        grid=(t // tm,),
        in_specs=[row(lat_w), full(qa2), full(kva2), full(gq), full(gk), full(wq_p), full(wkv_p),
                  row(LANES), row(LANES), full(ones), full(perm)],
        out_specs=[row(heads * LANES), row(heads * LANES),
                   pl.BlockSpec((None, heads * C_V, tm), lambda i: (i // tiles, 0, i % tiles))],
        out_shape=[jax.ShapeDtypeStruct((t, heads * LANES), BF16),
                   jax.ShapeDtypeStruct((t, heads * LANES), BF16),
                   jax.ShapeDtypeStruct((batch, heads * C_V, s_len), BF16)],
        compiler_params=_params("parallel"),
        name="mla_qkv",
    )(lat, qa2, kva2, gq, gk, wq_p, wkv_p, cs, sn, ones, perm)
    pack = LANES // C_V
    tq = min(512, s_len)
    o_t = pl.pallas_call(
        functools.partial(_mla_attn_kernel, pack=pack),
        grid=(batch, heads // pack, s_len // tq),
        in_specs=[pl.BlockSpec((1, tq, pack * LANES), lambda b, h, i: (b, i, h)),
                  pl.BlockSpec((1, s_len, pack * LANES), lambda b, h, i: (b, 0, h)),
                  pl.BlockSpec((1, pack * C_V, s_len), lambda b, h, i: (b, h, 0))],
        out_specs=pl.BlockSpec((1, pack * C_V, tq), lambda b, h, i: (b, h, i)),
        out_shape=jax.ShapeDtypeStruct((batch, heads * C_V, s_len), BF16),
        compiler_params=_params("parallel", "parallel", "arbitrary"),
        name="mla_attention",
    )(q.reshape(batch, s_len, -1), k.reshape(batch, s_len, -1), vt)
    return matmul_residual_t(o_t, wo.astype(BF16), x, batch, s_len)


B_HEAD_DIM = 64
B_DECAY_SCALE = math.exp(-0.5)
B_GN_EPS = 64e-5
WKV_CHUNK = 64
WKV_HEADS_PER_STEP = 16
WKV_SEQS_PER_STEP = 2
WKV_PACK = 4


def _bdot(a, b):
    return jnp.dot(a.astype(BF16), b.astype(BF16), preferred_element_type=F32)


def _rwkv_prep_kernel(x_ref, xp_ref, xn_ref, ln_ref, mu_ref, wrkv_ref, g1_ref, g2_ref, w0_ref, w1_ref, w2_ref,
                      a0_ref, a1_ref, a2_ref, r_ref, k_ref, v_ref, g_ref, lw_ref, as_ref, *, tiles_per_seq):
    i = pl.program_id(0)
    tm = x_ref.shape[0]
    ln = ln_ref[...]
    h = _rms(x_ref[...], ln)
    pos = i % tiles_per_seq
    prev_row = jnp.where(pos == 0, 0.0, _rms(xp_ref[7:8, :], ln))
    next_row = jnp.where(pos == tiles_per_seq - 1, 0.0, _rms(xn_ref[0:1, :], ln))
    row = lax.broadcasted_iota(I32, (tm, 1), 0)
    d_f = jnp.where(row == 0, prev_row, pltpu.roll(h, 1, 0)) - h
    d_b = jnp.where(row == tm - 1, next_row, pltpu.roll(h, tm - 1, 0)) - h
    d_c = 0.5 * (d_f + d_b)
    mu = mu_ref[...]
    mix = lambda dd, n: h + dd * mu[n:n + 1]
    r_ref[...] = _bdot(mix(d_c, 0), wrkv_ref[0]).astype(r_ref.dtype)
    k_ref[...] = _bdot(mix(d_c, 1), wrkv_ref[1]).astype(k_ref.dtype)
    v_ref[...] = _bdot(mix(d_c, 2), wrkv_ref[2]).astype(v_ref.dtype)
    g_ref[...] = _bdot(jax.nn.sigmoid(_bdot(mix(d_c, 3), g1_ref[...])), g2_ref[...]).astype(g_ref.dtype)
    for dr, dd in enumerate((d_f, d_b)):
        z = w0_ref[dr:dr + 1, :] + _bdot(jnp.tanh(_bdot(mix(dd, 4 + dr), w1_ref[dr])), w2_ref[dr])
        lw_ref[dr] = -B_DECAY_SCALE * jax.nn.sigmoid(z)
        aa = a0_ref[dr:dr + 1, :] + _bdot(_bdot(mix(dd, 6 + dr), a1_ref[dr]), a2_ref[dr])
        as_ref[dr] = jax.nn.sigmoid(aa).astype(as_ref.dtype)


def _wkv_kernel(rf_ref, kf_ref, vf_ref, rb_ref, kb_ref, vb_ref, lwf_ref, lwb_ref, asf_ref, asb_ref, kk_ref, ka_ref,
                yf_ref, yb_ref, st_ref, *, hb):
    n = B_HEAD_DIM
    nb, c = rf_ref.shape[0], rf_ref.shape[1]

    @pl.when(pl.program_id(1) == 0)
    def _():
        st_ref[...] = jnp.zeros_like(st_ref)

    nt = (((1,), (1,)), ((), ()))
    tn = (((0,), (0,)), ((), ()))
    mm = (((1,), (0,)), ((), ()))
    low = lax.broadcasted_iota(I32, (1, LANES), 1) < n
    pk = WKV_PACK
    gw = pk * n
    per_dir = hb // pk
    gsl = [slice(g * gw, (g + 1) * gw) for g in range(per_dir)]
    cut = lambda a: [a[:, sl] for sl in gsl]
    row_p = lax.broadcasted_iota(I32, (c, gw), 0)
    col_p = lax.broadcasted_iota(I32, (c, gw), 1) % n

    def operands(sgn, bi, r_ref, k_ref, v_ref, lw_ref, as_ref):
        r_ref, k_ref, v_ref, lw_ref, as_ref = (ref.at[bi] for ref in (r_ref, k_ref, v_ref, lw_ref, as_ref))
        ahead = (lax.broadcasted_iota(I32, (c, c), 0) - lax.broadcasted_iota(I32, (c, c), 1)) * sgn
        lw = lw_ref[...]
        cl = jnp.dot((ahead >= 0).astype(F32), lw, preferred_element_type=F32, precision=lax.Precision.HIGHEST)
        total = jnp.sum(lw, axis=0, keepdims=True)
        e_in, e_ex, e_ng, e_end = jnp.exp(cl), jnp.exp(cl - lw), jnp.exp(-cl), jnp.exp(total - cl)
        rr, kk_raw, vv, asig = (ref[...].astype(F32) for ref in (r_ref, k_ref, v_ref, as_ref))
        kk_all = kk_raw * kk_ref[...]
        kdir = kk_raw * (1.0 + (asig - 1.0) * ka_ref[...])
        sq = kk_all * kk_all
        norms = []
        for p in range(sq.shape[1] // LANES):
            slab = sq[:, p * LANES:(p + 1) * LANES]
            s_lo = jnp.sum(jnp.where(low, slab, 0.0), axis=-1, keepdims=True)
            s_hi = jnp.sum(jnp.where(low, 0.0, slab), axis=-1, keepdims=True)
            norms.append(jnp.where(low, s_lo, s_hi))
        kkn_all = kk_all * lax.rsqrt(jnp.maximum(jnp.concatenate(norms, axis=-1), 1e-24))
        beta_all = kkn_all * asig
        ahead_p = (row_p - col_p) * sgn
        return dict(
            a_t=cut((-kkn_all * e_ex).astype(BF16)), b_t=cut((beta_all * e_ng).astype(BF16)),
            rt=cut((rr * e_in).astype(BF16)), kt=cut((kdir * e_ng).astype(BF16)), v=cut(vv.astype(BF16)),
            b_h=cut((beta_all * e_end).astype(BF16)), k_hb=cut((kdir * e_end).astype(BF16)),
            p_end=cut(jnp.exp(total)), strict=[ahead_p > 0] * per_dir, incl=[ahead_p >= 0] * per_dir)

    both = None
    for bi in range(nb):
        for part in (operands(1, bi, rf_ref, kf_ref, vf_ref, lwf_ref, asf_ref),
                     operands(-1, bi, rb_ref, kb_ref, vb_ref, lwb_ref, asb_ref)):
            both = part if both is None else {key: both[key] + part[key] for key in part}
    a_t, b_t, rt, kt, v, b_h, k_hb = (both[key] for key in ("a_t", "b_t", "rt", "kt", "v", "b_h", "k_hb"))
    p_end, strict_p, incl_p = both["p_end"], both["strict"], both["incl"]
    groups = range(nb * 2 * per_dir)

    diag_block = ((lax.broadcasted_iota(I32, (gw, gw), 0) // n)
                  == (lax.broadcasted_iota(I32, (gw, gw), 1) // n))
    diag_bf = diag_block.astype(BF16)

    def bd(p):
        return jnp.concatenate([p] * pk, axis=0) * diag_bf

    def fold(full):
        kept = jnp.where(diag_block, full, 0.0)
        out = kept[:n]
        for h in range(1, pk):
            out = out + kept[h * n:(h + 1) * n]
        return out

    eye_p = (row_p == col_p).astype(F32)
    same_block = lambda m: (row_p // m) == (col_p // m)
    dots = lambda a, b, dims: [lax.dot_general(p, q, dims, preferred_element_type=F32) for p, q in zip(a, b)]
    rows = lambda *parts: [jnp.concatenate(list(ps), axis=0) for ps in zip(*parts)]
    bf = lambda xs: [t.astype(BF16) for t in xs]
    bds = lambda xs: [bd(t) for t in xs]
    gram = dots(rows(a_t, rt), rows(bds(b_t), bds(kt)), nt)
    l_ab = [jnp.where(s, g[:c, :gw], 0.0) for s, g in zip(strict_p, gram)]
    l_ak = [jnp.where(s, g[:c, gw:], 0.0).astype(BF16) for s, g in zip(strict_p, gram)]
    m_r = [jnp.concatenate([jnp.where(i, g[c:, :gw], 0.0), jnp.where(i, g[c:, gw:], 0.0)],
                           axis=1).astype(BF16) for i, g in zip(incl_p, gram)]
    in_pair = same_block(2)
    x = [eye_p + jnp.where(in_pair, l, 0.0) for l in l_ab]
    m = 2
    while m < c:
        level = same_block(2 * m) & jnp.logical_not(same_block(m))
        xb = bf(x)
        l_m = [jnp.where(level, l, 0.0).astype(BF16) for l in l_ab]
        xl = bf(dots(xb, bds(l_m), mm))
        x = [xx + t for xx, t in zip(x, dots(xl, bds(xb), mm))]
        m *= 2
    s0 = [st_ref[g] for g in groups]
    bd_s = bds(bf(s0))
    bd_v = bds(v)
    w = [p + q for p, q in zip(dots(a_t, bd_s, nt), dots(l_ak, bd_v, mm))]
    u = bf(dots(bf(x), bds(bf(w)), mm))
    y = [p + q for p, q in zip(dots(rt, bd_s, nt), dots(m_r, rows(bds(u), bd_v), mm))]
    s_new = dots(rows(u, v), rows(b_h, k_hb), tn)
    for g in groups:
        st_ref[g] = s0[g] * p_end[g] + fold(s_new[g])
        bi, within = divmod(g, 2 * per_dir)
        out_ref = yf_ref if within < per_dir else yb_ref
        out_ref[bi, :, gsl[within % per_dir]] = y[g]


def _rwkv_out_kernel(x_ref, yf_ref, yb_ref, r_ref, k_ref, v_ref, g_ref, as_ref, ka_ref, rk_ref, lnx_ref, wo_ref,
                     sum_ref, o_ref):
    d = x_ref.shape[1]
    n = B_HEAD_DIM
    y = yf_ref[...] + yb_ref[...]
    a_sum = as_ref[0].astype(F32) + as_ref[1].astype(F32)
    kd = k_ref[...].astype(F32) * (2.0 + (a_sum - 2.0) * ka_ref[...])
    prod = r_ref[...].astype(F32) * kd * rk_ref[...]
    slab = sum_ref.shape[0]

    def head_sum(a):
        return jnp.dot(a.astype(BF16), sum_ref[...], preferred_element_type=F32)

    parts = []
    for p in range(d // slab):
        sl = slice(p * slab, (p + 1) * slab)
        ys = y[:, sl]
        cen = ys - head_sum(ys) * (1.0 / n)
        var = head_sum(cen * cen) * (1.0 / n)
        yn = cen * lax.rsqrt(var + B_GN_EPS) * lnx_ref[0:1, sl] + lnx_ref[1:2, sl]
        bonus = head_sum(prod[:, sl]) * v_ref[:, sl]
        parts.append(((yn + bonus) * g_ref[:, sl]).astype(BF16))
    o_ref[...] = x_ref[...] + jnp.dot(jnp.concatenate(parts, axis=-1), wo_ref[...], preferred_element_type=F32)


def mixer_b(x, ln, mu, w_rkv, w0, w1, w2, a0, a1, a2, g1, g2, k_k, k_a, r_k, ln_x, wo, batch, s_len):
    t, d = x.shape
    bf = lambda a: a.astype(BF16)
    halo = 8
    full = lambda a: pl.BlockSpec(a.shape, lambda i: (0,) * a.ndim)
    resident = lambda a: pl.BlockSpec(a.shape, lambda i: (0,) * a.ndim, pipeline_mode=pl.Buffered(1))
    ln2 = ln.reshape(1, d)
    weights = (ln2, mu, bf(w_rkv), bf(g1), bf(g2), w0, bf(w1), bf(w2), a0, bf(a1), bf(a2))
    tok = lambda dt: jax.ShapeDtypeStruct((t, d), dt)
    tok2 = lambda dt: jax.ShapeDtypeStruct((2, t, d), dt)
    tm = _row_tile(s_len, 512)
    tiles_per_seq = s_len // tm
    row = lambda n: pl.BlockSpec((tm, n), lambda i: (i, 0))
    row2 = lambda n: pl.BlockSpec((2, tm, n), lambda i: (0, i, 0))
    r, k, v, g, lw, asig = pl.pallas_call(
        functools.partial(_rwkv_prep_kernel, tiles_per_seq=tiles_per_seq),
        grid=(t // tm,),
        in_specs=[row(d),
                  pl.BlockSpec((halo, d), lambda i: (jnp.maximum(i * (tm // halo) - 1, 0), 0)),
                  pl.BlockSpec((halo, d), lambda i: (jnp.minimum((i + 1) * (tm // halo), t // halo - 1), 0))]
                 + [resident(a) for a in weights],
        out_specs=[row(d), row(d), row(d), row(d), row2(d), row2(d)],
        out_shape=[tok(BF16), tok(BF16), tok(BF16), tok(BF16), tok2(F32), tok2(BF16)],
        compiler_params=_params("parallel"),
        name="rwkv_prep",
    )(x, x, x, *weights)
    tm = _row_tile(s_len, 256)
    row = lambda n: pl.BlockSpec((tm, n), lambda i: (i, 0))
    row2 = lambda n: pl.BlockSpec((2, tm, n), lambda i: (0, i, 0))

    hb = WKV_HEADS_PER_STEP
    width = hb * B_HEAD_DIM
    c = min(WKV_CHUNK, s_len)
    nc = s_len // c
    assert width == d
    nb = math.gcd(batch, WKV_SEQS_PER_STEP)
    tok_f = pl.BlockSpec((nb, c, d), lambda b, ci: (b, ci, 0))
    tok_b = pl.BlockSpec((nb, c, d), lambda b, ci: (b, nc - 1 - ci, 0))
    dir_f = pl.BlockSpec((None, nb, c, d), lambda b, ci: (0, b, ci, 0))
    dir_b = pl.BlockSpec((None, nb, c, d), lambda b, ci: (1, b, nc - 1 - ci, 0))
    par = pl.BlockSpec((1, d), lambda b, ci: (0, 0))
    v3 = lambda a: a.reshape(batch, s_len, d)
    v4 = lambda a: a.reshape(2, batch, s_len, d)
    seq_out = jax.ShapeDtypeStruct((batch, s_len, d), F32)
    y_f, y_b = pl.pallas_call(
        functools.partial(_wkv_kernel, hb=hb),
        grid=(batch // nb, nc),
        in_specs=[tok_f, tok_f, tok_f, tok_b, tok_b, tok_b, dir_f, dir_b, dir_f, dir_b, par, par],
        out_specs=[tok_f, tok_b],
        out_shape=[seq_out, seq_out],
        scratch_shapes=[pltpu.VMEM((nb * 2 * hb // WKV_PACK, B_HEAD_DIM, WKV_PACK * B_HEAD_DIM), F32)],
        compiler_params=_params("parallel", "arbitrary"),
        name="wkv_scan",
    )(v3(r), v3(k), v3(v), v3(r), v3(k), v3(v), v4(lw), v4(lw), v4(asig), v4(asig),
      k_k.reshape(1, d), k_a.reshape(1, d))

    lane = jnp.arange(WKV_PACK * B_HEAD_DIM)
    same_head = ((lane[:, None] // B_HEAD_DIM) == (lane[None, :] // B_HEAD_DIM)).astype(BF16)
    params = (k_a.reshape(1, d), r_k.reshape(1, d), ln_x, bf(wo), same_head)
    return pl.pallas_call(
        _rwkv_out_kernel,
        grid=(t // tm,),
        in_specs=[row(d), row(d), row(d), row(d), row(d), row(d), row(d), row2(d)] + [full(a) for a in params],
        out_specs=row(d),
        out_shape=tok(F32),
        compiler_params=_params("parallel"),
        name="rwkv_out",
    )(x, y_f.reshape(t, d), y_b.reshape(t, d), r, k, v, g, asig, *params)


TOP_K = 2
ROUTE_EXPERT, ROUTE_GATE, ROUTE_RANK = 0, TOP_K, 2 * TOP_K
MOE_ROW_TILE = 512
ROW_DMA_UNROLL = 8


def _router_kernel(x_ref, g_ref, wr_ref, hn_ref, route_ref, cnt_ref, run_ref, *, n_exp):
    @pl.when(pl.program_id(0) == 0)
    def _():
        run_ref[...] = jnp.zeros_like(run_ref)

    tm = x_ref.shape[0]
    h = _rms(x_ref[...], g_ref[...])
    hn_ref[...] = h
    logits = jnp.dot(h, wr_ref[...], preferred_element_type=F32, precision=lax.Precision.HIGHEST)
    lane = lax.broadcasted_iota(I32, (tm, LANES), 1)
    logits = jnp.where(lane < n_exp, logits, NEG_INF)

    def top1(v):
        m = jnp.max(v, axis=-1, keepdims=True)
        return m, jnp.min(jnp.where(v == m, lane, LANES), axis=-1, keepdims=True)

    m1, i1 = top1(logits)
    m2, i2 = top1(jnp.where(lane == i1, NEG_INF, logits))
    e = jnp.exp(m2 - m1)
    g1 = 1.0 / (1.0 + e)
    g2 = e * g1
    sel = ((lane == i1) | (lane == i2)).astype(F32)
    earlier = (lax.broadcasted_iota(I32, (tm, tm), 0) > lax.broadcasted_iota(I32, (tm, tm), 1)).astype(BF16)
    before = jnp.dot(earlier, sel.astype(BF16), preferred_element_type=F32) + run_ref[...]
    r1 = jnp.sum(jnp.where(lane == i1, before, 0.0), axis=-1, keepdims=True)
    r2 = jnp.sum(jnp.where(lane == i2, before, 0.0), axis=-1, keepdims=True)
    run_ref[...] += jnp.sum(sel, axis=0, keepdims=True)
    cnt_ref[...] = run_ref[...]
    route = jnp.zeros((tm, LANES), F32)
    for off, vals in ((ROUTE_EXPERT, (i1.astype(F32), i2.astype(F32))), (ROUTE_GATE, (g1, g2)), (ROUTE_RANK, (r1, r2))):
        for kk, v in enumerate(vals):
            route = jnp.where(lane == off + kk, v, route)
    route_ref[...] = route


def _dispatch_kernel(dest_ref, hn_ref, xs_in_ref, xs_ref, sem):
    del xs_in_ref
    tm = hn_ref.shape[0]

    def row_copy(r, slot):
        return pltpu.make_async_copy(hn_ref.at[pl.ds(r, 1)], xs_ref.at[pl.ds(dest_ref[TOP_K * r + slot], 1)], sem)

    def start(r, c):
        for slot in range(TOP_K):
            row_copy(r, slot).start()
        return c

    def wait(r, c):
        for slot in range(TOP_K):
            row_copy(r, slot).wait()
        return c

    lax.fori_loop(0, tm, start, 0, unroll=ROW_DMA_UNROLL)
    lax.fori_loop(0, tm, wait, 0, unroll=ROW_DMA_UNROLL)


def _expert_ffn_kernel(te_ref, x_ref, w1_ref, w3_ref, w2_ref, o_ref, xb_ref, acc_ref):
    del te_ref
    j = pl.program_id(1)

    @pl.when(j == 0)
    def _():
        xb_ref[...] = x_ref[...].astype(BF16)
        acc_ref[...] = jnp.zeros_like(acc_ref)

    acc_ref[...] += _swiglu_chunks(xb_ref[...], w1_ref, w3_ref, w2_ref)

    @pl.when(j == pl.num_programs(1) - 1)
    def _():
        o_ref[...] = acc_ref[...]


def _moe_combine_kernel(dest_ref, route_ref, x_ref, ys_ref, o_ref, buf_ref, sem):
    tm = x_ref.shape[0]

    def row_copy(r, slot):
        return pltpu.make_async_copy(ys_ref.at[pl.ds(dest_ref[TOP_K * r + slot], 1)], buf_ref.at[slot, pl.ds(r, 1)], sem)

    def start(r, c):
        for slot in range(TOP_K):
            row_copy(r, slot).start()
        return c

    def wait(r, c):
        for slot in range(TOP_K):
            row_copy(r, slot).wait()
        return c

    lax.fori_loop(0, tm, start, 0, unroll=ROW_DMA_UNROLL)
    lax.fori_loop(0, tm, wait, 0, unroll=ROW_DMA_UNROLL)
    route = route_ref[...]
    out = x_ref[...]
    for slot in range(TOP_K):
        out = out + route[:, ROUTE_GATE + slot:ROUTE_GATE + slot + 1] * buf_ref[slot]
    o_ref[...] = out


def moe_ffn(x, ln, router, w13, w2, tf=1792):
    t, d = x.shape
    n_exp, dff = w2.shape[0], w2.shape[1]
    assert dff % tf == 0
    nj = dff // tf
    tmx = min(MOE_ROW_TILE, t)
    tm = _row_tile(t, 512)
    wr = jnp.pad(router, ((0, 0), (0, LANES - n_exp)))
    hn, route, cnt = pl.pallas_call(
        functools.partial(_router_kernel, n_exp=n_exp),
        grid=(t // tm,),
        in_specs=[pl.BlockSpec((tm, d), lambda i: (i, 0)),
                  pl.BlockSpec((1, d), lambda i: (0, 0)),
                  pl.BlockSpec((d, LANES), lambda i: (0, 0))],
        out_specs=[pl.BlockSpec((tm, d), lambda i: (i, 0)),
                   pl.BlockSpec((tm, LANES), lambda i: (i, 0)),
                   pl.BlockSpec((1, LANES), lambda i: (0, 0))],
        out_shape=[jax.ShapeDtypeStruct((t, d), F32),
                   jax.ShapeDtypeStruct((t, LANES), F32),
                   jax.ShapeDtypeStruct((1, LANES), F32)],
        scratch_shapes=[pltpu.VMEM((1, LANES), F32)],
        compiler_params=_params("arbitrary"),
        name="moe_router",
    )(x, ln.reshape(1, d), wr)

    expert = route[:, ROUTE_EXPERT:ROUTE_EXPERT + TOP_K].astype(I32)
    rank = route[:, ROUTE_RANK:ROUTE_RANK + TOP_K].astype(I32)
    counts = cnt[0, :n_exp].astype(I32)
    padded = (counts + tmx - 1) // tmx * tmx
    ends = jnp.cumsum(padded)
    dest = ((ends - padded)[expert] + rank).reshape(TOP_K * t)
    n_tiles = (TOP_K * t) // tmx + n_exp
    rows = n_tiles * tmx
    tile_start = jnp.arange(n_tiles, dtype=I32) * tmx
    tile_expert = jnp.minimum(jnp.sum((tile_start[:, None] >= ends[None, :]).astype(I32), axis=1), n_exp - 1)

    smem_rows = lambda n: pl.BlockSpec((TOP_K * n,), lambda i: (i,), memory_space=pltpu.SMEM)
    xs = pl.pallas_call(
        _dispatch_kernel,
        grid=(t // tm,),
        in_specs=[smem_rows(tm),
                  pl.BlockSpec((tm, d), lambda i: (i, 0)),
                  pl.BlockSpec(memory_space=pl.ANY)],
        out_specs=pl.BlockSpec(memory_space=pl.ANY),
        out_shape=jax.ShapeDtypeStruct((rows, d), F32),
        scratch_shapes=[pltpu.SemaphoreType.DMA(())],
        input_output_aliases={2: 0},
        compiler_params=_params("arbitrary"),
        name="moe_dispatch",
    )(dest, hn, jnp.zeros((rows, d), F32))

    ys = pl.pallas_call(
        _expert_ffn_kernel,
        grid_spec=pltpu.PrefetchScalarGridSpec(
            num_scalar_prefetch=1,
            grid=(n_tiles, nj),
            in_specs=[pl.BlockSpec((tmx, d), lambda i, j, te: (i, 0)),
                      pl.BlockSpec((None, d, tf), lambda i, j, te: (te[i], 0, j)),
                      pl.BlockSpec((None, d, tf), lambda i, j, te: (te[i], 0, nj + j)),
                      pl.BlockSpec((None, tf, d), lambda i, j, te: (te[i], j, 0))],
            out_specs=pl.BlockSpec((tmx, d), lambda i, j, te: (i, 0)),
            scratch_shapes=[pltpu.VMEM((tmx, d), BF16), pltpu.VMEM((tmx, d), F32)]),
        out_shape=jax.ShapeDtypeStruct((rows, d), F32),
        compiler_params=_params("parallel", "arbitrary"),
        name="moe_expert_ffn",
    )(tile_expert, xs, w13, w13, w2)

    return pl.pallas_call(
        _moe_combine_kernel,
        grid=(t // tm,),
        in_specs=[smem_rows(tm),
                  pl.BlockSpec((tm, LANES), lambda i: (i, 0)),
                  pl.BlockSpec((tm, d), lambda i: (i, 0)),
                  pl.BlockSpec(memory_space=pl.ANY)],
        out_specs=pl.BlockSpec((tm, d), lambda i: (i, 0)),
        out_shape=jax.ShapeDtypeStruct((t, d), F32),
        scratch_shapes=[pltpu.VMEM((TOP_K, tm, d), F32), pltpu.SemaphoreType.DMA(())],
        compiler_params=_params("arbitrary"),
        name="moe_combine",
    )(dest, route, x, ys)


def kernel(x, positions, ln_mix, ln_ffn, a_wqkv, a_qk_norm, a_wo, b_mu, b_wrkv, b_w0, b_w1, b_w2, b_a0, b_a1, b_a2, b_g1, b_g2, b_kk, b_ka, b_rk, b_lnx, b_wo, c_wa, c_qa_norm, c_kva_norm, c_wq_b, c_wkv_b, c_qk_norm, c_wo, f_w13, f_w2, m_router, m_w13, m_w2):
    batch, s_len, d = x.shape
    t = batch * s_len
    cs, sn = _rope_tables(positions)
    cs, sn = cs.reshape(t, LANES), sn.reshape(t, LANES)
    h = x.reshape(t, d)
    n_mixers = 3
    for i in range(ln_mix.shape[0]):
        j, kind = divmod(i, n_mixers)
        if kind == 0:
            h = mixer_a(h, ln_mix[i], a_wqkv[j], a_qk_norm[j], a_wo[j], cs, sn, batch, s_len)
        elif kind == 1:
            h = mixer_b(h, ln_mix[i], b_mu[j], b_wrkv[j], b_w0[j], b_w1[j], b_w2[j], b_a0[j], b_a1[j], b_a2[j],
                        b_g1[j], b_g2[j], b_kk[j], b_ka[j], b_rk[j], b_lnx[j], b_wo[j], batch, s_len)
        else:
            h = mixer_c(h, ln_mix[i], c_wa[j], c_qa_norm[j], c_kva_norm[j], c_wq_b[j], c_wkv_b[j], c_qk_norm[j],
                        c_wo[j], cs, sn, batch, s_len)
        f = i // 2
        if i % 2 == 0:
            h = dense_ffn(h, ln_ffn[i], f_w13[f].astype(BF16), f_w2[f].astype(BF16))
        else:
            h = moe_ffn(h, ln_ffn[i], m_router[f], m_w13[f].astype(BF16), m_w2[f].astype(BF16))
    return h.reshape(batch, s_len, d)
```

```python
import functools
import math

import jax
import jax.numpy as jnp
from jax import lax
from jax.experimental import pallas as pl
from jax.experimental.pallas import tpu as pltpu

F32 = jnp.float32
BF16 = jnp.bfloat16
I32 = jnp.int32

NORM_EPS = 1e-6
ROPE_THETA = 500000.0
NEG_INF = -1e30
LANES = 128
ROT_DIM = 32
ROT_HALF = ROT_DIM // 2
DILATED_PAIRS = ((128, 1), (512, 4), (2048, 16))
A_HEAD_DIM = 128
VMEM_LIMIT = 48 * 1024 * 1024


def _params(*sem):
    return pltpu.CompilerParams(dimension_semantics=sem, vmem_limit_bytes=VMEM_LIMIT)


def _rms(x, g):
    return x * lax.rsqrt(jnp.mean(x * x, axis=-1, keepdims=True) + NORM_EPS) * g


def _row_tile(t, want):
    tm = min(t, want)
    assert t % tm == 0
    return tm


def _norm_matmul_kernel(x_ref, g_ref, w_ref, o_ref, xn_ref):
    @pl.when(pl.program_id(1) == 0)
    def _():
        xn_ref[...] = _rms(x_ref[...], g_ref[...]).astype(BF16)

    o_ref[...] = jnp.dot(xn_ref[...], w_ref[...], preferred_element_type=F32).astype(o_ref.dtype)


def norm_matmul(x, g, w, out_dtype, tn):
    t, k = x.shape
    n = w.shape[1]
    tm = _row_tile(t, 1024)
    assert n % tn == 0
    return pl.pallas_call(
        _norm_matmul_kernel,
        grid=(t // tm, n // tn),
        in_specs=[pl.BlockSpec((tm, k), lambda i, j: (i, 0)),
                  pl.BlockSpec((1, k), lambda i, j: (0, 0)),
                  pl.BlockSpec((k, tn), lambda i, j: (0, j))],
        out_specs=pl.BlockSpec((tm, tn), lambda i, j: (i, j)),
        out_shape=jax.ShapeDtypeStruct((t, n), out_dtype),
        scratch_shapes=[pltpu.VMEM((tm, k), BF16)],
        compiler_params=_params("parallel", "arbitrary"),
        name="norm_matmul",
    )(x, g.reshape(1, k), w)


def _matmul_res_kernel(a_ref, w_ref, x_ref, o_ref):
    o_ref[...] = x_ref[...] + jnp.dot(a_ref[...], w_ref[...], preferred_element_type=F32)


def matmul_residual(a, w, x):
    t, k = a.shape
    n = w.shape[1]
    tm = _row_tile(t, 1024)
    return pl.pallas_call(
        _matmul_res_kernel,
        grid=(t // tm,),
        in_specs=[pl.BlockSpec((tm, k), lambda i: (i, 0)),
                  pl.BlockSpec((k, n), lambda i: (0, 0)),
                  pl.BlockSpec((tm, n), lambda i: (i, 0))],
        out_specs=pl.BlockSpec((tm, n), lambda i: (i, 0)),
        out_shape=jax.ShapeDtypeStruct((t, n), F32),
        compiler_params=_params("parallel"),
        name="matmul_residual",
    )(a, w, x)


def _rope_tables(positions):
    b, s = positions.shape
    inv_freq = ROPE_THETA ** (-jnp.arange(0, ROT_DIM, 2, dtype=F32) / ROT_DIM)
    ang = positions.astype(F32)[..., None] * inv_freq
    c, sn = jnp.cos(ang), jnp.sin(ang)
    rest = (b, s, LANES - ROT_DIM)
    cs_t = jnp.concatenate([c, c, jnp.ones(rest, F32)], axis=-1)
    sn_t = jnp.concatenate([-sn, sn, jnp.zeros(rest, F32)], axis=-1)
    return cs_t, sn_t


def _rope(x, cs, sn, lane):
    n = x.shape[-1]
    partner = jnp.where(lane < ROT_HALF, pltpu.roll(x, n - ROT_HALF, 1), pltpu.roll(x, ROT_HALF, 1))
    return x * cs + partner * sn


def _dil_attn_kernel(q_ref, k_ref, v_ref, o_ref, lse_ref, *, seq, qt, kw, half, heads):
    hd = A_HEAD_DIM
    qi = pl.program_id(2)
    q0 = pl.multiple_of(qi * qt, qt)
    start = jnp.clip(q0 - half, 0, seq - kw)
    start = pl.multiple_of(start, math.gcd(half, qt))
    qpos = q0 + lax.broadcasted_iota(I32, (qt, kw), 0)
    kpos = start + lax.broadcasted_iota(I32, (qt, kw), 1)
    valid = jnp.abs(qpos - kpos) <= half
    head_of_lane = lax.broadcasted_iota(I32, (1, LANES), 1) // (LANES // heads)
    ones = jnp.ones((kw, hd), BF16)
    work = [(r, slice(h * hd, (h + 1) * hd)) for r in range(q_ref.shape[0]) for h in range(heads)]
    s = [lax.dot_general(q_ref[r, :, sl], k_ref[r, pl.ds(start, kw), sl], (((1,), (1,)), ((), ())),
                         preferred_element_type=F32) for r, sl in work]
    s = [jnp.where(valid, t, NEG_INF) for t in s]
    m = [jnp.max(t, axis=-1, keepdims=True) for t in s]
    p = [jnp.exp(t - mm).astype(BF16) for t, mm in zip(s, m)]
    ol = [jnp.dot(pp, jnp.concatenate([v_ref[r, pl.ds(start, kw), sl], ones], axis=1), preferred_element_type=F32)
          for pp, (r, sl) in zip(p, work)]
    for r in range(q_ref.shape[0]):
        lse_row = jnp.zeros((qt, LANES), F32)
        for h in range(heads):
            i = r * heads + h
            o_ref[r, :, work[i][1]] = (ol[i][:, :hd] / ol[i][:, hd:]).astype(o_ref.dtype)
            lse_row = jnp.where(head_of_lane == h, m[i] + jnp.log(ol[i][:, hd:hd + 1]), lse_row)
        lse_ref[r] = lse_row


DIL_ATTN_ROWS = 512
A_PAIR = 2 * A_HEAD_DIM


def _head_helper_matrices():
    i = jnp.arange(A_PAIR)
    same_head = (i[:, None] // A_HEAD_DIM) == (i[None, :] // A_HEAD_DIM)
    dst = i % A_HEAD_DIM
    src = jnp.where(dst < ROT_HALF, i + ROT_HALF, i - ROT_HALF)
    perm = (i[:, None] == src[None, :]) & (dst[None, :] < ROT_DIM)
    return same_head.astype(BF16), perm.astype(BF16)


def _qkv_group_kernel(x_ref, g_ref, w_ref, cs_ref, sn_ref, gain_ref, mean_ref, perm_ref, o_ref, xn_ref, acc_ref, *, dil):
    j = pl.program_id(1)

    @pl.when(j == 0)
    def _():
        xn_ref[...] = _rms(x_ref[...], g_ref[...]).astype(BF16)

    acc = jnp.dot(xn_ref[...], w_ref[...], preferred_element_type=F32)
    tm, width = acc.shape

    def emit(vals):
        if dil == 1:
            o_ref[0, 0] = vals.astype(o_ref.dtype)
            return
        chunks = width // LANES
        for c in range(chunks):
            acc_ref[c] = vals[:, c * LANES:(c + 1) * LANES]
        for r in range(dil):
            o_ref[0, r] = jnp.concatenate([acc_ref[c, pl.ds(r, tm // dil, stride=dil), :] for c in range(chunks)],
                                          axis=-1).astype(o_ref.dtype)

    @pl.when(j < 2)
    def _():
        cs = jnp.concatenate([cs_ref[...]] * 2, axis=1)
        sn = jnp.concatenate([sn_ref[...]] * 2, axis=1)
        gain = gain_ref[pl.ds(j, 1), :]
        parts = []
        for p in range(width // A_PAIR):
            a = acc[:, p * A_PAIR:(p + 1) * A_PAIR]
            meansq = jnp.dot((a * a).astype(BF16), mean_ref[...], preferred_element_type=F32)
            ag = a * gain[:, p * A_PAIR:(p + 1) * A_PAIR]
            pair = jnp.dot(ag.astype(BF16), perm_ref[...], preferred_element_type=F32)
            parts.append((ag * cs + pair * sn) * lax.rsqrt(meansq + NORM_EPS))
        emit(jnp.concatenate(parts, axis=1))

    @pl.when(j == 2)
    def _():
        emit(acc)


def qkv_group_projection(x, ln, wqkv, cs, sn, gq, gk, group, batch, s_len, width):
    t, d = x.shape
    dil = DILATED_PAIRS[group][1]
    n_groups = len(DILATED_PAIRS)
    heads = width // A_HEAD_DIM
    tm = _row_tile(s_len, 1024)
    tiles = s_len // tm
    assert tm % (16 * dil) == 0
    gains = jnp.stack([jnp.tile(gq, heads) * A_HEAD_DIM ** -0.5, jnp.tile(gk, heads)])
    ones, perm = _head_helper_matrices()
    ones = (ones.astype(F32) * (1.0 / A_HEAD_DIM)).astype(BF16)
    const = lambda a: pl.BlockSpec(a.shape, lambda i, j: (0, 0))
    return pl.pallas_call(
        functools.partial(_qkv_group_kernel, dil=dil),
        grid=(t // tm, 3),
        in_specs=[pl.BlockSpec((tm, d), lambda i, j: (i, 0)),
                  pl.BlockSpec((1, d), lambda i, j: (0, 0)),
                  pl.BlockSpec((d, width), lambda i, j: (0, j * n_groups + group)),
                  pl.BlockSpec((tm, LANES), lambda i, j: (i, 0)),
                  pl.BlockSpec((tm, LANES), lambda i, j: (i, 0)),
                  const(gains), const(ones), const(perm)],
        out_specs=pl.BlockSpec((1, dil, tm // dil, width), lambda i, j: (i // tiles, 0, i % tiles, j)),
        out_shape=jax.ShapeDtypeStruct((batch, dil, s_len // dil, 3 * width), BF16),
        scratch_shapes=[pltpu.VMEM((tm, d), BF16), pltpu.VMEM((width // LANES, tm, LANES), F32)],
        compiler_params=_params("parallel", "arbitrary"),
        name=f"qkv_projection_g{group}",
    )(x, ln.reshape(1, d), wqkv, cs, sn, gains, ones, perm)


def dilated_group_attention(qkv, group, batch, s_len, heads):
    window, dil = DILATED_PAIRS[group]
    half = window // (2 * dil)
    seq = s_len // dil
    width = heads * A_HEAD_DIM
    qt = min(256, seq)
    kw = min(qt + 2 * half, seq)
    assert seq % qt == 0 and (seq - kw) % math.gcd(half, qt) == 0
    rb = max(1, min(dil, DIL_ATTN_ROWS // qt))
    assert dil % rb == 0
    whole = lambda c: pl.BlockSpec((None, rb, seq, width), lambda b, r, qi: (b, r, 0, c))
    tile = lambda n: pl.BlockSpec((None, rb, qt, n), lambda b, r, qi: (b, r, qi, 0))
    kern = functools.partial(_dil_attn_kernel, seq=seq, qt=qt, kw=kw, half=half, heads=heads)
    return pl.pallas_call(
        kern,
        grid=(batch, dil // rb, seq // qt),
        in_specs=[tile(width), whole(1), whole(2)],
        out_specs=[tile(width), tile(LANES)],
        out_shape=[jax.ShapeDtypeStruct((batch, dil, seq, width), BF16),
                   jax.ShapeDtypeStruct((batch, dil, seq, LANES), F32)],
        compiler_params=_params("parallel", "parallel", "parallel"),
        name=f"dilated_attention_g{group}",
    )(qkv, qkv, qkv)


def _combine_wo_kernel(x_ref, o0_ref, o1_ref, o2_ref, l0_ref, l1_ref, l2_ref, w_ref, out_ref, ot_ref, lt_ref, *, heads):
    hd = A_HEAD_DIM
    rep = LANES // heads
    tm = x_ref.shape[0]

    def token_order(src_ref, dst_ref, g):
        dil = src_ref.shape[0]
        if dil == 1:
            return src_ref[0].astype(F32)
        chunks = src_ref.shape[-1] // LANES
        for r in range(dil):
            blk = src_ref[r].astype(F32)
            for c in range(chunks):
                dst_ref[g, c, pl.ds(r, tm // dil, stride=dil), :] = blk[:, c * LANES:(c + 1) * LANES]
        return jnp.concatenate([dst_ref[g, c] for c in range(chunks)], axis=-1)

    ls = [token_order(ref, lt_ref, g) for g, ref in enumerate((l0_ref, l1_ref, l2_ref))]
    os_ = [token_order(ref, ot_ref, g) for g, ref in enumerate((o0_ref, o1_ref, o2_ref))]
    m = jnp.maximum(jnp.maximum(ls[0], ls[1]), ls[2])
    es = [jnp.exp(l - m) for l in ls]
    inv = 1.0 / (es[0] + es[1] + es[2])
    al = [e * inv for e in es]
    parts = []
    for h in range(heads):
        sl = slice(h * hd, (h + 1) * hd)
        c = slice(h * rep, h * rep + 1)
        parts.append((al[0][:, c] * os_[0][:, sl] + al[1][:, c] * os_[1][:, sl] + al[2][:, c] * os_[2][:, sl]).astype(BF16))
    o = jnp.concatenate(parts, axis=-1)
    out_ref[...] = x_ref[...] + jnp.dot(o, w_ref[...], preferred_element_type=F32)


def combine_wo(x, outs, lses, wo, heads, batch, s_len):
    t, d = x.shape
    width = wo.shape[0]
    n_groups = len(DILATED_PAIRS)
    tm = _row_tile(s_len, 512)
    tiles = s_len // tm
    assert all(tm % (16 * dil) == 0 for _, dil in DILATED_PAIRS)
    row = lambda n: pl.BlockSpec((tm, n), lambda i: (i, 0))
    res = lambda g, n: pl.BlockSpec((None, DILATED_PAIRS[g][1], tm // DILATED_PAIRS[g][1], n),
                                    lambda i: (i // tiles, 0, i % tiles, 0))
    return pl.pallas_call(
        functools.partial(_combine_wo_kernel, heads=heads),
        grid=(t // tm,),
        in_specs=[row(d)] + [res(g, width) for g in range(n_groups)] + [res(g, LANES) for g in range(n_groups)]
                 + [pl.BlockSpec((width, d), lambda i: (0, 0))],
        out_specs=row(d),
        out_shape=jax.ShapeDtypeStruct((t, d), F32),
        scratch_shapes=[pltpu.VMEM((n_groups, width // LANES, tm, LANES), F32),
                        pltpu.VMEM((n_groups, 1, tm, LANES), F32)],
        compiler_params=_params("parallel"),
        name="combine_wo",
    )(x, *outs, *lses, wo)


def mixer_a(x, ln, wqkv, qk_norm, wo, cs, sn, batch, s_len):
    heads = wo.shape[0] // A_HEAD_DIM
    wqkv = wqkv.astype(BF16)
    outs, lses = [], []
    for g in range(len(DILATED_PAIRS)):
        qkv = qkv_group_projection(x, ln, wqkv, cs, sn, qk_norm[0, g], qk_norm[1, g], g, batch, s_len, wo.shape[0])
        o, lse = dilated_group_attention(qkv, g, batch, s_len, heads)
        outs.append(o)
        lses.append(lse)
    return combine_wo(x, outs, lses, wo.astype(BF16), heads, batch, s_len)


def _ffn_kernel(x_ref, g_ref, w1_ref, w3_ref, w2_ref, o_ref, xn_ref, acc_ref):
    j = pl.program_id(1)

    @pl.when(j == 0)
    def _():
        xn_ref[...] = _rms(x_ref[...], g_ref[...]).astype(BF16)
        acc_ref[...] = jnp.zeros_like(acc_ref)

    acc_ref[...] += _swiglu_chunks(xn_ref[...], w1_ref, w3_ref, w2_ref)

    @pl.when(j == pl.num_programs(1) - 1)
    def _():
        o_ref[...] = x_ref[...] + acc_ref[...]


FFN_SUB = 256


def _swiglu_chunks(xn, w1_ref, w3_ref, w2_ref):
    tf = w2_ref.shape[0]
    assert tf % FFN_SUB == 0
    total = None
    for s in range(tf // FFN_SUB):
        sl = slice(s * FFN_SUB, (s + 1) * FFN_SUB)
        gate = jnp.dot(xn, w1_ref[:, sl], preferred_element_type=F32)
        up = jnp.dot(xn, w3_ref[:, sl], preferred_element_type=F32)
        act = (gate * jax.nn.sigmoid(gate) * up).astype(BF16)
        part = jnp.dot(act, w2_ref[sl, :], preferred_element_type=F32)
        total = part if total is None else total + part
    return total


def dense_ffn(x, ln, w13, w2):
    t, d = x.shape
    tf = dff = w2.shape[0]
    nj = dff // tf
    tm = _row_tile(t, 512)
    resident = dict(pipeline_mode=pl.Buffered(1))
    return pl.pallas_call(
        _ffn_kernel,
        grid=(t // tm, nj),
        in_specs=[pl.BlockSpec((tm, d), lambda i, j: (i, 0)),
                  pl.BlockSpec((1, d), lambda i, j: (0, 0)),
                  pl.BlockSpec((d, tf), lambda i, j: (0, j), **resident),
                  pl.BlockSpec((d, tf), lambda i, j: (0, nj + j), **resident),
                  pl.BlockSpec((tf, d), lambda i, j: (j, 0), **resident)],
        out_specs=pl.BlockSpec((tm, d), lambda i, j: (i, 0)),
        out_shape=jax.ShapeDtypeStruct((t, d), F32),
        scratch_shapes=[pltpu.VMEM((tm, d), BF16), pltpu.VMEM((tm, d), F32)],
        compiler_params=_params("parallel", "arbitrary"),
        name="dense_ffn",
    )(x, ln.reshape(1, d), w13, w13, w2)


C_ROPE = ROT_DIM
C_NOPE = 64
C_V = 64
C_QK = C_ROPE + C_NOPE
LOG2_E = math.log2(math.e)


def _mla_qkv_kernel(lat_ref, qa_ref, kva_ref, gq_ref, gk_ref, wq_ref, wkv_ref, cs_ref, sn_ref,
                    ones_ref, perm_ref, q_ref, k_ref, vt_ref, *, heads, q_lora, kv_lora):
    two = lambda a: jnp.concatenate([a, a], axis=1)
    cs, sn = two(cs_ref[...]), two(sn_ref[...])
    lat = lat_ref[...]
    qn = _rms(lat[:, :q_lora], qa_ref[...]).astype(BF16)
    kvn = _rms(lat[:, q_lora:q_lora + kv_lora], kva_ref[...]).astype(BF16)
    k_rope = two(lat[:, q_lora + kv_lora:])

    def head_norm(a, g, post):
        sumsq = jnp.dot((a * a).astype(BF16), ones_ref[...], preferred_element_type=F32)
        ag = a * g
        pair = jnp.dot(ag.astype(BF16), perm_ref[...], preferred_element_type=F32)
        return (ag * cs + pair * sn) * (lax.rsqrt(sumsq * (1.0 / C_QK) + NORM_EPS) * post)

    gq, gk = two(gq_ref[...]), two(gk_ref[...])
    kv = jnp.dot(kvn, wkv_ref[...], preferred_element_type=F32)
    for p in range(heads * LANES // A_PAIR):
        sl = slice(p * A_PAIR, (p + 1) * A_PAIR)
        q = jnp.dot(qn, wq_ref[:, sl], preferred_element_type=F32)
        q_ref[:, sl] = head_norm(q, gq, C_QK ** -0.5 * LOG2_E).astype(BF16)
        k_ref[:, sl] = head_norm(kv[:, sl] + k_rope, gk, 1.0).astype(BF16)
    vt_ref[...] = kv[:, heads * LANES:].T.astype(BF16)


def _mla_attn_kernel(q_ref, k_ref, vt_ref, o_ref, *, pack):
    ones = jnp.ones((16, k_ref.shape[1]), BF16)
    tq = q_ref.shape[1]
    halves = 2 if tq % (2 * LANES) == 0 else 1
    qw = tq // halves
    work = [(i, j) for i in range(pack) for j in range(halves)]
    st = [lax.dot_general(k_ref[0, :, i * LANES:(i + 1) * LANES], q_ref[0, j * qw:(j + 1) * qw, i * LANES:(i + 1) * LANES],
                          (((1,), (1,)), ((), ())), preferred_element_type=F32).astype(BF16) for i, j in work]
    m = [jnp.max(t, axis=0, keepdims=True) for t in st]
    p = [jnp.exp2(t - mm) for t, mm in zip(st, m)]
    vts = [jnp.concatenate([vt_ref[0, i * C_V:(i + 1) * C_V, :], ones], axis=0) for i in range(pack)]
    ol = [jnp.dot(vts[i], pp, preferred_element_type=F32) for (i, j), pp in zip(work, p)]
    for (i, j), t in zip(work, ol):
        o_ref[0, i * C_V:(i + 1) * C_V, j * qw:(j + 1) * qw] = (t[:C_V] / t[C_V:C_V + 1]).astype(o_ref.dtype)


def _matmul_res_t_kernel(at_ref, w_ref, x_ref, o_ref):
    o_ref[...] = x_ref[...] + lax.dot_general(at_ref[...], w_ref[...], (((0,), (0,)), ((), ())),
                                              preferred_element_type=F32)


def matmul_residual_t(a_t, w, x, batch, s_len):
    k, n = w.shape
    tm = _row_tile(s_len, 512)
    tiles = s_len // tm
    return pl.pallas_call(
        _matmul_res_t_kernel,
        grid=(batch * tiles,),
        in_specs=[pl.BlockSpec((None, k, tm), lambda i: (i // tiles, 0, i % tiles)),
                  pl.BlockSpec((k, n), lambda i: (0, 0)),
                  pl.BlockSpec((tm, n), lambda i: (i, 0))],
        out_specs=pl.BlockSpec((tm, n), lambda i: (i, 0)),
        out_shape=jax.ShapeDtypeStruct(x.shape, F32),
        compiler_params=_params("parallel"),
        name="matmul_residual_t",
    )(a_t, w, x)


def _pad_cols(w, heads, real, slot):
    k = w.shape[0]
    w = w.reshape(k, heads, real)
    return jnp.pad(w, ((0, 0), (0, 0), (0, slot - real))).reshape(k, heads * slot)


def mixer_c(x, ln, wa, qa_norm, kva_norm, wq_b, wkv_b, qk_norm, wo, cs, sn, batch, s_len):
    t, d = x.shape
    q_lora, kv_lora = qa_norm.shape[0], kva_norm.shape[0]
    heads = wo.shape[0] // C_V
    lat_w = q_lora + kv_lora + LANES
    wa_p = jnp.pad(wa, ((0, 0), (0, lat_w - wa.shape[1]))).astype(BF16)
    lat = norm_matmul(x, ln, wa_p, F32, tn=lat_w)
    wq_p = _pad_cols(wq_b, heads, C_QK, LANES).astype(BF16)
    wkv = wkv_b.reshape(kv_lora, heads, C_NOPE + C_V)
    wk_p = jnp.pad(wkv[:, :, :C_NOPE], ((0, 0), (0, 0), (C_ROPE, LANES - C_QK))).reshape(kv_lora, heads * LANES)
    wv = wkv[:, :, C_NOPE:].reshape(kv_lora, heads * C_V)
    wkv_p = jnp.concatenate([wk_p, wv], axis=1).astype(BF16)
    gq = jnp.pad(qk_norm[0], (0, LANES - C_QK)).reshape(1, LANES)
    gk = jnp.pad(qk_norm[1], (0, LANES - C_QK)).reshape(1, LANES)
    tm = _row_tile(s_len, 512)
    tiles = s_len // tm
    row = lambda n: pl.BlockSpec((tm, n), lambda i: (i, 0))
    full = lambda a: pl.BlockSpec(a.shape, lambda i: (0, 0))
    qa2, kva2 = qa_norm.reshape(1, -1), kva_norm.reshape(1, -1)
    ones, perm = _head_helper_matrices()
    q, k, vt = pl.pallas_call(
        functools.partial(_mla_qkv_kernel, heads=heads, q_lora=q_lora, kv_lora=kv_lora),
        grid=(t // tm,),
        in_specs=[row(lat_w), full(qa2), full(kva2), full(gq), full(gk), full(wq_p), full(wkv_p),
                  row(LANES), row(LANES), full(ones), full(perm)],
        out_specs=[row(heads * LANES), row(heads * LANES),
                   pl.BlockSpec((None, heads * C_V, tm), lambda i: (i // tiles, 0, i % tiles))],
        out_shape=[jax.ShapeDtypeStruct((t, heads * LANES), BF16),
                   jax.ShapeDtypeStruct((t, heads * LANES), BF16),
                   jax.ShapeDtypeStruct((batch, heads * C_V, s_len), BF16)],
        compiler_params=_params("parallel"),
        name="mla_qkv",
    )(lat, qa2, kva2, gq, gk, wq_p, wkv_p, cs, sn, ones, perm)
    pack = LANES // C_V
    tq = min(512, s_len)
    o_t = pl.pallas_call(
        functools.partial(_mla_attn_kernel, pack=pack),
        grid=(batch, heads // pack, s_len // tq),
        in_specs=[pl.BlockSpec((1, tq, pack * LANES), lambda b, h, i: (b, i, h)),
                  pl.BlockSpec((1, s_len, pack * LANES), lambda b, h, i: (b, 0, h)),
                  pl.BlockSpec((1, pack * C_V, s_len), lambda b, h, i: (b, h, 0))],
        out_specs=pl.BlockSpec((1, pack * C_V, tq), lambda b, h, i: (b, h, i)),
        out_shape=jax.ShapeDtypeStruct((batch, heads * C_V, s_len), BF16),
        compiler_params=_params("parallel", "parallel", "arbitrary"),
        name="mla_attention",
    )(q.reshape(batch, s_len, -1), k.reshape(batch, s_len, -1), vt)
    return matmul_residual_t(o_t, wo.astype(BF16), x, batch, s_len)


B_HEAD_DIM = 64
B_DECAY_SCALE = math.exp(-0.5)
B_GN_EPS = 64e-5
WKV_CHUNK = 64
WKV_HEADS_PER_STEP = 16
WKV_SEQS_PER_STEP = 1
WKV_PACK = 4


def _bdot(a, b):
    return jnp.dot(a.astype(BF16), b.astype(BF16), preferred_element_type=F32)


def _rwkv_prep_kernel(x_ref, xp_ref, xn_ref, ln_ref, mu_ref, wrkv_ref, g1_ref, g2_ref, w0_ref, w1_ref, w2_ref,
                      a0_ref, a1_ref, a2_ref, r_ref, k_ref, v_ref, g_ref, lw_ref, as_ref, *, tiles_per_seq):
    i = pl.program_id(0)
    tm = x_ref.shape[0]
    ln = ln_ref[...]
    h = _rms(x_ref[...], ln)
    pos = i % tiles_per_seq
    prev_row = jnp.where(pos == 0, 0.0, _rms(xp_ref[7:8, :], ln))
    next_row = jnp.where(pos == tiles_per_seq - 1, 0.0, _rms(xn_ref[0:1, :], ln))
    row = lax.broadcasted_iota(I32, (tm, 1), 0)
    d_f = jnp.where(row == 0, prev_row, pltpu.roll(h, 1, 0)) - h
    d_b = jnp.where(row == tm - 1, next_row, pltpu.roll(h, tm - 1, 0)) - h
    d_c = 0.5 * (d_f + d_b)
    mu = mu_ref[...]
    mix = lambda dd, n: h + dd * mu[n:n + 1]
    r_ref[...] = _bdot(mix(d_c, 0), wrkv_ref[0]).astype(r_ref.dtype)
    k_ref[...] = _bdot(mix(d_c, 1), wrkv_ref[1]).astype(k_ref.dtype)
    v_ref[...] = _bdot(mix(d_c, 2), wrkv_ref[2]).astype(v_ref.dtype)
    g_ref[...] = _bdot(jax.nn.sigmoid(_bdot(mix(d_c, 3), g1_ref[...])), g2_ref[...]).astype(g_ref.dtype)
    for dr, dd in enumerate((d_f, d_b)):
        z = w0_ref[dr:dr + 1, :] + _bdot(jnp.tanh(_bdot(mix(dd, 4 + dr), w1_ref[dr])), w2_ref[dr])
        lw_ref[dr] = -B_DECAY_SCALE * jax.nn.sigmoid(z)
        aa = a0_ref[dr:dr + 1, :] + _bdot(_bdot(mix(dd, 6 + dr), a1_ref[dr]), a2_ref[dr])
        as_ref[dr] = jax.nn.sigmoid(aa).astype(as_ref.dtype)


def _wkv_kernel(rf_ref, kf_ref, vf_ref, rb_ref, kb_ref, vb_ref, lwf_ref, lwb_ref, asf_ref, asb_ref, kk_ref, ka_ref,
                yf_ref, yb_ref, st_ref, *, hb):
    n = B_HEAD_DIM
    nb, c = rf_ref.shape[0], rf_ref.shape[1]

    @pl.when(pl.program_id(1) == 0)
    def _():
        st_ref[...] = jnp.zeros_like(st_ref)

    nt = (((1,), (1,)), ((), ()))
    tn = (((0,), (0,)), ((), ()))
    mm = (((1,), (0,)), ((), ()))
    low = lax.broadcasted_iota(I32, (1, LANES), 1) < n
    pk = WKV_PACK
    gw = pk * n
    per_dir = hb // pk
    gsl = [slice(g * gw, (g + 1) * gw) for g in range(per_dir)]
    cut = lambda a: [a[:, sl] for sl in gsl]
    row_p = lax.broadcasted_iota(I32, (c, gw), 0)
    col_p = lax.broadcasted_iota(I32, (c, gw), 1) % n

    def operands(sgn, bi, r_ref, k_ref, v_ref, lw_ref, as_ref):
        r_ref, k_ref, v_ref, lw_ref, as_ref = (ref.at[bi] for ref in (r_ref, k_ref, v_ref, lw_ref, as_ref))
        ahead = (lax.broadcasted_iota(I32, (c, c), 0) - lax.broadcasted_iota(I32, (c, c), 1)) * sgn
        lw = lw_ref[...]
        cl = jnp.dot((ahead >= 0).astype(F32), lw, preferred_element_type=F32, precision=lax.Precision.HIGHEST)
        total = jnp.sum(lw, axis=0, keepdims=True)
        e_in, e_ex, e_ng, e_end = jnp.exp(cl), jnp.exp(cl - lw), jnp.exp(-cl), jnp.exp(total - cl)
        rr, kk_raw, vv, asig = (ref[...].astype(F32) for ref in (r_ref, k_ref, v_ref, as_ref))
        kk_all = kk_raw * kk_ref[...]
        kdir = kk_raw * (1.0 + (asig - 1.0) * ka_ref[...])
        sq = kk_all * kk_all
        norms = []
        for p in range(sq.shape[1] // LANES):
            slab = sq[:, p * LANES:(p + 1) * LANES]
            s_lo = jnp.sum(jnp.where(low, slab, 0.0), axis=-1, keepdims=True)
            s_hi = jnp.sum(jnp.where(low, 0.0, slab), axis=-1, keepdims=True)
            norms.append(jnp.where(low, s_lo, s_hi))
        kkn_all = kk_all * lax.rsqrt(jnp.maximum(jnp.concatenate(norms, axis=-1), 1e-24))
        beta_all = kkn_all * asig
        ahead_p = (row_p - col_p) * sgn
        return dict(
            a_t=cut((-kkn_all * e_ex).astype(BF16)), b_t=cut((beta_all * e_ng).astype(BF16)),
            rt=cut((rr * e_in).astype(BF16)), kt=cut((kdir * e_ng).astype(BF16)), v=cut(vv.astype(BF16)),
            b_h=cut((beta_all * e_end).astype(BF16)), k_hb=cut((kdir * e_end).astype(BF16)),
            p_end=cut(jnp.exp(total)), strict=[ahead_p > 0] * per_dir, incl=[ahead_p >= 0] * per_dir)

    both = None
    for bi in range(nb):
        for part in (operands(1, bi, rf_ref, kf_ref, vf_ref, lwf_ref, asf_ref),
                     operands(-1, bi, rb_ref, kb_ref, vb_ref, lwb_ref, asb_ref)):
            both = part if both is None else {key: both[key] + part[key] for key in part}
    a_t, b_t, rt, kt, v, b_h, k_hb = (both[key] for key in ("a_t", "b_t", "rt", "kt", "v", "b_h", "k_hb"))
    p_end, strict_p, incl_p = both["p_end"], both["strict"], both["incl"]
    groups = range(nb * 2 * per_dir)

    diag_block = ((lax.broadcasted_iota(I32, (gw, gw), 0) // n)
                  == (lax.broadcasted_iota(I32, (gw, gw), 1) // n))
    diag_bf = diag_block.astype(BF16)

    def bd(p):
        return jnp.concatenate([p] * pk, axis=0) * diag_bf

    def fold(full):
        kept = jnp.where(diag_block, full, 0.0)
        out = kept[:n]
        for h in range(1, pk):
            out = out + kept[h * n:(h + 1) * n]
        return out

    eye_p = (row_p == col_p).astype(F32)
    same_block = lambda m: (row_p // m) == (col_p // m)
    dots = lambda a, b, dims: [lax.dot_general(p, q, dims, preferred_element_type=F32) for p, q in zip(a, b)]
    rows = lambda *parts: [jnp.concatenate(list(ps), axis=0) for ps in zip(*parts)]
    bf = lambda xs: [t.astype(BF16) for t in xs]
    bds = lambda xs: [bd(t) for t in xs]
    gram = dots(rows(a_t, rt), rows(bds(b_t), bds(kt)), nt)
    l_ab = [jnp.where(s, g[:c, :gw], 0.0) for s, g in zip(strict_p, gram)]
    l_ak = [jnp.where(s, g[:c, gw:], 0.0).astype(BF16) for s, g in zip(strict_p, gram)]
    m_r = [jnp.concatenate([jnp.where(i, g[c:, :gw], 0.0), jnp.where(i, g[c:, gw:], 0.0)],
                           axis=1).astype(BF16) for i, g in zip(incl_p, gram)]
    in_pair = same_block(2)
    x = [eye_p + jnp.where(in_pair, l, 0.0) for l in l_ab]
    m = 2
    while m < c:
        level = same_block(2 * m) & jnp.logical_not(same_block(m))
        xb = bf(x)
        l_m = [jnp.where(level, l, 0.0).astype(BF16) for l in l_ab]
        xl = bf(dots(xb, bds(l_m), mm))
        x = [xx + t for xx, t in zip(x, dots(xl, bds(xb), mm))]
        m *= 2
    s0 = [st_ref[g] for g in groups]
    bd_s = bds(bf(s0))
    bd_v = bds(v)
    w = [p + q for p, q in zip(dots(a_t, bd_s, nt), dots(l_ak, bd_v, mm))]
    u = bf(dots(bf(x), bds(bf(w)), mm))
    y = [p + q for p, q in zip(dots(rt, bd_s, nt), dots(m_r, rows(bds(u), bd_v), mm))]
    s_new = dots(rows(u, v), rows(b_h, k_hb), tn)
    for g in groups:
        st_ref[g] = s0[g] * p_end[g] + fold(s_new[g])
        bi, within = divmod(g, 2 * per_dir)
        out_ref = yf_ref if within < per_dir else yb_ref
        out_ref[bi, :, gsl[within % per_dir]] = y[g]


def _rwkv_out_kernel(x_ref, yf_ref, yb_ref, r_ref, k_ref, v_ref, g_ref, as_ref, ka_ref, rk_ref, lnx_ref, wo_ref,
                     sum_ref, o_ref):
    d = x_ref.shape[1]
    n = B_HEAD_DIM
    y = yf_ref[...] + yb_ref[...]
    a_sum = as_ref[0].astype(F32) + as_ref[1].astype(F32)
    kd = k_ref[...].astype(F32) * (2.0 + (a_sum - 2.0) * ka_ref[...])
    prod = r_ref[...].astype(F32) * kd * rk_ref[...]
    slab = sum_ref.shape[0]

    def head_sum(a):
        return jnp.dot(a.astype(BF16), sum_ref[...], preferred_element_type=F32)

    parts = []
    for p in range(d // slab):
        sl = slice(p * slab, (p + 1) * slab)
        ys = y[:, sl]
        cen = ys - head_sum(ys) * (1.0 / n)
        var = head_sum(cen * cen) * (1.0 / n)
        yn = cen * lax.rsqrt(var + B_GN_EPS) * lnx_ref[0:1, sl] + lnx_ref[1:2, sl]
        bonus = head_sum(prod[:, sl]) * v_ref[:, sl]
        parts.append(((yn + bonus) * g_ref[:, sl]).astype(BF16))
    o_ref[...] = x_ref[...] + jnp.dot(jnp.concatenate(parts, axis=-1), wo_ref[...], preferred_element_type=F32)


def mixer_b(x, ln, mu, w_rkv, w0, w1, w2, a0, a1, a2, g1, g2, k_k, k_a, r_k, ln_x, wo, batch, s_len):
    t, d = x.shape
    bf = lambda a: a.astype(BF16)
    halo = 8
    full = lambda a: pl.BlockSpec(a.shape, lambda i: (0,) * a.ndim)
    resident = lambda a: pl.BlockSpec(a.shape, lambda i: (0,) * a.ndim, pipeline_mode=pl.Buffered(1))
    ln2 = ln.reshape(1, d)
    weights = (ln2, mu, bf(w_rkv), bf(g1), bf(g2), w0, bf(w1), bf(w2), a0, bf(a1), bf(a2))
    tok = lambda dt: jax.ShapeDtypeStruct((t, d), dt)
    tok2 = lambda dt: jax.ShapeDtypeStruct((2, t, d), dt)
    tm = _row_tile(s_len, 512)
    tiles_per_seq = s_len // tm
    row = lambda n: pl.BlockSpec((tm, n), lambda i: (i, 0))
    row2 = lambda n: pl.BlockSpec((2, tm, n), lambda i: (0, i, 0))
    r, k, v, g, lw, asig = pl.pallas_call(
        functools.partial(_rwkv_prep_kernel, tiles_per_seq=tiles_per_seq),
        grid=(t // tm,),
        in_specs=[row(d),
                  pl.BlockSpec((halo, d), lambda i: (jnp.maximum(i * (tm // halo) - 1, 0), 0)),
                  pl.BlockSpec((halo, d), lambda i: (jnp.minimum((i + 1) * (tm // halo), t // halo - 1), 0))]
                 + [resident(a) for a in weights],
        out_specs=[row(d), row(d), row(d), row(d), row2(d), row2(d)],
        out_shape=[tok(BF16), tok(BF16), tok(BF16), tok(BF16), tok2(F32), tok2(BF16)],
        compiler_params=_params("parallel"),
        name="rwkv_prep",
    )(x, x, x, *weights)
    tm = _row_tile(s_len, 256)
    row = lambda n: pl.BlockSpec((tm, n), lambda i: (i, 0))
    row2 = lambda n: pl.BlockSpec((2, tm, n), lambda i: (0, i, 0))

    hb = WKV_HEADS_PER_STEP
    width = hb * B_HEAD_DIM
    c = min(WKV_CHUNK, s_len)
    nc = s_len // c
    assert width == d
    nb = math.gcd(batch, WKV_SEQS_PER_STEP)
    tok_f = pl.BlockSpec((nb, c, d), lambda b, ci: (b, ci, 0))
    tok_b = pl.BlockSpec((nb, c, d), lambda b, ci: (b, nc - 1 - ci, 0))
    dir_f = pl.BlockSpec((None, nb, c, d), lambda b, ci: (0, b, ci, 0))
    dir_b = pl.BlockSpec((None, nb, c, d), lambda b, ci: (1, b, nc - 1 - ci, 0))
    par = pl.BlockSpec((1, d), lambda b, ci: (0, 0))
    v3 = lambda a: a.reshape(batch, s_len, d)
    v4 = lambda a: a.reshape(2, batch, s_len, d)
    seq_out = jax.ShapeDtypeStruct((batch, s_len, d), F32)
    y_f, y_b = pl.pallas_call(
        functools.partial(_wkv_kernel, hb=hb),
        grid=(batch // nb, nc),
        in_specs=[tok_f, tok_f, tok_f, tok_b, tok_b, tok_b, dir_f, dir_b, dir_f, dir_b, par, par],
        out_specs=[tok_f, tok_b],
        out_shape=[seq_out, seq_out],
        scratch_shapes=[pltpu.VMEM((nb * 2 * hb // WKV_PACK, B_HEAD_DIM, WKV_PACK * B_HEAD_DIM), F32)],
        compiler_params=_params("parallel", "arbitrary"),
        name="wkv_scan",
    )(v3(r), v3(k), v3(v), v3(r), v3(k), v3(v), v4(lw), v4(lw), v4(asig), v4(asig),
      k_k.reshape(1, d), k_a.reshape(1, d))

    lane = jnp.arange(WKV_PACK * B_HEAD_DIM)
    same_head = ((lane[:, None] // B_HEAD_DIM) == (lane[None, :] // B_HEAD_DIM)).astype(BF16)
    params = (k_a.reshape(1, d), r_k.reshape(1, d), ln_x, bf(wo), same_head)
    return pl.pallas_call(
        _rwkv_out_kernel,
        grid=(t // tm,),
        in_specs=[row(d), row(d), row(d), row(d), row(d), row(d), row(d), row2(d)] + [full(a) for a in params],
        out_specs=row(d),
        out_shape=tok(F32),
        compiler_params=_params("parallel"),
        name="rwkv_out",
    )(x, y_f.reshape(t, d), y_b.reshape(t, d), r, k, v, g, asig, *params)


TOP_K = 2
ROUTE_EXPERT, ROUTE_GATE, ROUTE_RANK = 0, TOP_K, 2 * TOP_K
MOE_ROW_TILE = 512
ROW_DMA_UNROLL = 8


def _router_kernel(x_ref, g_ref, wr_ref, hn_ref, route_ref, cnt_ref, run_ref, *, n_exp):
    @pl.when(pl.program_id(0) == 0)
    def _():
        run_ref[...] = jnp.zeros_like(run_ref)

    tm = x_ref.shape[0]
    h = _rms(x_ref[...], g_ref[...])
    hn_ref[...] = h
    logits = jnp.dot(h, wr_ref[...], preferred_element_type=F32, precision=lax.Precision.HIGHEST)
    lane = lax.broadcasted_iota(I32, (tm, LANES), 1)
    logits = jnp.where(lane < n_exp, logits, NEG_INF)

    def top1(v):
        m = jnp.max(v, axis=-1, keepdims=True)
        return m, jnp.min(jnp.where(v == m, lane, LANES), axis=-1, keepdims=True)

    m1, i1 = top1(logits)
    m2, i2 = top1(jnp.where(lane == i1, NEG_INF, logits))
    e = jnp.exp(m2 - m1)
    g1 = 1.0 / (1.0 + e)
    g2 = e * g1
    sel = ((lane == i1) | (lane == i2)).astype(F32)
    earlier = (lax.broadcasted_iota(I32, (tm, tm), 0) > lax.broadcasted_iota(I32, (tm, tm), 1)).astype(BF16)
    before = jnp.dot(earlier, sel.astype(BF16), preferred_element_type=F32) + run_ref[...]
    r1 = jnp.sum(jnp.where(lane == i1, before, 0.0), axis=-1, keepdims=True)
    r2 = jnp.sum(jnp.where(lane == i2, before, 0.0), axis=-1, keepdims=True)
    run_ref[...] += jnp.sum(sel, axis=0, keepdims=True)
    cnt_ref[...] = run_ref[...]
    route = jnp.zeros((tm, LANES), F32)
    for off, vals in ((ROUTE_EXPERT, (i1.astype(F32), i2.astype(F32))), (ROUTE_GATE, (g1, g2)), (ROUTE_RANK, (r1, r2))):
        for kk, v in enumerate(vals):
            route = jnp.where(lane == off + kk, v, route)
    route_ref[...] = route


def _dispatch_kernel(dest_ref, hn_ref, xs_in_ref, xs_ref, sem):
    del xs_in_ref
    tm = hn_ref.shape[0]

    def row_copy(r, slot):
        return pltpu.make_async_copy(hn_ref.at[pl.ds(r, 1)], xs_ref.at[pl.ds(dest_ref[TOP_K * r + slot], 1)], sem)

    def start(r, c):
        for slot in range(TOP_K):
            row_copy(r, slot).start()
        return c

    def wait(r, c):
        for slot in range(TOP_K):
            row_copy(r, slot).wait()
        return c

    lax.fori_loop(0, tm, start, 0, unroll=ROW_DMA_UNROLL)
    lax.fori_loop(0, tm, wait, 0, unroll=ROW_DMA_UNROLL)


def _expert_ffn_kernel(te_ref, used_ref, x_ref, w1_ref, w3_ref, w2_ref, o_ref, xb_ref, acc_ref):
    del te_ref
    j = pl.program_id(1)
    last = j == pl.num_programs(1) - 1
    in_use = pl.program_id(0) < used_ref[0]

    @pl.when(in_use)
    def _():
        @pl.when(j == 0)
        def _():
            xb_ref[...] = x_ref[...].astype(BF16)
            acc_ref[...] = jnp.zeros_like(acc_ref)

        acc_ref[...] += _swiglu_chunks(xb_ref[...], w1_ref, w3_ref, w2_ref)

        @pl.when(last)
        def _():
            o_ref[...] = acc_ref[...]

    @pl.when(jnp.logical_not(in_use) & last)
    def _():
        o_ref[...] = jnp.zeros_like(o_ref)


def _moe_combine_kernel(dest_ref, route_ref, x_ref, ys_ref, o_ref, buf_ref, sem):
    tm = x_ref.shape[0]

    def row_copy(r, slot):
        return pltpu.make_async_copy(ys_ref.at[pl.ds(dest_ref[TOP_K * r + slot], 1)], buf_ref.at[slot, pl.ds(r, 1)], sem)

    def start(r, c):
        for slot in range(TOP_K):
            row_copy(r, slot).start()
        return c

    def wait(r, c):
        for slot in range(TOP_K):
            row_copy(r, slot).wait()
        return c

    lax.fori_loop(0, tm, start, 0, unroll=ROW_DMA_UNROLL)
    lax.fori_loop(0, tm, wait, 0, unroll=ROW_DMA_UNROLL)
    route = route_ref[...]
    out = x_ref[...]
    for slot in range(TOP_K):
        out = out + route[:, ROUTE_GATE + slot:ROUTE_GATE + slot + 1] * buf_ref[slot]
    o_ref[...] = out


def moe_ffn(x, ln, router, w13, w2, tf=1792):
    t, d = x.shape
    n_exp, dff = w2.shape[0], w2.shape[1]
    assert dff % tf == 0
    nj = dff // tf
    tmx = min(MOE_ROW_TILE, t)
    tm = _row_tile(t, 512)
    wr = jnp.pad(router, ((0, 0), (0, LANES - n_exp)))
    hn, route, cnt = pl.pallas_call(
        functools.partial(_router_kernel, n_exp=n_exp),
        grid=(t // tm,),
        in_specs=[pl.BlockSpec((tm, d), lambda i: (i, 0)),
                  pl.BlockSpec((1, d), lambda i: (0, 0)),
                  pl.BlockSpec((d, LANES), lambda i: (0, 0))],
        out_specs=[pl.BlockSpec((tm, d), lambda i: (i, 0)),
                   pl.BlockSpec((tm, LANES), lambda i: (i, 0)),
                   pl.BlockSpec((1, LANES), lambda i: (0, 0))],
        out_shape=[jax.ShapeDtypeStruct((t, d), F32),
                   jax.ShapeDtypeStruct((t, LANES), F32),
                   jax.ShapeDtypeStruct((1, LANES), F32)],
        scratch_shapes=[pltpu.VMEM((1, LANES), F32)],
        compiler_params=_params("arbitrary"),
        name="moe_router",
    )(x, ln.reshape(1, d), wr)

    expert = route[:, ROUTE_EXPERT:ROUTE_EXPERT + TOP_K].astype(I32)
    rank = route[:, ROUTE_RANK:ROUTE_RANK + TOP_K].astype(I32)
    counts = cnt[0, :n_exp].astype(I32)
    padded = (counts + tmx - 1) // tmx * tmx
    ends = jnp.cumsum(padded)
    dest = ((ends - padded)[expert] + rank).reshape(TOP_K * t)
    n_tiles = (TOP_K * t) // tmx + n_exp
    rows = n_tiles * tmx
    tile_start = jnp.arange(n_tiles, dtype=I32) * tmx
    tile_expert = jnp.minimum(jnp.sum((tile_start[:, None] >= ends[None, :]).astype(I32), axis=1), n_exp - 1)

    smem_rows = lambda n: pl.BlockSpec((TOP_K * n,), lambda i: (i,), memory_space=pltpu.SMEM)
    xs = pl.pallas_call(
        _dispatch_kernel,
        grid=(t // tm,),
        in_specs=[smem_rows(tm),
                  pl.BlockSpec((tm, d), lambda i: (i, 0)),
                  pl.BlockSpec(memory_space=pl.ANY)],
        out_specs=pl.BlockSpec(memory_space=pl.ANY),
        out_shape=jax.ShapeDtypeStruct((rows, d), F32),
        scratch_shapes=[pltpu.SemaphoreType.DMA(())],
        input_output_aliases={2: 0},
        compiler_params=_params("arbitrary"),
        name="moe_dispatch",
    )(dest, hn, jnp.zeros((rows, d), F32))

    ys = pl.pallas_call(
        _expert_ffn_kernel,
        grid_spec=pltpu.PrefetchScalarGridSpec(
            num_scalar_prefetch=2,
            grid=(n_tiles, nj),
            in_specs=[pl.BlockSpec((tmx, d), lambda i, j, te, nu: (i, 0)),
                      pl.BlockSpec((None, d, tf), lambda i, j, te, nu: (te[i], 0, j)),
                      pl.BlockSpec((None, d, tf), lambda i, j, te, nu: (te[i], 0, nj + j)),
                      pl.BlockSpec((None, tf, d), lambda i, j, te, nu: (te[i], j, 0))],
            out_specs=pl.BlockSpec((tmx, d), lambda i, j, te, nu: (i, 0)),
            scratch_shapes=[pltpu.VMEM((tmx, d), BF16), pltpu.VMEM((tmx, d), F32)]),
        out_shape=jax.ShapeDtypeStruct((rows, d), F32),
        compiler_params=_params("parallel", "arbitrary"),
        name="moe_expert_ffn",
    )(tile_expert, (ends[-1:] // tmx).astype(I32), xs, w13, w13, w2)

    return pl.pallas_call(
        _moe_combine_kernel,
        grid=(t // tm,),
        in_specs=[smem_rows(tm),
                  pl.BlockSpec((tm, LANES), lambda i: (i, 0)),
                  pl.BlockSpec((tm, d), lambda i: (i, 0)),
                  pl.BlockSpec(memory_space=pl.ANY)],
        out_specs=pl.BlockSpec((tm, d), lambda i: (i, 0)),
        out_shape=jax.ShapeDtypeStruct((t, d), F32),
        scratch_shapes=[pltpu.VMEM((TOP_K, tm, d), F32), pltpu.SemaphoreType.DMA(())],
        compiler_params=_params("arbitrary"),
        name="moe_combine",
    )(dest, route, x, ys)


def kernel(x, positions, ln_mix, ln_ffn, a_wqkv, a_qk_norm, a_wo, b_mu, b_wrkv, b_w0, b_w1, b_w2, b_a0, b_a1, b_a2, b_g1, b_g2, b_kk, b_ka, b_rk, b_lnx, b_wo, c_wa, c_qa_norm, c_kva_norm, c_wq_b, c_wkv_b, c_qk_norm, c_wo, f_w13, f_w2, m_router, m_w13, m_w2):
    batch, s_len, d = x.shape
    t = batch * s_len
    cs, sn = _rope_tables(positions)
    cs, sn = cs.reshape(t, LANES), sn.reshape(t, LANES)
    h = x.reshape(t, d)
    n_mixers = 3
    for i in range(ln_mix.shape[0]):
        j, kind = divmod(i, n_mixers)
        if kind == 0:
            h = mixer_a(h, ln_mix[i], a_wqkv[j], a_qk_norm[j], a_wo[j], cs, sn, batch, s_len)
        elif kind == 1:
            h = mixer_b(h, ln_mix[i], b_mu[j], b_wrkv[j], b_w0[j], b_w1[j], b_w2[j], b_a0[j], b_a1[j], b_a2[j],
                        b_g1[j], b_g2[j], b_kk[j], b_ka[j], b_rk[j], b_lnx[j], b_wo[j], batch, s_len)
        else:
            h = mixer_c(h, ln_mix[i], c_wa[j], c_qa_norm[j], c_kva_norm[j], c_wq_b[j], c_wkv_b[j], c_qk_norm[j],
                        c_wo[j], cs, sn, batch, s_len)
        f = i // 2
        if i % 2 == 0:
            h = dense_ffn(h, ln_ffn[i], f_w13[f].astype(BF16), f_w2[f].astype(BF16))
        else:
            h = moe_ffn(h, ln_ffn[i], m_router[f], m_w13[f].astype(BF16), m_w2[f].astype(BF16))
    return h.reshape(batch, s_len, d)
```

```python
import functools
import math

import jax
import jax.numpy as jnp
from jax import lax
from jax.experimental import pallas as pl
from jax.experimental.pallas import tpu as pltpu

F32 = jnp.float32
BF16 = jnp.bfloat16
I32 = jnp.int32

NORM_EPS = 1e-6
ROPE_THETA = 500000.0
NEG_INF = -1e30
LANES = 128
ROT_DIM = 32
ROT_HALF = ROT_DIM // 2
DILATED_PAIRS = ((128, 1), (512, 4), (2048, 16))
A_HEAD_DIM = 128
VMEM_LIMIT = 48 * 1024 * 1024


def _params(*sem):
    return pltpu.CompilerParams(dimension_semantics=sem, vmem_limit_bytes=VMEM_LIMIT)


def _rms(x, g):
    return x * lax.rsqrt(jnp.mean(x * x, axis=-1, keepdims=True) + NORM_EPS) * g


def _row_tile(t, want):
    tm = min(t, want)
    assert t % tm == 0
    return tm


def _norm_matmul_kernel(x_ref, g_ref, w_ref, o_ref, xn_ref):
    @pl.when(pl.program_id(1) == 0)
    def _():
        xn_ref[...] = _rms(x_ref[...], g_ref[...]).astype(BF16)

    o_ref[...] = jnp.dot(xn_ref[...], w_ref[...], preferred_element_type=F32).astype(o_ref.dtype)


def norm_matmul(x, g, w, out_dtype, tn):
    t, k = x.shape
    n = w.shape[1]
    tm = _row_tile(t, 1024)
    assert n % tn == 0
    return pl.pallas_call(
        _norm_matmul_kernel,
        grid=(t // tm, n // tn),
        in_specs=[pl.BlockSpec((tm, k), lambda i, j: (i, 0)),
                  pl.BlockSpec((1, k), lambda i, j: (0, 0)),
                  pl.BlockSpec((k, tn), lambda i, j: (0, j))],
        out_specs=pl.BlockSpec((tm, tn), lambda i, j: (i, j)),
        out_shape=jax.ShapeDtypeStruct((t, n), out_dtype),
        scratch_shapes=[pltpu.VMEM((tm, k), BF16)],
        compiler_params=_params("parallel", "arbitrary"),
        name="norm_matmul",
    )(x, g.reshape(1, k), w)


def _matmul_res_kernel(a_ref, w_ref, x_ref, o_ref):
    o_ref[...] = x_ref[...] + jnp.dot(a_ref[...], w_ref[...], preferred_element_type=F32)


def matmul_residual(a, w, x):
    t, k = a.shape
    n = w.shape[1]
    tm = _row_tile(t, 1024)
    return pl.pallas_call(
        _matmul_res_kernel,
        grid=(t // tm,),
        in_specs=[pl.BlockSpec((tm, k), lambda i: (i, 0)),
                  pl.BlockSpec((k, n), lambda i: (0, 0)),
                  pl.BlockSpec((tm, n), lambda i: (i, 0))],
        out_specs=pl.BlockSpec((tm, n), lambda i: (i, 0)),
        out_shape=jax.ShapeDtypeStruct((t, n), F32),
        compiler_params=_params("parallel"),
        name="matmul_residual",
    )(a, w, x)


def _rope_tables(positions):
    b, s = positions.shape
    inv_freq = ROPE_THETA ** (-jnp.arange(0, ROT_DIM, 2, dtype=F32) / ROT_DIM)
    ang = positions.astype(F32)[..., None] * inv_freq
    c, sn = jnp.cos(ang), jnp.sin(ang)
    rest = (b, s, LANES - ROT_DIM)
    cs_t = jnp.concatenate([c, c, jnp.ones(rest, F32)], axis=-1)
    sn_t = jnp.concatenate([-sn, sn, jnp.zeros(rest, F32)], axis=-1)
    return cs_t, sn_t


def _rope(x, cs, sn, lane):
    n = x.shape[-1]
    partner = jnp.where(lane < ROT_HALF, pltpu.roll(x, n - ROT_HALF, 1), pltpu.roll(x, ROT_HALF, 1))
    return x * cs + partner * sn


def _dil_attn_kernel(q_ref, k_ref, v_ref, o_ref, lse_ref, *, seq, qt, kw, half, heads):
    hd = A_HEAD_DIM
    qi = pl.program_id(2)
    q0 = pl.multiple_of(qi * qt, qt)
    start = jnp.clip(q0 - half, 0, seq - kw)
    start = pl.multiple_of(start, math.gcd(half, qt))
    qpos = q0 + lax.broadcasted_iota(I32, (qt, kw), 0)
    kpos = start + lax.broadcasted_iota(I32, (qt, kw), 1)
    valid = jnp.abs(qpos - kpos) <= half
    head_of_lane = lax.broadcasted_iota(I32, (1, LANES), 1) // (LANES // heads)
    ones = jnp.ones((kw, hd), BF16)
    work = [(r, slice(h * hd, (h + 1) * hd)) for r in range(q_ref.shape[0]) for h in range(heads)]
    s = [lax.dot_general(q_ref[r, :, sl], k_ref[r, pl.ds(start, kw), sl], (((1,), (1,)), ((), ())),
                         preferred_element_type=F32).astype(BF16) for r, sl in work]
    s = [jnp.where(valid, t, NEG_INF) for t in s]
    m = [jnp.max(t, axis=-1, keepdims=True) for t in s]
    p = [jnp.exp(t - mm) for t, mm in zip(s, m)]
    ol = [jnp.dot(pp, jnp.concatenate([v_ref[r, pl.ds(start, kw), sl], ones], axis=1), preferred_element_type=F32)
          for pp, (r, sl) in zip(p, work)]
    for r in range(q_ref.shape[0]):
        lse_row = jnp.zeros((qt, LANES), F32)
        for h in range(heads):
            i = r * heads + h
            o_ref[r, :, work[i][1]] = (ol[i][:, :hd] / ol[i][:, hd:]).astype(o_ref.dtype)
            lse_row = jnp.where(head_of_lane == h, m[i] + jnp.log(ol[i][:, hd:hd + 1]), lse_row)
        lse_ref[r] = lse_row


DIL_ATTN_ROWS = 512
A_PAIR = 2 * A_HEAD_DIM


def _head_helper_matrices():
    i = jnp.arange(A_PAIR)
    same_head = (i[:, None] // A_HEAD_DIM) == (i[None, :] // A_HEAD_DIM)
    dst = i % A_HEAD_DIM
    src = jnp.where(dst < ROT_HALF, i + ROT_HALF, i - ROT_HALF)
    perm = (i[:, None] == src[None, :]) & (dst[None, :] < ROT_DIM)
    return same_head.astype(BF16), perm.astype(BF16)


def _qkv_group_kernel(x_ref, g_ref, w_ref, cs_ref, sn_ref, gain_ref, mean_ref, perm_ref, o_ref, xn_ref, acc_ref, *, dil):
    j = pl.program_id(1)

    @pl.when(j == 0)
    def _():
        xn_ref[...] = _rms(x_ref[...], g_ref[...]).astype(BF16)

    acc = jnp.dot(xn_ref[...], w_ref[...], preferred_element_type=F32)
    tm, width = acc.shape

    def emit(vals):
        if dil == 1:
            o_ref[0, 0] = vals.astype(o_ref.dtype)
            return
        chunks = width // LANES
        for c in range(chunks):
            acc_ref[c] = vals[:, c * LANES:(c + 1) * LANES]
        for r in range(dil):
            o_ref[0, r] = jnp.concatenate([acc_ref[c, pl.ds(r, tm // dil, stride=dil), :] for c in range(chunks)],
                                          axis=-1).astype(o_ref.dtype)

    @pl.when(j < 2)
    def _():
        cs = jnp.concatenate([cs_ref[...]] * 2, axis=1)
        sn = jnp.concatenate([sn_ref[...]] * 2, axis=1)
        gain = gain_ref[pl.ds(j, 1), :]
        parts = []
        for p in range(width // A_PAIR):
            a = acc[:, p * A_PAIR:(p + 1) * A_PAIR]
            meansq = jnp.dot((a * a).astype(BF16), mean_ref[...], preferred_element_type=F32)
            ag = a * gain[:, p * A_PAIR:(p + 1) * A_PAIR]
            pair = jnp.dot(ag.astype(BF16), perm_ref[...], preferred_element_type=F32)
            parts.append((ag * cs + pair * sn) * lax.rsqrt(meansq + NORM_EPS))
        emit(jnp.concatenate(parts, axis=1))

    @pl.when(j == 2)
    def _():
        emit(acc)


def qkv_group_projection(x, ln, wqkv, cs, sn, gq, gk, group, batch, s_len, width):
    t, d = x.shape
    dil = DILATED_PAIRS[group][1]
    n_groups = len(DILATED_PAIRS)
    heads = width // A_HEAD_DIM
    tm = _row_tile(s_len, 1024)
    tiles = s_len // tm
    assert tm % (16 * dil) == 0
    gains = jnp.stack([jnp.tile(gq, heads) * A_HEAD_DIM ** -0.5, jnp.tile(gk, heads)])
    ones, perm = _head_helper_matrices()
    ones = (ones.astype(F32) * (1.0 / A_HEAD_DIM)).astype(BF16)
    const = lambda a: pl.BlockSpec(a.shape, lambda i, j: (0, 0))
    return pl.pallas_call(
        functools.partial(_qkv_group_kernel, dil=dil),
        grid=(t // tm, 3),
        in_specs=[pl.BlockSpec((tm, d), lambda i, j: (i, 0)),
                  pl.BlockSpec((1, d), lambda i, j: (0, 0)),
                  pl.BlockSpec((d, width), lambda i, j: (0, j * n_groups + group)),
                  pl.BlockSpec((tm, LANES), lambda i, j: (i, 0)),
                  pl.BlockSpec((tm, LANES), lambda i, j: (i, 0)),
                  const(gains), const(ones), const(perm)],
        out_specs=pl.BlockSpec((1, dil, tm // dil, width), lambda i, j: (i // tiles, 0, i % tiles, j)),
        out_shape=jax.ShapeDtypeStruct((batch, dil, s_len // dil, 3 * width), BF16),
        scratch_shapes=[pltpu.VMEM((tm, d), BF16), pltpu.VMEM((width // LANES, tm, LANES), F32)],
        compiler_params=_params("parallel", "arbitrary"),
        name=f"qkv_projection_g{group}",
    )(x, ln.reshape(1, d), wqkv, cs, sn, gains, ones, perm)


def dilated_group_attention(qkv, group, batch, s_len, heads):
    window, dil = DILATED_PAIRS[group]
    half = window // (2 * dil)
    seq = s_len // dil
    width = heads * A_HEAD_DIM
    qt = min(256, seq)
    kw = min(qt + 2 * half, seq)
    assert seq % qt == 0 and (seq - kw) % math.gcd(half, qt) == 0
    rb = max(1, min(dil, DIL_ATTN_ROWS // qt))
    assert dil % rb == 0
    whole = lambda c: pl.BlockSpec((None, rb, seq, width), lambda b, r, qi: (b, r, 0, c))
    tile = lambda n: pl.BlockSpec((None, rb, qt, n), lambda b, r, qi: (b, r, qi, 0))
    kern = functools.partial(_dil_attn_kernel, seq=seq, qt=qt, kw=kw, half=half, heads=heads)
    return pl.pallas_call(
        kern,
        grid=(batch, dil // rb, seq // qt),
        in_specs=[tile(width), whole(1), whole(2)],
        out_specs=[tile(width), tile(LANES)],
        out_shape=[jax.ShapeDtypeStruct((batch, dil, seq, width), BF16),
                   jax.ShapeDtypeStruct((batch, dil, seq, LANES), F32)],
        compiler_params=_params("parallel", "parallel", "parallel"),
        name=f"dilated_attention_g{group}",
    )(qkv, qkv, qkv)


def _combine_wo_kernel(x_ref, o0_ref, o1_ref, o2_ref, l0_ref, l1_ref, l2_ref, w_ref, out_ref, ot_ref, lt_ref, *, heads):
    hd = A_HEAD_DIM
    rep = LANES // heads
    tm = x_ref.shape[0]

    def token_order(src_ref, dst_ref, g):
        dil = src_ref.shape[0]
        if dil == 1:
            return src_ref[0].astype(F32)
        chunks = src_ref.shape[-1] // LANES
        for r in range(dil):
            blk = src_ref[r].astype(F32)
            for c in range(chunks):
                dst_ref[g, c, pl.ds(r, tm // dil, stride=dil), :] = blk[:, c * LANES:(c + 1) * LANES]
        return jnp.concatenate([dst_ref[g, c] for c in range(chunks)], axis=-1)

    ls = [token_order(ref, lt_ref, g) for g, ref in enumerate((l0_ref, l1_ref, l2_ref))]
    os_ = [token_order(ref, ot_ref, g) for g, ref in enumerate((o0_ref, o1_ref, o2_ref))]
    m = jnp.maximum(jnp.maximum(ls[0], ls[1]), ls[2])
    es = [jnp.exp(l - m) for l in ls]
    inv = 1.0 / (es[0] + es[1] + es[2])
    al = [e * inv for e in es]
    parts = []
    for h in range(heads):
        sl = slice(h * hd, (h + 1) * hd)
        c = slice(h * rep, h * rep + 1)
        parts.append((al[0][:, c] * os_[0][:, sl] + al[1][:, c] * os_[1][:, sl] + al[2][:, c] * os_[2][:, sl]).astype(BF16))
    o = jnp.concatenate(parts, axis=-1)
    out_ref[...] = x_ref[...] + jnp.dot(o, w_ref[...], preferred_element_type=F32)


def combine_wo(x, outs, lses, wo, heads, batch, s_len):
    t, d = x.shape
    width = wo.shape[0]
    n_groups = len(DILATED_PAIRS)
    tm = _row_tile(s_len, 512)
    tiles = s_len // tm
    assert all(tm % (16 * dil) == 0 for _, dil in DILATED_PAIRS)
    row = lambda n: pl.BlockSpec((tm, n), lambda i: (i, 0))
    res = lambda g, n: pl.BlockSpec((None, DILATED_PAIRS[g][1], tm // DILATED_PAIRS[g][1], n),
                                    lambda i: (i // tiles, 0, i % tiles, 0))
    return pl.pallas_call(
        functools.partial(_combine_wo_kernel, heads=heads),
        grid=(t // tm,),
        in_specs=[row(d)] + [res(g, width) for g in range(n_groups)] + [res(g, LANES) for g in range(n_groups)]
                 + [pl.BlockSpec((width, d), lambda i: (0, 0))],
        out_specs=row(d),
        out_shape=jax.ShapeDtypeStruct((t, d), F32),
        scratch_shapes=[pltpu.VMEM((n_groups, width // LANES, tm, LANES), F32),
                        pltpu.VMEM((n_groups, 1, tm, LANES), F32)],
        compiler_params=_params("parallel"),
        name="combine_wo",
    )(x, *outs, *lses, wo)


def mixer_a(x, ln, wqkv, qk_norm, wo, cs, sn, batch, s_len):
    heads = wo.shape[0] // A_HEAD_DIM
    wqkv = wqkv.astype(BF16)
    outs, lses = [], []
    for g in range(len(DILATED_PAIRS)):
        qkv = qkv_group_projection(x, ln, wqkv, cs, sn, qk_norm[0, g], qk_norm[1, g], g, batch, s_len, wo.shape[0])
        o, lse = dilated_group_attention(qkv, g, batch, s_len, heads)
        outs.append(o)
        lses.append(lse)
    return combine_wo(x, outs, lses, wo.astype(BF16), heads, batch, s_len)


def _ffn_kernel(x_ref, g_ref, w1_ref, w3_ref, w2_ref, o_ref, xn_ref, acc_ref):
    j = pl.program_id(1)

    @pl.when(j == 0)
    def _():
        xn_ref[...] = _rms(x_ref[...], g_ref[...]).astype(BF16)
        acc_ref[...] = jnp.zeros_like(acc_ref)

    acc_ref[...] += _swiglu_chunks(xn_ref[...], w1_ref, w3_ref, w2_ref)

    @pl.when(j == pl.num_programs(1) - 1)
    def _():
        o_ref[...] = x_ref[...] + acc_ref[...]


FFN_SUB = 256


def _swiglu_chunks(xn, w1_ref, w3_ref, w2_ref):
    tf = w2_ref.shape[0]
    assert tf % FFN_SUB == 0
    total = None
    for s in range(tf // FFN_SUB):
        sl = slice(s * FFN_SUB, (s + 1) * FFN_SUB)
        gate = jnp.dot(xn, w1_ref[:, sl], preferred_element_type=F32)
        up = jnp.dot(xn, w3_ref[:, sl], preferred_element_type=F32)
        act = (gate * jax.nn.sigmoid(gate) * up).astype(BF16)
        part = jnp.dot(act, w2_ref[sl, :], preferred_element_type=F32)
        total = part if total is None else total + part
    return total


def dense_ffn(x, ln, w13, w2):
    t, d = x.shape
    tf = dff = w2.shape[0]
    nj = dff // tf
    tm = _row_tile(t, 512)
    resident = dict(pipeline_mode=pl.Buffered(1))
    return pl.pallas_call(
        _ffn_kernel,
        grid=(t // tm, nj),
        in_specs=[pl.BlockSpec((tm, d), lambda i, j: (i, 0)),
                  pl.BlockSpec((1, d), lambda i, j: (0, 0)),
                  pl.BlockSpec((d, tf), lambda i, j: (0, j), **resident),
                  pl.BlockSpec((d, tf), lambda i, j: (0, nj + j), **resident),
                  pl.BlockSpec((tf, d), lambda i, j: (j, 0), **resident)],
        out_specs=pl.BlockSpec((tm, d), lambda i, j: (i, 0)),
        out_shape=jax.ShapeDtypeStruct((t, d), F32),
        scratch_shapes=[pltpu.VMEM((tm, d), BF16), pltpu.VMEM((tm, d), F32)],
        compiler_params=_params("parallel", "arbitrary"),
        name="dense_ffn",
    )(x, ln.reshape(1, d), w13, w13, w2)


C_ROPE = ROT_DIM
C_NOPE = 64
C_V = 64
C_QK = C_ROPE + C_NOPE
LOG2_E = math.log2(math.e)


def _mla_qkv_kernel(lat_ref, qa_ref, kva_ref, gq_ref, gk_ref, wq_ref, wkv_ref, cs_ref, sn_ref,
                    ones_ref, perm_ref, q_ref, k_ref, vt_ref, *, heads, q_lora, kv_lora):
    two = lambda a: jnp.concatenate([a, a], axis=1)
    cs, sn = two(cs_ref[...]), two(sn_ref[...])
    lat = lat_ref[...]
    qn = _rms(lat[:, :q_lora], qa_ref[...]).astype(BF16)
    kvn = _rms(lat[:, q_lora:q_lora + kv_lora], kva_ref[...]).astype(BF16)
    k_rope = two(lat[:, q_lora + kv_lora:])

    def head_norm(a, g, post):
        sumsq = jnp.dot((a * a).astype(BF16), ones_ref[...], preferred_element_type=F32)
        ag = a * g
        pair = jnp.dot(ag.astype(BF16), perm_ref[...], preferred_element_type=F32)
        return (ag * cs + pair * sn) * (lax.rsqrt(sumsq * (1.0 / C_QK) + NORM_EPS) * post)

    gq, gk = two(gq_ref[...]), two(gk_ref[...])
    kv = jnp.dot(kvn, wkv_ref[...], preferred_element_type=F32)
    for p in range(heads * LANES // A_PAIR):
        sl = slice(p * A_PAIR, (p + 1) * A_PAIR)
        q = jnp.dot(qn, wq_ref[:, sl], preferred_element_type=F32)
        q_ref[:, sl] = head_norm(q, gq, C_QK ** -0.5 * LOG2_E).astype(BF16)
        k_ref[:, sl] = head_norm(kv[:, sl] + k_rope, gk, 1.0).astype(BF16)
    vt_ref[...] = kv[:, heads * LANES:].T.astype(BF16)


def _mla_attn_kernel(q_ref, k_ref, vt_ref, o_ref, *, pack):
    ones = jnp.ones((16, k_ref.shape[1]), BF16)
    tq = q_ref.shape[1]
    halves = 2 if tq % (2 * LANES) == 0 else 1
    qw = tq // halves
    work = [(i, j) for i in range(pack) for j in range(halves)]
    st = [lax.dot_general(k_ref[0, :, i * LANES:(i + 1) * LANES], q_ref[0, j * qw:(j + 1) * qw, i * LANES:(i + 1) * LANES],
                          (((1,), (1,)), ((), ())), preferred_element_type=F32).astype(BF16) for i, j in work]
    m = [jnp.max(t, axis=0, keepdims=True) for t in st]
    p = [jnp.exp2(t - mm) for t, mm in zip(st, m)]
    vts = [jnp.concatenate([vt_ref[0, i * C_V:(i + 1) * C_V, :], ones], axis=0) for i in range(pack)]
    ol = [jnp.dot(vts[i], pp, preferred_element_type=F32) for (i, j), pp in zip(work, p)]
    for (i, j), t in zip(work, ol):
        o_ref[0, i * C_V:(i + 1) * C_V, j * qw:(j + 1) * qw] = (t[:C_V] / t[C_V:C_V + 1]).astype(o_ref.dtype)


def _matmul_res_t_kernel(at_ref, w_ref, x_ref, o_ref):
    o_ref[...] = x_ref[...] + lax.dot_general(at_ref[...], w_ref[...], (((0,), (0,)), ((), ())),
                                              preferred_element_type=F32)


def matmul_residual_t(a_t, w, x, batch, s_len):
    k, n = w.shape
    tm = _row_tile(s_len, 512)
    tiles = s_len // tm
    return pl.pallas_call(
        _matmul_res_t_kernel,
        grid=(batch * tiles,),
        in_specs=[pl.BlockSpec((None, k, tm), lambda i: (i // tiles, 0, i % tiles)),
                  pl.BlockSpec((k, n), lambda i: (0, 0)),
                  pl.BlockSpec((tm, n), lambda i: (i, 0))],
        out_specs=pl.BlockSpec((tm, n), lambda i: (i, 0)),
        out_shape=jax.ShapeDtypeStruct(x.shape, F32),
        compiler_params=_params("parallel"),
        name="matmul_residual_t",
    )(a_t, w, x)


def _pad_cols(w, heads, real, slot):
    k = w.shape[0]
    w = w.reshape(k, heads, real)
    return jnp.pad(w, ((0, 0), (0, 0), (0, slot - real))).reshape(k, heads * slot)


def mixer_c(x, ln, wa, qa_norm, kva_norm, wq_b, wkv_b, qk_norm, wo, cs, sn, batch, s_len):
    t, d = x.shape
    q_lora, kv_lora = qa_norm.shape[0], kva_norm.shape[0]
    heads = wo.shape[0] // C_V
    lat_w = q_lora + kv_lora + LANES
    wa_p = jnp.pad(wa, ((0, 0), (0, lat_w - wa.shape[1]))).astype(BF16)
    lat = norm_matmul(x, ln, wa_p, F32, tn=lat_w)
    wq_p = _pad_cols(wq_b, heads, C_QK, LANES).astype(BF16)
    wkv = wkv_b.reshape(kv_lora, heads, C_NOPE + C_V)
    wk_p = jnp.pad(wkv[:, :, :C_NOPE], ((0, 0), (0, 0), (C_ROPE, LANES - C_QK))).reshape(kv_lora, heads * LANES)
    wv = wkv[:, :, C_NOPE:].reshape(kv_lora, heads * C_V)
    wkv_p = jnp.concatenate([wk_p, wv], axis=1).astype(BF16)
    gq = jnp.pad(qk_norm[0], (0, LANES - C_QK)).reshape(1, LANES)
    gk = jnp.pad(qk_norm[1], (0, LANES - C_QK)).reshape(1, LANES)
    tm = _row_tile(s_len, 512)
    tiles = s_len // tm
    row = lambda n: pl.BlockSpec((tm, n), lambda i: (i, 0))
    full = lambda a: pl.BlockSpec(a.shape, lambda i: (0, 0))
    qa2, kva2 = qa_norm.reshape(1, -1), kva_norm.reshape(1, -1)
    ones, perm = _head_helper_matrices()
    q, k, vt = pl.pallas_call(
        functools.partial(_mla_qkv_kernel, heads=heads, q_lora=q_lora, kv_lora=kv_lora),
        grid=(t // tm,),
        in_specs=[row(lat_w), full(qa2), full(kva2), full(gq), full(gk), full(wq_p), full(wkv_p),
                  row(LANES), row(LANES), full(ones), full(perm)],
        out_specs=[row(heads * LANES), row(heads * LANES),
                   pl.BlockSpec((None, heads * C_V, tm), lambda i: (i // tiles, 0, i % tiles))],
        out_shape=[jax.ShapeDtypeStruct((t, heads * LANES), BF16),
                   jax.ShapeDtypeStruct((t, heads * LANES), BF16),
                   jax.ShapeDtypeStruct((batch, heads * C_V, s_len), BF16)],
        compiler_params=_params("parallel"),
        name="mla_qkv",
    )(lat, qa2, kva2, gq, gk, wq_p, wkv_p, cs, sn, ones, perm)
    pack = LANES // C_V
    tq = min(512, s_len)
    o_t = pl.pallas_call(
        functools.partial(_mla_attn_kernel, pack=pack),
        grid=(batch, heads // pack, s_len // tq),
        in_specs=[pl.BlockSpec((1, tq, pack * LANES), lambda b, h, i: (b, i, h)),
                  pl.BlockSpec((1, s_len, pack * LANES), lambda b, h, i: (b, 0, h)),
                  pl.BlockSpec((1, pack * C_V, s_len), lambda b, h, i: (b, h, 0))],
        out_specs=pl.BlockSpec((1, pack * C_V, tq), lambda b, h, i: (b, h, i)),
        out_shape=jax.ShapeDtypeStruct((batch, heads * C_V, s_len), BF16),
        compiler_params=_params("parallel", "parallel", "arbitrary"),
        name="mla_attention",
    )(q.reshape(batch, s_len, -1), k.reshape(batch, s_len, -1), vt)
    return matmul_residual_t(o_t, wo.astype(BF16), x, batch, s_len)


B_HEAD_DIM = 64
B_DECAY_SCALE = math.exp(-0.5)
B_GN_EPS = 64e-5
WKV_CHUNK = 64
WKV_HEADS_PER_STEP = 16
WKV_SEQS_PER_STEP = 1
WKV_PACK = 4


def _bdot(a, b):
    return jnp.dot(a.astype(BF16), b.astype(BF16), preferred_element_type=F32)


def _rwkv_prep_kernel(x_ref, xp_ref, xn_ref, ln_ref, mu_ref, wrkv_ref, g1_ref, g2_ref, w0_ref, w1_ref, w2_ref,
                      a0_ref, a1_ref, a2_ref, r_ref, k_ref, v_ref, g_ref, lw_ref, as_ref, *, tiles_per_seq):
    i = pl.program_id(0)
    tm = x_ref.shape[0]
    ln = ln_ref[...]
    h = _rms(x_ref[...], ln)
    pos = i % tiles_per_seq
    prev_row = jnp.where(pos == 0, 0.0, _rms(xp_ref[7:8, :], ln))
    next_row = jnp.where(pos == tiles_per_seq - 1, 0.0, _rms(xn_ref[0:1, :], ln))
    row = lax.broadcasted_iota(I32, (tm, 1), 0)
    d_f = jnp.where(row == 0, prev_row, pltpu.roll(h, 1, 0)) - h
    d_b = jnp.where(row == tm - 1, next_row, pltpu.roll(h, tm - 1, 0)) - h
    d_c = 0.5 * (d_f + d_b)
    mu = mu_ref[...]
    mix = lambda dd, n: h + dd * mu[n:n + 1]
    r_ref[...] = _bdot(mix(d_c, 0), wrkv_ref[0]).astype(r_ref.dtype)
    k_ref[...] = _bdot(mix(d_c, 1), wrkv_ref[1]).astype(k_ref.dtype)
    v_ref[...] = _bdot(mix(d_c, 2), wrkv_ref[2]).astype(v_ref.dtype)
    g_ref[...] = _bdot(jax.nn.sigmoid(_bdot(mix(d_c, 3), g1_ref[...])), g2_ref[...]).astype(g_ref.dtype)
    for dr, dd in enumerate((d_f, d_b)):
        z = w0_ref[dr:dr + 1, :] + _bdot(jnp.tanh(_bdot(mix(dd, 4 + dr), w1_ref[dr])), w2_ref[dr])
        lw_ref[dr] = -B_DECAY_SCALE * jax.nn.sigmoid(z)
        aa = a0_ref[dr:dr + 1, :] + _bdot(_bdot(mix(dd, 6 + dr), a1_ref[dr]), a2_ref[dr])
        as_ref[dr] = jax.nn.sigmoid(aa).astype(as_ref.dtype)


def _wkv_kernel(rf_ref, kf_ref, vf_ref, rb_ref, kb_ref, vb_ref, lwf_ref, lwb_ref, asf_ref, asb_ref, kk_ref, ka_ref,
                yf_ref, yb_ref, st_ref, *, hb):
    n = B_HEAD_DIM
    nb, c = rf_ref.shape[0], rf_ref.shape[1]

    @pl.when(pl.program_id(1) == 0)
    def _():
        st_ref[...] = jnp.zeros_like(st_ref)

    nt = (((1,), (1,)), ((), ()))
    tn = (((0,), (0,)), ((), ()))
    mm = (((1,), (0,)), ((), ()))
    low = lax.broadcasted_iota(I32, (1, LANES), 1) < n
    pk = WKV_PACK
    gw = pk * n
    per_dir = hb // pk
    gsl = [slice(g * gw, (g + 1) * gw) for g in range(per_dir)]
    cut = lambda a: [a[:, sl] for sl in gsl]
    row_p = lax.broadcasted_iota(I32, (c, gw), 0)
    col_p = lax.broadcasted_iota(I32, (c, gw), 1) % n

    def operands(sgn, bi, r_ref, k_ref, v_ref, lw_ref, as_ref):
        r_ref, k_ref, v_ref, lw_ref, as_ref = (ref.at[bi] for ref in (r_ref, k_ref, v_ref, lw_ref, as_ref))
        ahead = (lax.broadcasted_iota(I32, (c, c), 0) - lax.broadcasted_iota(I32, (c, c), 1)) * sgn
        lw = lw_ref[...]
        cl = jnp.dot((ahead >= 0).astype(F32), lw, preferred_element_type=F32, precision=lax.Precision.HIGHEST)
        total = jnp.sum(lw, axis=0, keepdims=True)
        e_in, e_ex, e_ng, e_end = jnp.exp(cl), jnp.exp(cl - lw), jnp.exp(-cl), jnp.exp(total - cl)
        rr, kk_raw, vv, asig = (ref[...].astype(F32) for ref in (r_ref, k_ref, v_ref, as_ref))
        kk_all = kk_raw * kk_ref[...]
        kdir = kk_raw * (1.0 + (asig - 1.0) * ka_ref[...])
        sq = kk_all * kk_all
        norms = []
        for p in range(sq.shape[1] // LANES):
            slab = sq[:, p * LANES:(p + 1) * LANES]
            s_lo = jnp.sum(jnp.where(low, slab, 0.0), axis=-1, keepdims=True)
            s_hi = jnp.sum(jnp.where(low, 0.0, slab), axis=-1, keepdims=True)
            norms.append(jnp.where(low, s_lo, s_hi))
        kkn_all = kk_all * lax.rsqrt(jnp.maximum(jnp.concatenate(norms, axis=-1), 1e-24))
        beta_all = kkn_all * asig
        ahead_p = (row_p - col_p) * sgn
        return dict(
            a_t=cut((-kkn_all * e_ex).astype(BF16)), b_t=cut((beta_all * e_ng).astype(BF16)),
            rt=cut((rr * e_in).astype(BF16)), kt=cut((kdir * e_ng).astype(BF16)), v=cut(vv.astype(BF16)),
            b_h=cut((beta_all * e_end).astype(BF16)), k_hb=cut((kdir * e_end).astype(BF16)),
            p_end=cut(jnp.exp(total)), strict=[ahead_p > 0] * per_dir, incl=[ahead_p >= 0] * per_dir)

    both = None
    for bi in range(nb):
        for part in (operands(1, bi, rf_ref, kf_ref, vf_ref, lwf_ref, asf_ref),
                     operands(-1, bi, rb_ref, kb_ref, vb_ref, lwb_ref, asb_ref)):
            both = part if both is None else {key: both[key] + part[key] for key in part}
    a_t, b_t, rt, kt, v, b_h, k_hb = (both[key] for key in ("a_t", "b_t", "rt", "kt", "v", "b_h", "k_hb"))
    p_end, strict_p, incl_p = both["p_end"], both["strict"], both["incl"]
    groups = range(nb * 2 * per_dir)

    diag_block = ((lax.broadcasted_iota(I32, (gw, gw), 0) // n)
                  == (lax.broadcasted_iota(I32, (gw, gw), 1) // n))
    diag_bf = diag_block.astype(BF16)

    def bd(p):
        return jnp.concatenate([p] * pk, axis=0) * diag_bf

    def fold(full):
        kept = jnp.where(diag_block, full, 0.0)
        out = kept[:n]
        for h in range(1, pk):
            out = out + kept[h * n:(h + 1) * n]
        return out

    eye_p = (row_p == col_p).astype(F32)
    same_block = lambda m: (row_p // m) == (col_p // m)
    dots = lambda a, b, dims: [lax.dot_general(p, q, dims, preferred_element_type=F32) for p, q in zip(a, b)]
    rows = lambda *parts: [jnp.concatenate(list(ps), axis=0) for ps in zip(*parts)]
    bf = lambda xs: [t.astype(BF16) for t in xs]
    bds = lambda xs: [bd(t) for t in xs]
    gram = dots(rows(a_t, rt), rows(bds(b_t), bds(kt)), nt)
    l_ab = [jnp.where(s, g[:c, :gw], 0.0) for s, g in zip(strict_p, gram)]
    l_ak = [jnp.where(s, g[:c, gw:], 0.0).astype(BF16) for s, g in zip(strict_p, gram)]
    m_r = [jnp.concatenate([jnp.where(i, g[c:, :gw], 0.0), jnp.where(i, g[c:, gw:], 0.0)],
                           axis=1).astype(BF16) for i, g in zip(incl_p, gram)]
    in_pair = same_block(2)
    x = [eye_p + jnp.where(in_pair, l, 0.0) for l in l_ab]
    m = 2
    while m < c:
        level = same_block(2 * m) & jnp.logical_not(same_block(m))
        xb = bf(x)
        l_m = [jnp.where(level, l, 0.0).astype(BF16) for l in l_ab]
        xl = bf(dots(xb, bds(l_m), mm))
        x = [xx + t for xx, t in zip(x, dots(xl, bds(xb), mm))]
        m *= 2
    s0 = [st_ref[g] for g in groups]
    bd_s = bds(bf(s0))
    bd_v = bds(v)
    w = [p + q for p, q in zip(dots(a_t, bd_s, nt), dots(l_ak, bd_v, mm))]
    u = bf(dots(bf(x), bds(bf(w)), mm))
    y = [p + q for p, q in zip(dots(rt, bd_s, nt), dots(m_r, rows(bds(u), bd_v), mm))]
    s_new = dots(rows(u, v), rows(b_h, k_hb), tn)
    for g in groups:
        st_ref[g] = s0[g] * p_end[g] + fold(s_new[g])
        bi, within = divmod(g, 2 * per_dir)
        out_ref = yf_ref if within < per_dir else yb_ref
        out_ref[bi, :, gsl[within % per_dir]] = y[g]


def _rwkv_out_kernel(x_ref, yf_ref, yb_ref, r_ref, k_ref, v_ref, g_ref, as_ref, ka_ref, rk_ref, lnx_ref, wo_ref,
                     sum_ref, o_ref):
    d = x_ref.shape[1]
    n = B_HEAD_DIM
    y = yf_ref[...] + yb_ref[...]
    a_sum = as_ref[0].astype(F32) + as_ref[1].astype(F32)
    kd = k_ref[...].astype(F32) * (2.0 + (a_sum - 2.0) * ka_ref[...])
    prod = r_ref[...].astype(F32) * kd * rk_ref[...]
    slab = sum_ref.shape[0]

    def head_sum(a):
        return jnp.dot(a.astype(BF16), sum_ref[...], preferred_element_type=F32)

    parts = []
    for p in range(d // slab):
        sl = slice(p * slab, (p + 1) * slab)
        ys = y[:, sl]
        cen = ys - head_sum(ys) * (1.0 / n)
        var = head_sum(cen * cen) * (1.0 / n)
        yn = cen * lax.rsqrt(var + B_GN_EPS) * lnx_ref[0:1, sl] + lnx_ref[1:2, sl]
        bonus = head_sum(prod[:, sl]) * v_ref[:, sl]
        parts.append(((yn + bonus) * g_ref[:, sl]).astype(BF16))
    o_ref[...] = x_ref[...] + jnp.dot(jnp.concatenate(parts, axis=-1), wo_ref[...], preferred_element_type=F32)


def mixer_b(x, ln, mu, w_rkv, w0, w1, w2, a0, a1, a2, g1, g2, k_k, k_a, r_k, ln_x, wo, batch, s_len):
    t, d = x.shape
    bf = lambda a: a.astype(BF16)
    halo = 8
    full = lambda a: pl.BlockSpec(a.shape, lambda i: (0,) * a.ndim)
    resident = lambda a: pl.BlockSpec(a.shape, lambda i: (0,) * a.ndim, pipeline_mode=pl.Buffered(1))
    ln2 = ln.reshape(1, d)
    weights = (ln2, mu, bf(w_rkv), bf(g1), bf(g2), w0, bf(w1), bf(w2), a0, bf(a1), bf(a2))
    tok = lambda dt: jax.ShapeDtypeStruct((t, d), dt)
    tok2 = lambda dt: jax.ShapeDtypeStruct((2, t, d), dt)
    tm = _row_tile(s_len, 512)
    tiles_per_seq = s_len // tm
    row = lambda n: pl.BlockSpec((tm, n), lambda i: (i, 0))
    row2 = lambda n: pl.BlockSpec((2, tm, n), lambda i: (0, i, 0))
    r, k, v, g, lw, asig = pl.pallas_call(
        functools.partial(_rwkv_prep_kernel, tiles_per_seq=tiles_per_seq),
        grid=(t // tm,),
        in_specs=[row(d),
                  pl.BlockSpec((halo, d), lambda i: (jnp.maximum(i * (tm // halo) - 1, 0), 0)),
                  pl.BlockSpec((halo, d), lambda i: (jnp.minimum((i + 1) * (tm // halo), t // halo - 1), 0))]
                 + [resident(a) for a in weights],
        out_specs=[row(d), row(d), row(d), row(d), row2(d), row2(d)],
        out_shape=[tok(BF16), tok(BF16), tok(BF16), tok(BF16), tok2(F32), tok2(BF16)],
        compiler_params=_params("parallel"),
        name="rwkv_prep",
    )(x, x, x, *weights)
    tm = _row_tile(s_len, 256)
    row = lambda n: pl.BlockSpec((tm, n), lambda i: (i, 0))
    row2 = lambda n: pl.BlockSpec((2, tm, n), lambda i: (0, i, 0))

    hb = WKV_HEADS_PER_STEP
    width = hb * B_HEAD_DIM
    c = min(WKV_CHUNK, s_len)
    nc = s_len // c
    assert width == d
    nb = math.gcd(batch, WKV_SEQS_PER_STEP)
    tok_f = pl.BlockSpec((nb, c, d), lambda b, ci: (b, ci, 0))
    tok_b = pl.BlockSpec((nb, c, d), lambda b, ci: (b, nc - 1 - ci, 0))
    dir_f = pl.BlockSpec((None, nb, c, d), lambda b, ci: (0, b, ci, 0))
    dir_b = pl.BlockSpec((None, nb, c, d), lambda b, ci: (1, b, nc - 1 - ci, 0))
    par = pl.BlockSpec((1, d), lambda b, ci: (0, 0))
    v3 = lambda a: a.reshape(batch, s_len, d)
    v4 = lambda a: a.reshape(2, batch, s_len, d)
    seq_out = jax.ShapeDtypeStruct((batch, s_len, d), F32)
    y_f, y_b = pl.pallas_call(
        functools.partial(_wkv_kernel, hb=hb),
        grid=(batch // nb, nc),
        in_specs=[tok_f, tok_f, tok_f, tok_b, tok_b, tok_b, dir_f, dir_b, dir_f, dir_b, par, par],
        out_specs=[tok_f, tok_b],
        out_shape=[seq_out, seq_out],
        scratch_shapes=[pltpu.VMEM((nb * 2 * hb // WKV_PACK, B_HEAD_DIM, WKV_PACK * B_HEAD_DIM), F32)],
        compiler_params=_params("parallel", "arbitrary"),
        name="wkv_scan",
    )(v3(r), v3(k), v3(v), v3(r), v3(k), v3(v), v4(lw), v4(lw), v4(asig), v4(asig),
      k_k.reshape(1, d), k_a.reshape(1, d))

    lane = jnp.arange(WKV_PACK * B_HEAD_DIM)
    same_head = ((lane[:, None] // B_HEAD_DIM) == (lane[None, :] // B_HEAD_DIM)).astype(BF16)
    params = (k_a.reshape(1, d), r_k.reshape(1, d), ln_x, bf(wo), same_head)
    return pl.pallas_call(
        _rwkv_out_kernel,
        grid=(t // tm,),
        in_specs=[row(d), row(d), row(d), row(d), row(d), row(d), row(d), row2(d)] + [full(a) for a in params],
        out_specs=row(d),
        out_shape=tok(F32),
        compiler_params=_params("parallel"),
        name="rwkv_out",
    )(x, y_f.reshape(t, d), y_b.reshape(t, d), r, k, v, g, asig, *params)


TOP_K = 2
ROUTE_EXPERT, ROUTE_GATE, ROUTE_RANK = 0, TOP_K, 2 * TOP_K
MOE_ROW_TILE = 512
ROW_DMA_UNROLL = 8


def _to_token_tiles(ref, vals):
    n, d = vals.shape
    per = d // LANES
    assert per % 8 == 0
    for s in range(per):
        ref[pl.ds(s, n, stride=per), :] = vals[:, s * LANES:(s + 1) * LANES]


def _from_token_tiles(ref, n):
    per = ref.shape[0] // n
    return jnp.concatenate([ref[pl.ds(s, n, stride=per), :] for s in range(per)], axis=1)


def _router_kernel(x_ref, g_ref, wr_ref, hn_ref, route_ref, cnt_ref, run_ref, *, n_exp):
    @pl.when(pl.program_id(0) == 0)
    def _():
        run_ref[...] = jnp.zeros_like(run_ref)

    tm = x_ref.shape[0]
    h = _rms(x_ref[...], g_ref[...])
    _to_token_tiles(hn_ref, h)
    logits = jnp.dot(h, wr_ref[...], preferred_element_type=F32, precision=lax.Precision.HIGHEST)
    lane = lax.broadcasted_iota(I32, (tm, LANES), 1)
    logits = jnp.where(lane < n_exp, logits, NEG_INF)

    def top1(v):
        m = jnp.max(v, axis=-1, keepdims=True)
        return m, jnp.min(jnp.where(v == m, lane, LANES), axis=-1, keepdims=True)

    m1, i1 = top1(logits)
    m2, i2 = top1(jnp.where(lane == i1, NEG_INF, logits))
    e = jnp.exp(m2 - m1)
    g1 = 1.0 / (1.0 + e)
    g2 = e * g1
    sel = ((lane == i1) | (lane == i2)).astype(F32)
    earlier = (lax.broadcasted_iota(I32, (tm, tm), 0) > lax.broadcasted_iota(I32, (tm, tm), 1)).astype(BF16)
    before = jnp.dot(earlier, sel.astype(BF16), preferred_element_type=F32) + run_ref[...]
    r1 = jnp.sum(jnp.where(lane == i1, before, 0.0), axis=-1, keepdims=True)
    r2 = jnp.sum(jnp.where(lane == i2, before, 0.0), axis=-1, keepdims=True)
    run_ref[...] += jnp.sum(sel, axis=0, keepdims=True)
    cnt_ref[...] = run_ref[...]
    route = jnp.zeros((tm, LANES), F32)
    for off, vals in ((ROUTE_EXPERT, (i1.astype(F32), i2.astype(F32))), (ROUTE_GATE, (g1, g2)), (ROUTE_RANK, (r1, r2))):
        for kk, v in enumerate(vals):
            route = jnp.where(lane == off + kk, v, route)
    route_ref[...] = route


def _dispatch_kernel(dest_ref, hn_ref, xs_in_ref, xs_ref, sem, *, per):
    del xs_in_ref
    tm = hn_ref.shape[0] // per

    def row_copy(r, slot):
        src = pl.multiple_of(r * per, per)
        dst = pl.multiple_of(dest_ref[TOP_K * r + slot] * per, per)
        return pltpu.make_async_copy(hn_ref.at[pl.ds(src, per)], xs_ref.at[pl.ds(dst, per)], sem)

    def start(r, c):
        for slot in range(TOP_K):
            row_copy(r, slot).start()
        return c

    def wait(r, c):
        for slot in range(TOP_K):
            row_copy(r, slot).wait()
        return c

    lax.fori_loop(0, tm, start, 0, unroll=ROW_DMA_UNROLL)
    lax.fori_loop(0, tm, wait, 0, unroll=ROW_DMA_UNROLL)


def _expert_ffn_kernel(te_ref, used_ref, x_ref, w1_ref, w3_ref, w2_ref, o_ref, xb_ref, acc_ref):
    del te_ref
    j = pl.program_id(1)
    last = j == pl.num_programs(1) - 1
    in_use = pl.program_id(0) < used_ref[0]

    @pl.when(in_use)
    def _():
        @pl.when(j == 0)
        def _():
            xb_ref[...] = _from_token_tiles(x_ref, xb_ref.shape[0]).astype(BF16)
            acc_ref[...] = jnp.zeros_like(acc_ref)

        acc_ref[...] += _swiglu_chunks(xb_ref[...], w1_ref, w3_ref, w2_ref)

        @pl.when(last)
        def _():
            _to_token_tiles(o_ref, acc_ref[...])

    @pl.when(jnp.logical_not(in_use) & last)
    def _():
        o_ref[...] = jnp.zeros_like(o_ref)


def _moe_combine_kernel(dest_ref, route_ref, x_ref, ys_ref, o_ref, buf_ref, sem):
    tm = x_ref.shape[0]
    per = buf_ref.shape[1] // tm

    def row_copy(r, slot):
        src = pl.multiple_of(dest_ref[TOP_K * r + slot] * per, per)
        dst = pl.multiple_of(r * per, per)
        return pltpu.make_async_copy(ys_ref.at[pl.ds(src, per)], buf_ref.at[slot, pl.ds(dst, per)], sem)

    def start(r, c):
        for slot in range(TOP_K):
            row_copy(r, slot).start()
        return c

    def wait(r, c):
        for slot in range(TOP_K):
            row_copy(r, slot).wait()
        return c

    lax.fori_loop(0, tm, start, 0, unroll=ROW_DMA_UNROLL)
    lax.fori_loop(0, tm, wait, 0, unroll=ROW_DMA_UNROLL)
    route = route_ref[...]
    out = x_ref[...]
    for slot in range(TOP_K):
        out = out + route[:, ROUTE_GATE + slot:ROUTE_GATE + slot + 1] * _from_token_tiles(buf_ref.at[slot], tm)
    o_ref[...] = out


def moe_ffn(x, ln, router, w13, w2, tf=1792):
    t, d = x.shape
    n_exp, dff = w2.shape[0], w2.shape[1]
    assert dff % tf == 0
    nj = dff // tf
    tmx = min(MOE_ROW_TILE, t)
    tm = _row_tile(t, 512)
    per = d // LANES
    wr = jnp.pad(router, ((0, 0), (0, LANES - n_exp)))
    hn, route, cnt = pl.pallas_call(
        functools.partial(_router_kernel, n_exp=n_exp),
        grid=(t // tm,),
        in_specs=[pl.BlockSpec((tm, d), lambda i: (i, 0)),
                  pl.BlockSpec((1, d), lambda i: (0, 0)),
                  pl.BlockSpec((d, LANES), lambda i: (0, 0))],
        out_specs=[pl.BlockSpec((tm * per, LANES), lambda i: (i, 0)),
                   pl.BlockSpec((tm, LANES), lambda i: (i, 0)),
                   pl.BlockSpec((1, LANES), lambda i: (0, 0))],
        out_shape=[jax.ShapeDtypeStruct((t * per, LANES), F32),
                   jax.ShapeDtypeStruct((t, LANES), F32),
                   jax.ShapeDtypeStruct((1, LANES), F32)],
        scratch_shapes=[pltpu.VMEM((1, LANES), F32)],
        compiler_params=_params("arbitrary"),
        name="moe_router",
    )(x, ln.reshape(1, d), wr)

    expert = route[:, ROUTE_EXPERT:ROUTE_EXPERT + TOP_K].astype(I32)
    rank = route[:, ROUTE_RANK:ROUTE_RANK + TOP_K].astype(I32)
    counts = cnt[0, :n_exp].astype(I32)
    padded = (counts + tmx - 1) // tmx * tmx
    ends = jnp.cumsum(padded)
    dest = ((ends - padded)[expert] + rank).reshape(TOP_K * t)
    n_tiles = (TOP_K * t) // tmx + n_exp
    rows = n_tiles * tmx
    tile_start = jnp.arange(n_tiles, dtype=I32) * tmx
    tile_expert = jnp.minimum(jnp.sum((tile_start[:, None] >= ends[None, :]).astype(I32), axis=1), n_exp - 1)

    smem_rows = lambda n: pl.BlockSpec((TOP_K * n,), lambda i: (i,), memory_space=pltpu.SMEM)
    xs = pl.pallas_call(
        functools.partial(_dispatch_kernel, per=per),
        grid=(t // tm,),
        in_specs=[smem_rows(tm),
                  pl.BlockSpec((tm * per, LANES), lambda i: (i, 0)),
                  pl.BlockSpec(memory_space=pl.ANY)],
        out_specs=pl.BlockSpec(memory_space=pl.ANY),
        out_shape=jax.ShapeDtypeStruct((rows * per, LANES), F32),
        scratch_shapes=[pltpu.SemaphoreType.DMA(())],
        input_output_aliases={2: 0},
        compiler_params=_params("arbitrary"),
        name="moe_dispatch",
    )(dest, hn, jnp.zeros((rows * per, LANES), F32))

    ys = pl.pallas_call(
        _expert_ffn_kernel,
        grid_spec=pltpu.PrefetchScalarGridSpec(
            num_scalar_prefetch=2,
            grid=(n_tiles, nj),
            in_specs=[pl.BlockSpec((tmx * per, LANES), lambda i, j, te, nu: (i, 0)),
                      pl.BlockSpec((None, d, tf), lambda i, j, te, nu: (te[i], 0, j)),
                      pl.BlockSpec((None, d, tf), lambda i, j, te, nu: (te[i], 0, nj + j)),
                      pl.BlockSpec((None, tf, d), lambda i, j, te, nu: (te[i], j, 0))],
            out_specs=pl.BlockSpec((tmx * per, LANES), lambda i, j, te, nu: (i, 0)),
            scratch_shapes=[pltpu.VMEM((tmx, d), BF16), pltpu.VMEM((tmx, d), F32)]),
        out_shape=jax.ShapeDtypeStruct((rows * per, LANES), F32),
        compiler_params=_params("parallel", "arbitrary"),
        name="moe_expert_ffn",
    )(tile_expert, (ends[-1:] // tmx).astype(I32), xs, w13, w13, w2)

    return pl.pallas_call(
        _moe_combine_kernel,
        grid=(t // tm,),
        in_specs=[smem_rows(tm),
                  pl.BlockSpec((tm, LANES), lambda i: (i, 0)),
                  pl.BlockSpec((tm, d), lambda i: (i, 0)),
                  pl.BlockSpec(memory_space=pl.ANY)],
        out_specs=pl.BlockSpec((tm, d), lambda i: (i, 0)),
        out_shape=jax.ShapeDtypeStruct((t, d), F32),
        scratch_shapes=[pltpu.VMEM((TOP_K, tm * per, LANES), F32), pltpu.SemaphoreType.DMA(())],
        compiler_params=_params("arbitrary"),
        name="moe_combine",
    )(dest, route, x, ys)


def kernel(x, positions, ln_mix, ln_ffn, a_wqkv, a_qk_norm, a_wo, b_mu, b_wrkv, b_w0, b_w1, b_w2, b_a0, b_a1, b_a2, b_g1, b_g2, b_kk, b_ka, b_rk, b_lnx, b_wo, c_wa, c_qa_norm, c_kva_norm, c_wq_b, c_wkv_b, c_qk_norm, c_wo, f_w13, f_w2, m_router, m_w13, m_w2):
    batch, s_len, d = x.shape
    t = batch * s_len
    cs, sn = _rope_tables(positions)
    cs, sn = cs.reshape(t, LANES), sn.reshape(t, LANES)
    h = x.reshape(t, d)
    n_mixers = 3
    for i in range(ln_mix.shape[0]):
        j, kind = divmod(i, n_mixers)
        if kind == 0:
            h = mixer_a(h, ln_mix[i], a_wqkv[j], a_qk_norm[j], a_wo[j], cs, sn, batch, s_len)
        elif kind == 1:
            h = mixer_b(h, ln_mix[i], b_mu[j], b_wrkv[j], b_w0[j], b_w1[j], b_w2[j], b_a0[j], b_a1[j], b_a2[j],
                        b_g1[j], b_g2[j], b_kk[j], b_ka[j], b_rk[j], b_lnx[j], b_wo[j], batch, s_len)
        else:
            h = mixer_c(h, ln_mix[i], c_wa[j], c_qa_norm[j], c_kva_norm[j], c_wq_b[j], c_wkv_b[j], c_qk_norm[j],
                        c_wo[j], cs, sn, batch, s_len)
        f = i // 2
        if i % 2 == 0:
            h = dense_ffn(h, ln_ffn[i], f_w13[f].astype(BF16), f_w2[f].astype(BF16))
        else:
            h = moe_ffn(h, ln_ffn[i], m_router[f], m_w13[f].astype(BF16), m_w2[f].astype(BF16))
    return h.reshape(batch, s_len, d)
```

```python
import functools
import math

import jax
import jax.numpy as jnp
from jax import lax
from jax.experimental import pallas as pl
from jax.experimental.pallas import tpu as pltpu

F32 = jnp.float32
BF16 = jnp.bfloat16
I32 = jnp.int32

NORM_EPS = 1e-6
ROPE_THETA = 500000.0
NEG_INF = -1e30
LANES = 128
ROT_DIM = 32
ROT_HALF = ROT_DIM // 2
DILATED_PAIRS = ((128, 1), (512, 4), (2048, 16))
A_HEAD_DIM = 128
VMEM_LIMIT = 48 * 1024 * 1024


def _params(*sem):
    return pltpu.CompilerParams(dimension_semantics=sem, vmem_limit_bytes=VMEM_LIMIT)


def _rms(x, g):
    return x * lax.rsqrt(jnp.mean(x * x, axis=-1, keepdims=True) + NORM_EPS) * g


def _row_tile(t, want):
    tm = min(t, want)
    assert t % tm == 0
    return tm


def _norm_matmul_kernel(x_ref, g_ref, w_ref, o_ref, xn_ref):
    @pl.when(pl.program_id(1) == 0)
    def _():
        xn_ref[...] = _rms(x_ref[...], g_ref[...]).astype(BF16)

    o_ref[...] = jnp.dot(xn_ref[...], w_ref[...], preferred_element_type=F32).astype(o_ref.dtype)


def norm_matmul(x, g, w, out_dtype, tn):
    t, k = x.shape
    n = w.shape[1]
    tm = _row_tile(t, 1024)
    assert n % tn == 0
    return pl.pallas_call(
        _norm_matmul_kernel,
        grid=(t // tm, n // tn),
        in_specs=[pl.BlockSpec((tm, k), lambda i, j: (i, 0)),
                  pl.BlockSpec((1, k), lambda i, j: (0, 0)),
                  pl.BlockSpec((k, tn), lambda i, j: (0, j))],
        out_specs=pl.BlockSpec((tm, tn), lambda i, j: (i, j)),
        out_shape=jax.ShapeDtypeStruct((t, n), out_dtype),
        scratch_shapes=[pltpu.VMEM((tm, k), BF16)],
        compiler_params=_params("parallel", "arbitrary"),
        name="norm_matmul",
    )(x, g.reshape(1, k), w)


def _matmul_res_kernel(a_ref, w_ref, x_ref, o_ref):
    o_ref[...] = x_ref[...] + jnp.dot(a_ref[...], w_ref[...], preferred_element_type=F32)


def matmul_residual(a, w, x):
    t, k = a.shape
    n = w.shape[1]
    tm = _row_tile(t, 1024)
    return pl.pallas_call(
        _matmul_res_kernel,
        grid=(t // tm,),
        in_specs=[pl.BlockSpec((tm, k), lambda i: (i, 0)),
                  pl.BlockSpec((k, n), lambda i: (0, 0)),
                  pl.BlockSpec((tm, n), lambda i: (i, 0))],
        out_specs=pl.BlockSpec((tm, n), lambda i: (i, 0)),
        out_shape=jax.ShapeDtypeStruct((t, n), F32),
        compiler_params=_params("parallel"),
        name="matmul_residual",
    )(a, w, x)


def _rope_tables(positions):
    b, s = positions.shape
    inv_freq = ROPE_THETA ** (-jnp.arange(0, ROT_DIM, 2, dtype=F32) / ROT_DIM)
    ang = positions.astype(F32)[..., None] * inv_freq
    c, sn = jnp.cos(ang), jnp.sin(ang)
    rest = (b, s, LANES - ROT_DIM)
    cs_t = jnp.concatenate([c, c, jnp.ones(rest, F32)], axis=-1)
    sn_t = jnp.concatenate([-sn, sn, jnp.zeros(rest, F32)], axis=-1)
    return cs_t, sn_t


def _rope(x, cs, sn, lane):
    n = x.shape[-1]
    partner = jnp.where(lane < ROT_HALF, pltpu.roll(x, n - ROT_HALF, 1), pltpu.roll(x, ROT_HALF, 1))
    return x * cs + partner * sn


def _dil_attn_kernel(q_ref, k_ref, v_ref, o_ref, lse_ref, *, seq, qt, kw, half, heads):
    hd = A_HEAD_DIM
    qi = pl.program_id(2)
    q0 = pl.multiple_of(qi * qt, qt)
    start = jnp.clip(q0 - half, 0, seq - kw)
    start = pl.multiple_of(start, math.gcd(half, qt))
    qpos = q0 + lax.broadcasted_iota(I32, (qt, kw), 0)
    kpos = start + lax.broadcasted_iota(I32, (qt, kw), 1)
    valid = jnp.abs(qpos - kpos) <= half
    head_of_lane = lax.broadcasted_iota(I32, (1, LANES), 1) // (LANES // heads)
    ones = jnp.ones((kw, hd), BF16)
    work = [(r, slice(h * hd, (h + 1) * hd)) for r in range(q_ref.shape[0]) for h in range(heads)]
    s = [lax.dot_general(q_ref[r, :, sl], k_ref[r, pl.ds(start, kw), sl], (((1,), (1,)), ((), ())),
                         preferred_element_type=F32).astype(BF16) for r, sl in work]
    s = [jnp.where(valid, t, NEG_INF) for t in s]
    m = [jnp.max(t, axis=-1, keepdims=True) for t in s]
    p = [jnp.exp(t - mm) for t, mm in zip(s, m)]
    ol = [jnp.dot(pp, jnp.concatenate([v_ref[r, pl.ds(start, kw), sl], ones], axis=1), preferred_element_type=F32)
          for pp, (r, sl) in zip(p, work)]
    for r in range(q_ref.shape[0]):
        lse_row = jnp.zeros((qt, LANES), F32)
        for h in range(heads):
            i = r * heads + h
            o_ref[r, :, work[i][1]] = (ol[i][:, :hd] / ol[i][:, hd:]).astype(o_ref.dtype)
            lse_row = jnp.where(head_of_lane == h, m[i] + jnp.log(ol[i][:, hd:hd + 1]), lse_row)
        lse_ref[r] = lse_row


DIL_ATTN_ROWS = 512
A_PAIR = 2 * A_HEAD_DIM


def _head_helper_matrices():
    i = jnp.arange(A_PAIR)
    same_head = (i[:, None] // A_HEAD_DIM) == (i[None, :] // A_HEAD_DIM)
    dst = i % A_HEAD_DIM
    src = jnp.where(dst < ROT_HALF, i + ROT_HALF, i - ROT_HALF)
    perm = (i[:, None] == src[None, :]) & (dst[None, :] < ROT_DIM)
    return same_head.astype(BF16), perm.astype(BF16)


def _qkv_group_kernel(x_ref, g_ref, w_ref, cs_ref, sn_ref, gain_ref, mean_ref, perm_ref, o_ref, xn_ref, acc_ref, *, dil):
    j = pl.program_id(1)

    @pl.when(j == 0)
    def _():
        xn_ref[...] = _rms(x_ref[...], g_ref[...]).astype(BF16)

    acc = jnp.dot(xn_ref[...], w_ref[...], preferred_element_type=F32)
    tm, width = acc.shape

    def emit(vals):
        if dil == 1:
            o_ref[0, 0] = vals.astype(o_ref.dtype)
            return
        chunks = width // LANES
        for c in range(chunks):
            acc_ref[c] = vals[:, c * LANES:(c + 1) * LANES]
        for r in range(dil):
            o_ref[0, r] = jnp.concatenate([acc_ref[c, pl.ds(r, tm // dil, stride=dil), :] for c in range(chunks)],
                                          axis=-1).astype(o_ref.dtype)

    @pl.when(j < 2)
    def _():
        cs = jnp.concatenate([cs_ref[...]] * 2, axis=1)
        sn = jnp.concatenate([sn_ref[...]] * 2, axis=1)
        gain = gain_ref[pl.ds(j, 1), :]
        parts = []
        for p in range(width // A_PAIR):
            a = acc[:, p * A_PAIR:(p + 1) * A_PAIR]
            meansq = jnp.dot((a * a).astype(BF16), mean_ref[...], preferred_element_type=F32)
            ag = a * gain[:, p * A_PAIR:(p + 1) * A_PAIR]
            pair = jnp.dot(ag.astype(BF16), perm_ref[...], preferred_element_type=F32)
            parts.append((ag * cs + pair * sn) * lax.rsqrt(meansq + NORM_EPS))
        emit(jnp.concatenate(parts, axis=1))

    @pl.when(j == 2)
    def _():
        emit(acc)


def qkv_group_projection(x, ln, wqkv, cs, sn, gq, gk, group, batch, s_len, width):
    t, d = x.shape
    dil = DILATED_PAIRS[group][1]
    n_groups = len(DILATED_PAIRS)
    heads = width // A_HEAD_DIM
    tm = _row_tile(s_len, 1024)
    tiles = s_len // tm
    assert tm % (16 * dil) == 0
    gains = jnp.stack([jnp.tile(gq, heads) * A_HEAD_DIM ** -0.5, jnp.tile(gk, heads)])
    ones, perm = _head_helper_matrices()
    ones = (ones.astype(F32) * (1.0 / A_HEAD_DIM)).astype(BF16)
    const = lambda a: pl.BlockSpec(a.shape, lambda i, j: (0, 0))
    return pl.pallas_call(
        functools.partial(_qkv_group_kernel, dil=dil),
        grid=(t // tm, 3),
        in_specs=[pl.BlockSpec((tm, d), lambda i, j: (i, 0)),
                  pl.BlockSpec((1, d), lambda i, j: (0, 0)),
                  pl.BlockSpec((d, width), lambda i, j: (0, j * n_groups + group)),
                  pl.BlockSpec((tm, LANES), lambda i, j: (i, 0)),
                  pl.BlockSpec((tm, LANES), lambda i, j: (i, 0)),
                  const(gains), const(ones), const(perm)],
        out_specs=pl.BlockSpec((1, dil, tm // dil, width), lambda i, j: (i // tiles, 0, i % tiles, j)),
        out_shape=jax.ShapeDtypeStruct((batch, dil, s_len // dil, 3 * width), BF16),
        scratch_shapes=[pltpu.VMEM((tm, d), BF16), pltpu.VMEM((width // LANES, tm, LANES), F32)],
        compiler_params=_params("parallel", "arbitrary"),
        name=f"qkv_projection_g{group}",
    )(x, ln.reshape(1, d), wqkv, cs, sn, gains, ones, perm)


def dilated_group_attention(qkv, group, batch, s_len, heads):
    window, dil = DILATED_PAIRS[group]
    half = window // (2 * dil)
    seq = s_len // dil
    width = heads * A_HEAD_DIM
    qt = min(256, seq)
    kw = min(qt + 2 * half, seq)
    assert seq % qt == 0 and (seq - kw) % math.gcd(half, qt) == 0
    rb = max(1, min(dil, DIL_ATTN_ROWS // qt))
    assert dil % rb == 0
    whole = lambda c: pl.BlockSpec((None, rb, seq, width), lambda b, r, qi: (b, r, 0, c))
    tile = lambda n: pl.BlockSpec((None, rb, qt, n), lambda b, r, qi: (b, r, qi, 0))
    kern = functools.partial(_dil_attn_kernel, seq=seq, qt=qt, kw=kw, half=half, heads=heads)
    return pl.pallas_call(
        kern,
        grid=(batch, dil // rb, seq // qt),
        in_specs=[tile(width), whole(1), whole(2)],
        out_specs=[tile(width), tile(LANES)],
        out_shape=[jax.ShapeDtypeStruct((batch, dil, seq, width), BF16),
                   jax.ShapeDtypeStruct((batch, dil, seq, LANES), F32)],
        compiler_params=_params("parallel", "parallel", "parallel"),
        name=f"dilated_attention_g{group}",
    )(qkv, qkv, qkv)


def _combine_wo_kernel(x_ref, o0_ref, o1_ref, o2_ref, l0_ref, l1_ref, l2_ref, w_ref, out_ref, ot_ref, lt_ref, *, heads):
    hd = A_HEAD_DIM
    rep = LANES // heads
    tm = x_ref.shape[0]

    def token_order(src_ref, dst_ref, g):
        dil = src_ref.shape[0]
        if dil == 1:
            return src_ref[0].astype(F32)
        chunks = src_ref.shape[-1] // LANES
        for r in range(dil):
            blk = src_ref[r].astype(F32)
            for c in range(chunks):
                dst_ref[g, c, pl.ds(r, tm // dil, stride=dil), :] = blk[:, c * LANES:(c + 1) * LANES]
        return jnp.concatenate([dst_ref[g, c] for c in range(chunks)], axis=-1)

    ls = [token_order(ref, lt_ref, g) for g, ref in enumerate((l0_ref, l1_ref, l2_ref))]
    os_ = [token_order(ref, ot_ref, g) for g, ref in enumerate((o0_ref, o1_ref, o2_ref))]
    m = jnp.maximum(jnp.maximum(ls[0], ls[1]), ls[2])
    es = [jnp.exp(l - m) for l in ls]
    inv = 1.0 / (es[0] + es[1] + es[2])
    al = [e * inv for e in es]
    parts = []
    for h in range(heads):
        sl = slice(h * hd, (h + 1) * hd)
        c = slice(h * rep, h * rep + 1)
        parts.append((al[0][:, c] * os_[0][:, sl] + al[1][:, c] * os_[1][:, sl] + al[2][:, c] * os_[2][:, sl]).astype(BF16))
    o = jnp.concatenate(parts, axis=-1)
    out_ref[...] = x_ref[...] + jnp.dot(o, w_ref[...], preferred_element_type=F32)


def combine_wo(x, outs, lses, wo, heads, batch, s_len):
    t, d = x.shape
    width = wo.shape[0]
    n_groups = len(DILATED_PAIRS)
    tm = _row_tile(s_len, 512)
    tiles = s_len // tm
    assert all(tm % (16 * dil) == 0 for _, dil in DILATED_PAIRS)
    row = lambda n: pl.BlockSpec((tm, n), lambda i: (i, 0))
    res = lambda g, n: pl.BlockSpec((None, DILATED_PAIRS[g][1], tm // DILATED_PAIRS[g][1], n),
                                    lambda i: (i // tiles, 0, i % tiles, 0))
    return pl.pallas_call(
        functools.partial(_combine_wo_kernel, heads=heads),
        grid=(t // tm,),
        in_specs=[row(d)] + [res(g, width) for g in range(n_groups)] + [res(g, LANES) for g in range(n_groups)]
                 + [pl.BlockSpec((width, d), lambda i: (0, 0))],
        out_specs=row(d),
        out_shape=jax.ShapeDtypeStruct((t, d), F32),
        scratch_shapes=[pltpu.VMEM((n_groups, width // LANES, tm, LANES), F32),
                        pltpu.VMEM((n_groups, 1, tm, LANES), F32)],
        compiler_params=_params("parallel"),
        name="combine_wo",
    )(x, *outs, *lses, wo)


def mixer_a(x, ln, wqkv, qk_norm, wo, cs, sn, batch, s_len):
    heads = wo.shape[0] // A_HEAD_DIM
    wqkv = wqkv.astype(BF16)
    outs, lses = [], []
    for g in range(len(DILATED_PAIRS)):
        qkv = qkv_group_projection(x, ln, wqkv, cs, sn, qk_norm[0, g], qk_norm[1, g], g, batch, s_len, wo.shape[0])
        o, lse = dilated_group_attention(qkv, g, batch, s_len, heads)
        outs.append(o)
        lses.append(lse)
    return combine_wo(x, outs, lses, wo.astype(BF16), heads, batch, s_len)


def _ffn_kernel(x_ref, g_ref, w1_ref, w3_ref, w2_ref, o_ref, xn_ref, acc_ref):
    j = pl.program_id(1)

    @pl.when(j == 0)
    def _():
        xn_ref[...] = _rms(x_ref[...], g_ref[...]).astype(BF16)
        acc_ref[...] = jnp.zeros_like(acc_ref)

    acc_ref[...] += _swiglu_chunks(xn_ref[...], w1_ref, w3_ref, w2_ref)

    @pl.when(j == pl.num_programs(1) - 1)
    def _():
        o_ref[...] = x_ref[...] + acc_ref[...]


FFN_SUB = 256


def _swiglu_chunks(xn, w1_ref, w3_ref, w2_ref):
    tf = w2_ref.shape[0]
    assert tf % FFN_SUB == 0
    total = None
    for s in range(tf // FFN_SUB):
        sl = slice(s * FFN_SUB, (s + 1) * FFN_SUB)
        gate = jnp.dot(xn, w1_ref[:, sl], preferred_element_type=F32)
        up = jnp.dot(xn, w3_ref[:, sl], preferred_element_type=F32)
        act = (gate * jax.nn.sigmoid(gate) * up).astype(BF16)
        part = jnp.dot(act, w2_ref[sl, :], preferred_element_type=F32)
        total = part if total is None else total + part
    return total


def dense_ffn(x, ln, w13, w2):
    t, d = x.shape
    tf = dff = w2.shape[0]
    nj = dff // tf
    tm = _row_tile(t, 512)
    resident = dict(pipeline_mode=pl.Buffered(1))
    return pl.pallas_call(
        _ffn_kernel,
        grid=(t // tm, nj),
        in_specs=[pl.BlockSpec((tm, d), lambda i, j: (i, 0)),
                  pl.BlockSpec((1, d), lambda i, j: (0, 0)),
                  pl.BlockSpec((d, tf), lambda i, j: (0, j), **resident),
                  pl.BlockSpec((d, tf), lambda i, j: (0, nj + j), **resident),
                  pl.BlockSpec((tf, d), lambda i, j: (j, 0), **resident)],
        out_specs=pl.BlockSpec((tm, d), lambda i, j: (i, 0)),
        out_shape=jax.ShapeDtypeStruct((t, d), F32),
        scratch_shapes=[pltpu.VMEM((tm, d), BF16), pltpu.VMEM((tm, d), F32)],
        compiler_params=_params("parallel", "arbitrary"),
        name="dense_ffn",
    )(x, ln.reshape(1, d), w13, w13, w2)


C_ROPE = ROT_DIM
C_NOPE = 64
C_V = 64
C_QK = C_ROPE + C_NOPE
LOG2_E = math.log2(math.e)


def _mla_qkv_kernel(lat_ref, qa_ref, kva_ref, gq_ref, gk_ref, wq_ref, wkv_ref, cs_ref, sn_ref,
                    ones_ref, perm_ref, q_ref, k_ref, vt_ref, *, heads, q_lora, kv_lora):
    two = lambda a: jnp.concatenate([a, a], axis=1)
    cs, sn = two(cs_ref[...]), two(sn_ref[...])
    lat = lat_ref[...]
    qn = _rms(lat[:, :q_lora], qa_ref[...]).astype(BF16)
    kvn = _rms(lat[:, q_lora:q_lora + kv_lora], kva_ref[...]).astype(BF16)
    k_rope = two(lat[:, q_lora + kv_lora:])

    def head_norm(a, g, post):
        sumsq = jnp.dot((a * a).astype(BF16), ones_ref[...], preferred_element_type=F32)
        ag = a * g
        pair = jnp.dot(ag.astype(BF16), perm_ref[...], preferred_element_type=F32)
        return (ag * cs + pair * sn) * (lax.rsqrt(sumsq * (1.0 / C_QK) + NORM_EPS) * post)

    gq, gk = two(gq_ref[...]), two(gk_ref[...])
    kv = jnp.dot(kvn, wkv_ref[...], preferred_element_type=F32)
    for p in range(heads * LANES // A_PAIR):
        sl = slice(p * A_PAIR, (p + 1) * A_PAIR)
        q = jnp.dot(qn, wq_ref[:, sl], preferred_element_type=F32)
        q_ref[:, sl] = head_norm(q, gq, C_QK ** -0.5 * LOG2_E).astype(BF16)
        k_ref[:, sl] = head_norm(kv[:, sl] + k_rope, gk, 1.0).astype(BF16)
    vt_ref[...] = kv[:, heads * LANES:].T.astype(BF16)


def _mla_attn_kernel(q_ref, k_ref, vt_ref, o_ref, *, pack):
    ones = jnp.ones((16, k_ref.shape[1]), BF16)
    tq = q_ref.shape[1]
    halves = 2 if tq % (2 * LANES) == 0 else 1
    qw = tq // halves
    work = [(i, j) for i in range(pack) for j in range(halves)]
    st = [lax.dot_general(k_ref[0, :, i * LANES:(i + 1) * LANES], q_ref[0, j * qw:(j + 1) * qw, i * LANES:(i + 1) * LANES],
                          (((1,), (1,)), ((), ())), preferred_element_type=F32).astype(BF16) for i, j in work]
    m = [jnp.max(t, axis=0, keepdims=True) for t in st]
    p = [jnp.exp2(t - mm) for t, mm in zip(st, m)]
    vts = [jnp.concatenate([vt_ref[0, i * C_V:(i + 1) * C_V, :], ones], axis=0) for i in range(pack)]
    ol = [jnp.dot(vts[i], pp, preferred_element_type=F32) for (i, j), pp in zip(work, p)]
    for (i, j), t in zip(work, ol):
        o_ref[0, i * C_V:(i + 1) * C_V, j * qw:(j + 1) * qw] = (t[:C_V] / t[C_V:C_V + 1]).astype(o_ref.dtype)


def _matmul_res_t_kernel(at_ref, w_ref, x_ref, o_ref):
    o_ref[...] = x_ref[...] + lax.dot_general(at_ref[...], w_ref[...], (((0,), (0,)), ((), ())),
                                              preferred_element_type=F32)


def matmul_residual_t(a_t, w, x, batch, s_len):
    k, n = w.shape
    tm = _row_tile(s_len, 512)
    tiles = s_len // tm
    return pl.pallas_call(
        _matmul_res_t_kernel,
        grid=(batch * tiles,),
        in_specs=[pl.BlockSpec((None, k, tm), lambda i: (i // tiles, 0, i % tiles)),
                  pl.BlockSpec((k, n), lambda i: (0, 0)),
                  pl.BlockSpec((tm, n), lambda i: (i, 0))],
        out_specs=pl.BlockSpec((tm, n), lambda i: (i, 0)),
        out_shape=jax.ShapeDtypeStruct(x.shape, F32),
        compiler_params=_params("parallel"),
        name="matmul_residual_t",
    )(a_t, w, x)


def _pad_cols(w, heads, real, slot):
    k = w.shape[0]
    w = w.reshape(k, heads, real)
    return jnp.pad(w, ((0, 0), (0, 0), (0, slot - real))).reshape(k, heads * slot)


def mixer_c(x, ln, wa, qa_norm, kva_norm, wq_b, wkv_b, qk_norm, wo, cs, sn, batch, s_len):
    t, d = x.shape
    q_lora, kv_lora = qa_norm.shape[0], kva_norm.shape[0]
    heads = wo.shape[0] // C_V
    lat_w = q_lora + kv_lora + LANES
    wa_p = jnp.pad(wa, ((0, 0), (0, lat_w - wa.shape[1]))).astype(BF16)
    lat = norm_matmul(x, ln, wa_p, F32, tn=lat_w)
    wq_p = _pad_cols(wq_b, heads, C_QK, LANES).astype(BF16)
    wkv = wkv_b.reshape(kv_lora, heads, C_NOPE + C_V)
    wk_p = jnp.pad(wkv[:, :, :C_NOPE], ((0, 0), (0, 0), (C_ROPE, LANES - C_QK))).reshape(kv_lora, heads * LANES)
    wv = wkv[:, :, C_NOPE:].reshape(kv_lora, heads * C_V)
    wkv_p = jnp.concatenate([wk_p, wv], axis=1).astype(BF16)
    gq = jnp.pad(qk_norm[0], (0, LANES - C_QK)).reshape(1, LANES)
    gk = jnp.pad(qk_norm[1], (0, LANES - C_QK)).reshape(1, LANES)
    tm = _row_tile(s_len, 512)
    tiles = s_len // tm
    row = lambda n: pl.BlockSpec((tm, n), lambda i: (i, 0))
    full = lambda a: pl.BlockSpec(a.shape, lambda i: (0, 0))
    qa2, kva2 = qa_norm.reshape(1, -1), kva_norm.reshape(1, -1)
    ones, perm = _head_helper_matrices()
    q, k, vt = pl.pallas_call(
        functools.partial(_mla_qkv_kernel, heads=heads, q_lora=q_lora, kv_lora=kv_lora),
        grid=(t // tm,),
        in_specs=[row(lat_w), full(qa2), full(kva2), full(gq), full(gk), full(wq_p), full(wkv_p),
                  row(LANES), row(LANES), full(ones), full(perm)],
        out_specs=[row(heads * LANES), row(heads * LANES),
                   pl.BlockSpec((None, heads * C_V, tm), lambda i: (i // tiles, 0, i % tiles))],
        out_shape=[jax.ShapeDtypeStruct((t, heads * LANES), BF16),
                   jax.ShapeDtypeStruct((t, heads * LANES), BF16),
                   jax.ShapeDtypeStruct((batch, heads * C_V, s_len), BF16)],
        compiler_params=_params("parallel"),
        name="mla_qkv",
    )(lat, qa2, kva2, gq, gk, wq_p, wkv_p, cs, sn, ones, perm)
    pack = LANES // C_V
    tq = min(512, s_len)
    o_t = pl.pallas_call(
        functools.partial(_mla_attn_kernel, pack=pack),
        grid=(batch, heads // pack, s_len // tq),
        in_specs=[pl.BlockSpec((1, tq, pack * LANES), lambda b, h, i: (b, i, h)),
                  pl.BlockSpec((1, s_len, pack * LANES), lambda b, h, i: (b, 0, h)),
                  pl.BlockSpec((1, pack * C_V, s_len), lambda b, h, i: (b, h, 0))],
        out_specs=pl.BlockSpec((1, pack * C_V, tq), lambda b, h, i: (b, h, i)),
        out_shape=jax.ShapeDtypeStruct((batch, heads * C_V, s_len), BF16),
        compiler_params=_params("parallel", "parallel", "arbitrary"),
        name="mla_attention",
    )(q.reshape(batch, s_len, -1), k.reshape(batch, s_len, -1), vt)
    return matmul_residual_t(o_t, wo.astype(BF16), x, batch, s_len)


B_HEAD_DIM = 64
B_DECAY_SCALE = math.exp(-0.5)
B_GN_EPS = 64e-5
WKV_CHUNK = 64
WKV_HEADS_PER_STEP = 16
WKV_SEQS_PER_STEP = 1
WKV_PACK = 4


def _bdot(a, b):
    return jnp.dot(a.astype(BF16), b.astype(BF16), preferred_element_type=F32)


def _rwkv_prep_kernel(x_ref, xp_ref, xn_ref, ln_ref, mu_ref, wrkv_ref, g1_ref, g2_ref, w0_ref, w1_ref, w2_ref,
                      a0_ref, a1_ref, a2_ref, r_ref, k_ref, v_ref, g_ref, lw_ref, as_ref, *, tiles_per_seq):
    i = pl.program_id(0)
    tm = x_ref.shape[0]
    ln = ln_ref[...]
    h = _rms(x_ref[...], ln)
    pos = i % tiles_per_seq
    prev_row = jnp.where(pos == 0, 0.0, _rms(xp_ref[7:8, :], ln))
    next_row = jnp.where(pos == tiles_per_seq - 1, 0.0, _rms(xn_ref[0:1, :], ln))
    row = lax.broadcasted_iota(I32, (tm, 1), 0)
    d_f = jnp.where(row == 0, prev_row, pltpu.roll(h, 1, 0)) - h
    d_b = jnp.where(row == tm - 1, next_row, pltpu.roll(h, tm - 1, 0)) - h
    d_c = 0.5 * (d_f + d_b)
    mu = mu_ref[...]
    mix = lambda dd, n: h + dd * mu[n:n + 1]
    r_ref[...] = _bdot(mix(d_c, 0), wrkv_ref[0]).astype(r_ref.dtype)
    k_ref[...] = _bdot(mix(d_c, 1), wrkv_ref[1]).astype(k_ref.dtype)
    v_ref[...] = _bdot(mix(d_c, 2), wrkv_ref[2]).astype(v_ref.dtype)
    g_ref[...] = _bdot(jax.nn.sigmoid(_bdot(mix(d_c, 3), g1_ref[...])), g2_ref[...]).astype(g_ref.dtype)
    for dr, dd in enumerate((d_f, d_b)):
        z = w0_ref[dr:dr + 1, :] + _bdot(jnp.tanh(_bdot(mix(dd, 4 + dr), w1_ref[dr])), w2_ref[dr])
        lw_ref[dr] = -B_DECAY_SCALE * jax.nn.sigmoid(z)
        aa = a0_ref[dr:dr + 1, :] + _bdot(_bdot(mix(dd, 6 + dr), a1_ref[dr]), a2_ref[dr])
        as_ref[dr] = jax.nn.sigmoid(aa).astype(as_ref.dtype)


def _wkv_kernel(rf_ref, kf_ref, vf_ref, rb_ref, kb_ref, vb_ref, lwf_ref, lwb_ref, asf_ref, asb_ref, kk_ref, ka_ref,
                yf_ref, yb_ref, st_ref, *, hb):
    n = B_HEAD_DIM
    nb, c = rf_ref.shape[0], rf_ref.shape[1]

    @pl.when(pl.program_id(1) == 0)
    def _():
        st_ref[...] = jnp.zeros_like(st_ref)

    nt = (((1,), (1,)), ((), ()))
    tn = (((0,), (0,)), ((), ()))
    mm = (((1,), (0,)), ((), ()))
    low = lax.broadcasted_iota(I32, (1, LANES), 1) < n
    pk = WKV_PACK
    gw = pk * n
    per_dir = hb // pk
    gsl = [slice(g * gw, (g + 1) * gw) for g in range(per_dir)]
    cut = lambda a: [a[:, sl] for sl in gsl]
    row_p = lax.broadcasted_iota(I32, (c, gw), 0)
    col_p = lax.broadcasted_iota(I32, (c, gw), 1) % n

    def operands(sgn, bi, r_ref, k_ref, v_ref, lw_ref, as_ref):
        r_ref, k_ref, v_ref, lw_ref, as_ref = (ref.at[bi] for ref in (r_ref, k_ref, v_ref, lw_ref, as_ref))
        ahead = (lax.broadcasted_iota(I32, (c, c), 0) - lax.broadcasted_iota(I32, (c, c), 1)) * sgn
        lw = lw_ref[...]
        cl = jnp.dot((ahead >= 0).astype(F32), lw, preferred_element_type=F32, precision=lax.Precision.HIGHEST)
        total = jnp.sum(lw, axis=0, keepdims=True)
        e_in, e_ex, e_ng, e_end = jnp.exp(cl), jnp.exp(cl - lw), jnp.exp(-cl), jnp.exp(total - cl)
        rr, kk_raw, vv, asig = (ref[...].astype(F32) for ref in (r_ref, k_ref, v_ref, as_ref))
        kk_all = kk_raw * kk_ref[...]
        kdir = kk_raw * (1.0 + (asig - 1.0) * ka_ref[...])
        sq = kk_all * kk_all
        norms = []
        for p in range(sq.shape[1] // LANES):
            slab = sq[:, p * LANES:(p + 1) * LANES]
            s_lo = jnp.sum(jnp.where(low, slab, 0.0), axis=-1, keepdims=True)
            s_hi = jnp.sum(jnp.where(low, 0.0, slab), axis=-1, keepdims=True)
            norms.append(jnp.where(low, s_lo, s_hi))
        kkn_all = kk_all * lax.rsqrt(jnp.maximum(jnp.concatenate(norms, axis=-1), 1e-24))
        beta_all = kkn_all * asig
        ahead_p = (row_p - col_p) * sgn
        return dict(
            a_t=cut((-kkn_all * e_ex).astype(BF16)), b_t=cut((beta_all * e_ng).astype(BF16)),
            rt=cut((rr * e_in).astype(BF16)), kt=cut((kdir * e_ng).astype(BF16)), v=cut(vv.astype(BF16)),
            b_h=cut((beta_all * e_end).astype(BF16)), k_hb=cut((kdir * e_end).astype(BF16)),
            p_end=cut(jnp.exp(total)), strict=[ahead_p > 0] * per_dir, incl=[ahead_p >= 0] * per_dir)

    both = None
    for bi in range(nb):
        for part in (operands(1, bi, rf_ref, kf_ref, vf_ref, lwf_ref, asf_ref),
                     operands(-1, bi, rb_ref, kb_ref, vb_ref, lwb_ref, asb_ref)):
            both = part if both is None else {key: both[key] + part[key] for key in part}
    a_t, b_t, rt, kt, v, b_h, k_hb = (both[key] for key in ("a_t", "b_t", "rt", "kt", "v", "b_h", "k_hb"))
    p_end, strict_p, incl_p = both["p_end"], both["strict"], both["incl"]
    groups = range(nb * 2 * per_dir)

    diag_block = ((lax.broadcasted_iota(I32, (gw, gw), 0) // n)
                  == (lax.broadcasted_iota(I32, (gw, gw), 1) // n))
    diag_bf = diag_block.astype(BF16)

    def bd(p):
        return jnp.concatenate([p] * pk, axis=0) * diag_bf

    def fold(full):
        kept = jnp.where(diag_block, full, 0.0)
        out = kept[:n]
        for h in range(1, pk):
            out = out + kept[h * n:(h + 1) * n]
        return out

    eye_p = (row_p == col_p).astype(F32)
    same_block = lambda m: (row_p // m) == (col_p // m)
    dots = lambda a, b, dims: [lax.dot_general(p, q, dims, preferred_element_type=F32) for p, q in zip(a, b)]
    rows = lambda *parts: [jnp.concatenate(list(ps), axis=0) for ps in zip(*parts)]
    bf = lambda xs: [t.astype(BF16) for t in xs]
    bds = lambda xs: [bd(t) for t in xs]
    gram = dots(rows(a_t, rt), rows(bds(b_t), bds(kt)), nt)
    l_ab = [jnp.where(s, g[:c, :gw], 0.0) for s, g in zip(strict_p, gram)]
    l_ak = [jnp.where(s, g[:c, gw:], 0.0).astype(BF16) for s, g in zip(strict_p, gram)]
    m_r = [jnp.concatenate([jnp.where(i, g[c:, :gw], 0.0), jnp.where(i, g[c:, gw:], 0.0)],
                           axis=1).astype(BF16) for i, g in zip(incl_p, gram)]
    in_pair = same_block(2)
    x = [eye_p + jnp.where(in_pair, l, 0.0) for l in l_ab]
    m = 2
    while m < c:
        level = same_block(2 * m) & jnp.logical_not(same_block(m))
        xb = bf(x)
        l_m = [jnp.where(level, l, 0.0).astype(BF16) for l in l_ab]
        xl = bf(dots(xb, bds(l_m), mm))
        x = [xx + t for xx, t in zip(x, dots(xl, bds(xb), mm))]
        m *= 2
    s0 = [st_ref[g] for g in groups]
    bd_s = bds(bf(s0))
    bd_v = bds(v)
    w = [p + q for p, q in zip(dots(a_t, bd_s, nt), dots(l_ak, bd_v, mm))]
    u = bf(dots(bf(x), bds(bf(w)), mm))
    y = [p + q for p, q in zip(dots(rt, bd_s, nt), dots(m_r, rows(bds(u), bd_v), mm))]
    s_new = dots(rows(u, v), rows(b_h, k_hb), tn)
    for g in groups:
        st_ref[g] = s0[g] * p_end[g] + fold(s_new[g])
        bi, within = divmod(g, 2 * per_dir)
        out_ref = yf_ref if within < per_dir else yb_ref
        out_ref[bi, :, gsl[within % per_dir]] = y[g]


def _rwkv_out_kernel(x_ref, yf_ref, yb_ref, r_ref, k_ref, v_ref, g_ref, as_ref, ka_ref, rk_ref, lnx_ref, wo_ref,
                     sum_ref, o_ref):
    d = x_ref.shape[1]
    n = B_HEAD_DIM
    y = yf_ref[...] + yb_ref[...]
    a_sum = as_ref[0].astype(F32) + as_ref[1].astype(F32)
    kd = k_ref[...].astype(F32) * (2.0 + (a_sum - 2.0) * ka_ref[...])
    prod = r_ref[...].astype(F32) * kd * rk_ref[...]
    slab = sum_ref.shape[0]

    def head_sum(a):
        return jnp.dot(a.astype(BF16), sum_ref[...], preferred_element_type=F32)

    parts = []
    for p in range(d // slab):
        sl = slice(p * slab, (p + 1) * slab)
        ys = y[:, sl]
        cen = ys - head_sum(ys) * (1.0 / n)
        var = head_sum(cen * cen) * (1.0 / n)
        yn = cen * lax.rsqrt(var + B_GN_EPS) * lnx_ref[0:1, sl] + lnx_ref[1:2, sl]
        bonus = head_sum(prod[:, sl]) * v_ref[:, sl]
        parts.append(((yn + bonus) * g_ref[:, sl]).astype(BF16))
    o_ref[...] = x_ref[...] + jnp.dot(jnp.concatenate(parts, axis=-1), wo_ref[...], preferred_element_type=F32)


def mixer_b(x, ln, mu, w_rkv, w0, w1, w2, a0, a1, a2, g1, g2, k_k, k_a, r_k, ln_x, wo, batch, s_len):
    t, d = x.shape
    bf = lambda a: a.astype(BF16)
    halo = 8
    full = lambda a: pl.BlockSpec(a.shape, lambda i: (0,) * a.ndim)
    resident = lambda a: pl.BlockSpec(a.shape, lambda i: (0,) * a.ndim, pipeline_mode=pl.Buffered(1))
    ln2 = ln.reshape(1, d)
    weights = (ln2, mu, bf(w_rkv), bf(g1), bf(g2), w0, bf(w1), bf(w2), a0, bf(a1), bf(a2))
    tok = lambda dt: jax.ShapeDtypeStruct((t, d), dt)
    tok2 = lambda dt: jax.ShapeDtypeStruct((2, t, d), dt)
    tm = _row_tile(s_len, 512)
    tiles_per_seq = s_len // tm
    row = lambda n: pl.BlockSpec((tm, n), lambda i: (i, 0))
    row2 = lambda n: pl.BlockSpec((2, tm, n), lambda i: (0, i, 0))
    r, k, v, g, lw, asig = pl.pallas_call(
        functools.partial(_rwkv_prep_kernel, tiles_per_seq=tiles_per_seq),
        grid=(t // tm,),
        in_specs=[row(d),
                  pl.BlockSpec((halo, d), lambda i: (jnp.maximum(i * (tm // halo) - 1, 0), 0)),
                  pl.BlockSpec((halo, d), lambda i: (jnp.minimum((i + 1) * (tm // halo), t // halo - 1), 0))]
                 + [resident(a) for a in weights],
        out_specs=[row(d), row(d), row(d), row(d), row2(d), row2(d)],
        out_shape=[tok(BF16), tok(BF16), tok(BF16), tok(BF16), tok2(F32), tok2(BF16)],
        compiler_params=_params("parallel"),
        name="rwkv_prep",
    )(x, x, x, *weights)
    tm = _row_tile(s_len, 256)
    row = lambda n: pl.BlockSpec((tm, n), lambda i: (i, 0))
    row2 = lambda n: pl.BlockSpec((2, tm, n), lambda i: (0, i, 0))

    hb = WKV_HEADS_PER_STEP
    width = hb * B_HEAD_DIM
    c = min(WKV_CHUNK, s_len)
    nc = s_len // c
    assert width == d
    nb = math.gcd(batch, WKV_SEQS_PER_STEP)
    tok_f = pl.BlockSpec((nb, c, d), lambda b, ci: (b, ci, 0))
    tok_b = pl.BlockSpec((nb, c, d), lambda b, ci: (b, nc - 1 - ci, 0))
    dir_f = pl.BlockSpec((None, nb, c, d), lambda b, ci: (0, b, ci, 0))
    dir_b = pl.BlockSpec((None, nb, c, d), lambda b, ci: (1, b, nc - 1 - ci, 0))
    par = pl.BlockSpec((1, d), lambda b, ci: (0, 0))
    v3 = lambda a: a.reshape(batch, s_len, d)
    v4 = lambda a: a.reshape(2, batch, s_len, d)
    seq_out = jax.ShapeDtypeStruct((batch, s_len, d), F32)
    y_f, y_b = pl.pallas_call(
        functools.partial(_wkv_kernel, hb=hb),
        grid=(batch // nb, nc),
        in_specs=[tok_f, tok_f, tok_f, tok_b, tok_b, tok_b, dir_f, dir_b, dir_f, dir_b, par, par],
        out_specs=[tok_f, tok_b],
        out_shape=[seq_out, seq_out],
        scratch_shapes=[pltpu.VMEM((nb * 2 * hb // WKV_PACK, B_HEAD_DIM, WKV_PACK * B_HEAD_DIM), F32)],
        compiler_params=_params("parallel", "arbitrary"),
        name="wkv_scan",
    )(v3(r), v3(k), v3(v), v3(r), v3(k), v3(v), v4(lw), v4(lw), v4(asig), v4(asig),
      k_k.reshape(1, d), k_a.reshape(1, d))

    lane = jnp.arange(WKV_PACK * B_HEAD_DIM)
    same_head = ((lane[:, None] // B_HEAD_DIM) == (lane[None, :] // B_HEAD_DIM)).astype(BF16)
    params = (k_a.reshape(1, d), r_k.reshape(1, d), ln_x, bf(wo), same_head)
    return pl.pallas_call(
        _rwkv_out_kernel,
        grid=(t // tm,),
        in_specs=[row(d), row(d), row(d), row(d), row(d), row(d), row(d), row2(d)] + [full(a) for a in params],
        out_specs=row(d),
        out_shape=tok(F32),
        compiler_params=_params("parallel"),
        name="rwkv_out",
    )(x, y_f.reshape(t, d), y_b.reshape(t, d), r, k, v, g, asig, *params)


TOP_K = 2
ROUTE_EXPERT, ROUTE_GATE, ROUTE_RANK = 0, TOP_K, 2 * TOP_K
MOE_ROW_TILE = 512
ROW_DMA_UNROLL = 8


def _router_kernel(x_ref, g_ref, wr_ref, hn_ref, route_ref, cnt_ref, run_ref, *, n_exp):
    @pl.when(pl.program_id(0) == 0)
    def _():
        run_ref[...] = jnp.zeros_like(run_ref)

    tm = x_ref.shape[0]
    h = _rms(x_ref[...], g_ref[...])
    hn_ref[...] = h
    logits = jnp.dot(h, wr_ref[...], preferred_element_type=F32, precision=lax.Precision.HIGHEST)
    lane = lax.broadcasted_iota(I32, (tm, LANES), 1)
    logits = jnp.where(lane < n_exp, logits, NEG_INF)

    def top1(v):
        m = jnp.max(v, axis=-1, keepdims=True)
        return m, jnp.min(jnp.where(v == m, lane, LANES), axis=-1, keepdims=True)

    m1, i1 = top1(logits)
    m2, i2 = top1(jnp.where(lane == i1, NEG_INF, logits))
    e = jnp.exp(m2 - m1)
    g1 = 1.0 / (1.0 + e)
    g2 = e * g1
    sel = ((lane == i1) | (lane == i2)).astype(F32)
    earlier = (lax.broadcasted_iota(I32, (tm, tm), 0) > lax.broadcasted_iota(I32, (tm, tm), 1)).astype(BF16)
    before = jnp.dot(earlier, sel.astype(BF16), preferred_element_type=F32) + run_ref[...]
    r1 = jnp.sum(jnp.where(lane == i1, before, 0.0), axis=-1, keepdims=True)
    r2 = jnp.sum(jnp.where(lane == i2, before, 0.0), axis=-1, keepdims=True)
    run_ref[...] += jnp.sum(sel, axis=0, keepdims=True)
    cnt_ref[...] = run_ref[...]
    route = jnp.zeros((tm, LANES), F32)
    for off, vals in ((ROUTE_EXPERT, (i1.astype(F32), i2.astype(F32))), (ROUTE_GATE, (g1, g2)), (ROUTE_RANK, (r1, r2))):
        for kk, v in enumerate(vals):
            route = jnp.where(lane == off + kk, v, route)
    route_ref[...] = route


def _dispatch_kernel(dest_ref, hn_ref, xs_in_ref, xs_ref, sem):
    del xs_in_ref
    tm = hn_ref.shape[0]

    def row_copy(r, slot):
        return pltpu.make_async_copy(hn_ref.at[pl.ds(r, 1)], xs_ref.at[pl.ds(dest_ref[TOP_K * r + slot], 1)], sem)

    def start(r, c):
        for slot in range(TOP_K):
            row_copy(r, slot).start()
        return c

    def wait(r, c):
        for slot in range(TOP_K):
            row_copy(r, slot).wait()
        return c

    lax.fori_loop(0, tm, start, 0, unroll=ROW_DMA_UNROLL)
    lax.fori_loop(0, tm, wait, 0, unroll=ROW_DMA_UNROLL)


def _expert_ffn_kernel(te_ref, used_ref, x_ref, w1_ref, w3_ref, w2_ref, o_ref, xb_ref, acc_ref):
    del te_ref
    j = pl.program_id(1)
    last = j == pl.num_programs(1) - 1
    in_use = pl.program_id(0) < used_ref[0]

    @pl.when(in_use)
    def _():
        @pl.when(j == 0)
        def _():
            xb_ref[...] = x_ref[...].astype(BF16)
            acc_ref[...] = jnp.zeros_like(acc_ref)

        acc_ref[...] += _swiglu_chunks(xb_ref[...], w1_ref, w3_ref, w2_ref)

        @pl.when(last)
        def _():
            o_ref[...] = acc_ref[...]

    @pl.when(jnp.logical_not(in_use) & last)
    def _():
        o_ref[...] = jnp.zeros_like(o_ref)


def _moe_combine_kernel(dest_ref, route_ref, x_ref, ys_ref, o_ref, buf_ref, sem):
    tm = x_ref.shape[0]

    def row_copy(r, slot):
        return pltpu.make_async_copy(ys_ref.at[pl.ds(dest_ref[TOP_K * r + slot], 1)], buf_ref.at[slot, pl.ds(r, 1)], sem)

    def start(r, c):
        for slot in range(TOP_K):
            row_copy(r, slot).start()
        return c

    def wait(r, c):
        for slot in range(TOP_K):
            row_copy(r, slot).wait()
        return c

    lax.fori_loop(0, tm, start, 0, unroll=ROW_DMA_UNROLL)
    lax.fori_loop(0, tm, wait, 0, unroll=ROW_DMA_UNROLL)
    route = route_ref[...]
    out = x_ref[...]
    for slot in range(TOP_K):
        out = out + route[:, ROUTE_GATE + slot:ROUTE_GATE + slot + 1] * buf_ref[slot]
    o_ref[...] = out


def moe_ffn(x, ln, router, w13, w2, tf=1792):
    t, d = x.shape
    n_exp, dff = w2.shape[0], w2.shape[1]
    assert dff % tf == 0
    nj = dff // tf
    tmx = min(MOE_ROW_TILE, t)
    tm = _row_tile(t, 512)
    wr = jnp.pad(router, ((0, 0), (0, LANES - n_exp)))
    hn, route, cnt = pl.pallas_call(
        functools.partial(_router_kernel, n_exp=n_exp),
        grid=(t // tm,),
        in_specs=[pl.BlockSpec((tm, d), lambda i: (i, 0)),
                  pl.BlockSpec((1, d), lambda i: (0, 0)),
                  pl.BlockSpec((d, LANES), lambda i: (0, 0))],
        out_specs=[pl.BlockSpec((tm, d), lambda i: (i, 0)),
                   pl.BlockSpec((tm, LANES), lambda i: (i, 0)),
                   pl.BlockSpec((1, LANES), lambda i: (0, 0))],
        out_shape=[jax.ShapeDtypeStruct((t, d), F32),
                   jax.ShapeDtypeStruct((t, LANES), F32),
                   jax.ShapeDtypeStruct((1, LANES), F32)],
        scratch_shapes=[pltpu.VMEM((1, LANES), F32)],
        compiler_params=_params("arbitrary"),
        name="moe_router",
    )(x, ln.reshape(1, d), wr)

    expert = route[:, ROUTE_EXPERT:ROUTE_EXPERT + TOP_K].astype(I32)
    rank = route[:, ROUTE_RANK:ROUTE_RANK + TOP_K].astype(I32)
    counts = cnt[0, :n_exp].astype(I32)
    padded = (counts + tmx - 1) // tmx * tmx
    ends = jnp.cumsum(padded)
    dest = ((ends - padded)[expert] + rank).reshape(TOP_K * t)
    n_tiles = (TOP_K * t) // tmx + n_exp
    rows = n_tiles * tmx
    tile_start = jnp.arange(n_tiles, dtype=I32) * tmx
    tile_expert = jnp.minimum(jnp.sum((tile_start[:, None] >= ends[None, :]).astype(I32), axis=1), n_exp - 1)

    smem_rows = lambda n: pl.BlockSpec((TOP_K * n,), lambda i: (i,), memory_space=pltpu.SMEM)
    xs = pl.pallas_call(
        _dispatch_kernel,
        grid=(t // tm,),
        in_specs=[smem_rows(tm),
                  pl.BlockSpec((tm, d), lambda i: (i, 0)),
                  pl.BlockSpec(memory_space=pl.ANY)],
        out_specs=pl.BlockSpec(memory_space=pl.ANY),
        out_shape=jax.ShapeDtypeStruct((rows, d), F32),
        scratch_shapes=[pltpu.SemaphoreType.DMA(())],
        input_output_aliases={2: 0},
        compiler_params=_params("arbitrary"),
        name="moe_dispatch",
    )(dest, hn, jnp.zeros((rows, d), F32))

    ys = pl.pallas_call(
        _expert_ffn_kernel,
        grid_spec=pltpu.PrefetchScalarGridSpec(
            num_scalar_prefetch=2,
            grid=(n_tiles, nj),
            in_specs=[pl.BlockSpec((tmx, d), lambda i, j, te, nu: (i, 0)),
                      pl.BlockSpec((None, d, tf), lambda i, j, te, nu: (te[i], 0, j)),
                      pl.BlockSpec((None, d, tf), lambda i, j, te, nu: (te[i], 0, nj + j)),
                      pl.BlockSpec((None, tf, d), lambda i, j, te, nu: (te[i], j, 0))],
            out_specs=pl.BlockSpec((tmx, d), lambda i, j, te, nu: (i, 0)),
            scratch_shapes=[pltpu.VMEM((tmx, d), BF16), pltpu.VMEM((tmx, d), F32)]),
        out_shape=jax.ShapeDtypeStruct((rows, d), F32),
        compiler_params=_params("parallel", "arbitrary"),
        name="moe_expert_ffn",
    )(tile_expert, (ends[-1:] // tmx).astype(I32), xs, w13, w13, w2)

    return pl.pallas_call(
        _moe_combine_kernel,
        grid=(t // tm,),
        in_specs=[smem_rows(tm),
                  pl.BlockSpec((tm, LANES), lambda i: (i, 0)),
                  pl.BlockSpec((tm, d), lambda i: (i, 0)),
                  pl.BlockSpec(memory_space=pl.ANY)],
        out_specs=pl.BlockSpec((tm, d), lambda i: (i, 0)),
        out_shape=jax.ShapeDtypeStruct((t, d), F32),
        scratch_shapes=[pltpu.VMEM((TOP_K, tm, d), F32), pltpu.SemaphoreType.DMA(())],
        compiler_params=_params("arbitrary"),
        name="moe_combine",
    )(dest, route, x, ys)


def kernel(x, positions, ln_mix, ln_ffn, a_wqkv, a_qk_norm, a_wo, b_mu, b_wrkv, b_w0, b_w1, b_w2, b_a0, b_a1, b_a2, b_g1, b_g2, b_kk, b_ka, b_rk, b_lnx, b_wo, c_wa, c_qa_norm, c_kva_norm, c_wq_b, c_wkv_b, c_qk_norm, c_wo, f_w13, f_w2, m_router, m_w13, m_w2):
    batch, s_len, d = x.shape
    t = batch * s_len
    cs, sn = _rope_tables(positions)
    cs, sn = cs.reshape(t, LANES), sn.reshape(t, LANES)
    h = x.reshape(t, d)
    n_mixers = 3
    for i in range(ln_mix.shape[0]):
        j, kind = divmod(i, n_mixers)
        if kind == 0:
            h = mixer_a(h, ln_mix[i], a_wqkv[j], a_qk_norm[j], a_wo[j], cs, sn, batch, s_len)
        elif kind == 1:
            h = mixer_b(h, ln_mix[i], b_mu[j], b_wrkv[j], b_w0[j], b_w1[j], b_w2[j], b_a0[j], b_a1[j], b_a2[j],
                        b_g1[j], b_g2[j], b_kk[j], b_ka[j], b_rk[j], b_lnx[j], b_wo[j], batch, s_len)
        else:
            h = mixer_c(h, ln_mix[i], c_wa[j], c_qa_norm[j], c_kva_norm[j], c_wq_b[j], c_wkv_b[j], c_qk_norm[j],
                        c_wo[j], cs, sn, batch, s_len)
        f = i // 2
        if i % 2 == 0:
            h = dense_ffn(h, ln_ffn[i], f_w13[f].astype(BF16), f_w2[f].astype(BF16))
        else:
            h = moe_ffn(h, ln_ffn[i], m_router[f], m_w13[f].astype(BF16), m_w2[f].astype(BF16))
    return h.reshape(batch, s_len, d)
```

```python
import functools
import math

import jax
import jax.numpy as jnp
from jax import lax
from jax.experimental import pallas as pl
from jax.experimental.pallas import tpu as pltpu

F32 = jnp.float32
BF16 = jnp.bfloat16
I32 = jnp.int32

NORM_EPS = 1e-6
ROPE_THETA = 500000.0
NEG_INF = -1e30
LANES = 128
ROT_DIM = 32
ROT_HALF = ROT_DIM // 2
DILATED_PAIRS = ((128, 1), (512, 4), (2048, 16))
A_HEAD_DIM = 128
VMEM_LIMIT = 48 * 1024 * 1024


def _params(*sem):
    return pltpu.CompilerParams(dimension_semantics=sem, vmem_limit_bytes=VMEM_LIMIT)


def _rms(x, g):
    return x * lax.rsqrt(jnp.mean(x * x, axis=-1, keepdims=True) + NORM_EPS) * g


def _row_tile(t, want):
    tm = min(t, want)
    assert t % tm == 0
    return tm


def _norm_matmul_kernel(x_ref, g_ref, w_ref, o_ref, xn_ref):
    @pl.when(pl.program_id(1) == 0)
    def _():
        xn_ref[...] = _rms(x_ref[...], g_ref[...]).astype(BF16)

    o_ref[...] = jnp.dot(xn_ref[...], w_ref[...], preferred_element_type=F32).astype(o_ref.dtype)


def norm_matmul(x, g, w, out_dtype, tn):
    t, k = x.shape
    n = w.shape[1]
    tm = _row_tile(t, 1024)
    assert n % tn == 0
    return pl.pallas_call(
        _norm_matmul_kernel,
        grid=(t // tm, n // tn),
        in_specs=[pl.BlockSpec((tm, k), lambda i, j: (i, 0)),
                  pl.BlockSpec((1, k), lambda i, j: (0, 0)),
                  pl.BlockSpec((k, tn), lambda i, j: (0, j))],
        out_specs=pl.BlockSpec((tm, tn), lambda i, j: (i, j)),
        out_shape=jax.ShapeDtypeStruct((t, n), out_dtype),
        scratch_shapes=[pltpu.VMEM((tm, k), BF16)],
        compiler_params=_params("parallel", "arbitrary"),
        name="norm_matmul",
    )(x, g.reshape(1, k), w)


def _matmul_res_kernel(a_ref, w_ref, x_ref, o_ref):
    o_ref[...] = x_ref[...] + jnp.dot(a_ref[...], w_ref[...], preferred_element_type=F32)


def matmul_residual(a, w, x):
    t, k = a.shape
    n = w.shape[1]
    tm = _row_tile(t, 1024)
    return pl.pallas_call(
        _matmul_res_kernel,
        grid=(t // tm,),
        in_specs=[pl.BlockSpec((tm, k), lambda i: (i, 0)),
                  pl.BlockSpec((k, n), lambda i: (0, 0)),
                  pl.BlockSpec((tm, n), lambda i: (i, 0))],
        out_specs=pl.BlockSpec((tm, n), lambda i: (i, 0)),
        out_shape=jax.ShapeDtypeStruct((t, n), F32),
        compiler_params=_params("parallel"),
        name="matmul_residual",
    )(a, w, x)


def _rope_tables(positions):
    b, s = positions.shape
    inv_freq = ROPE_THETA ** (-jnp.arange(0, ROT_DIM, 2, dtype=F32) / ROT_DIM)
    ang = positions.astype(F32)[..., None] * inv_freq
    c, sn = jnp.cos(ang), jnp.sin(ang)
    rest = (b, s, LANES - ROT_DIM)
    cs_t = jnp.concatenate([c, c, jnp.ones(rest, F32)], axis=-1)
    sn_t = jnp.concatenate([-sn, sn, jnp.zeros(rest, F32)], axis=-1)
    return cs_t, sn_t


def _rope(x, cs, sn, lane):
    n = x.shape[-1]
    partner = jnp.where(lane < ROT_HALF, pltpu.roll(x, n - ROT_HALF, 1), pltpu.roll(x, ROT_HALF, 1))
    return x * cs + partner * sn


def _dil_attn_kernel(q_ref, k_ref, v_ref, o_ref, lse_ref, *, seq, qt, kw, half, heads):
    hd = A_HEAD_DIM
    qi = pl.program_id(2)
    q0 = pl.multiple_of(qi * qt, qt)
    start = jnp.clip(q0 - half, 0, seq - kw)
    start = pl.multiple_of(start, math.gcd(half, qt))
    qpos = q0 + lax.broadcasted_iota(I32, (qt, kw), 0)
    kpos = start + lax.broadcasted_iota(I32, (qt, kw), 1)
    valid = jnp.abs(qpos - kpos) <= half
    head_of_lane = lax.broadcasted_iota(I32, (1, LANES), 1) // (LANES // heads)
    ones = jnp.ones((kw, hd), BF16)
    work = [(r, slice(h * hd, (h + 1) * hd)) for r in range(q_ref.shape[0]) for h in range(heads)]
    s = [lax.dot_general(q_ref[r, :, sl], k_ref[r, pl.ds(start, kw), sl], (((1,), (1,)), ((), ())),
                         preferred_element_type=F32).astype(BF16) for r, sl in work]
    s = [jnp.where(valid, t, NEG_INF) for t in s]
    m = [jnp.max(t, axis=-1, keepdims=True) for t in s]
    p = [jnp.exp(t - mm) for t, mm in zip(s, m)]
    ol = [jnp.dot(pp, jnp.concatenate([v_ref[r, pl.ds(start, kw), sl], ones], axis=1), preferred_element_type=F32)
          for pp, (r, sl) in zip(p, work)]
    for r in range(q_ref.shape[0]):
        lse_row = jnp.zeros((qt, LANES), F32)
        for h in range(heads):
            i = r * heads + h
            o_ref[r, :, work[i][1]] = (ol[i][:, :hd] / ol[i][:, hd:]).astype(o_ref.dtype)
            lse_row = jnp.where(head_of_lane == h, m[i] + jnp.log(ol[i][:, hd:hd + 1]), lse_row)
        lse_ref[r] = lse_row


DIL_ATTN_ROWS = 512
A_PAIR = 2 * A_HEAD_DIM


def _head_helper_matrices():
    i = jnp.arange(A_PAIR)
    same_head = (i[:, None] // A_HEAD_DIM) == (i[None, :] // A_HEAD_DIM)
    dst = i % A_HEAD_DIM
    src = jnp.where(dst < ROT_HALF, i + ROT_HALF, i - ROT_HALF)
    perm = (i[:, None] == src[None, :]) & (dst[None, :] < ROT_DIM)
    return same_head.astype(BF16), perm.astype(BF16)


def _qkv_group_kernel(x_ref, g_ref, w_ref, cs_ref, sn_ref, gain_ref, mean_ref, perm_ref, o_ref, xn_ref, acc_ref, *, dil):
    j = pl.program_id(1)

    @pl.when(j == 0)
    def _():
        xn_ref[...] = _rms(x_ref[...], g_ref[...]).astype(BF16)

    acc = jnp.dot(xn_ref[...], w_ref[...], preferred_element_type=F32)
    tm, width = acc.shape

    def emit(vals):
        if dil == 1:
            o_ref[0, 0] = vals.astype(o_ref.dtype)
            return
        chunks = width // LANES
        for c in range(chunks):
            acc_ref[c] = vals[:, c * LANES:(c + 1) * LANES]
        for r in range(dil):
            o_ref[0, r] = jnp.concatenate([acc_ref[c, pl.ds(r, tm // dil, stride=dil), :] for c in range(chunks)],
                                          axis=-1).astype(o_ref.dtype)

    @pl.when(j < 2)
    def _():
        cs = jnp.concatenate([cs_ref[...]] * 2, axis=1)
        sn = jnp.concatenate([sn_ref[...]] * 2, axis=1)
        gain = gain_ref[pl.ds(j, 1), :]
        parts = []
        for p in range(width // A_PAIR):
            a = acc[:, p * A_PAIR:(p + 1) * A_PAIR]
            meansq = jnp.dot((a * a).astype(BF16), mean_ref[...], preferred_element_type=F32)
            ag = a * gain[:, p * A_PAIR:(p + 1) * A_PAIR]
            pair = jnp.dot(ag.astype(BF16), perm_ref[...], preferred_element_type=F32)
            parts.append((ag * cs + pair * sn) * lax.rsqrt(meansq + NORM_EPS))
        emit(jnp.concatenate(parts, axis=1))

    @pl.when(j == 2)
    def _():
        emit(acc)


def qkv_group_projection(x, ln, wqkv, cs, sn, gq, gk, group, batch, s_len, width):
    t, d = x.shape
    dil = DILATED_PAIRS[group][1]
    n_groups = len(DILATED_PAIRS)
    heads = width // A_HEAD_DIM
    tm = _row_tile(s_len, 1024)
    tiles = s_len // tm
    assert tm % (16 * dil) == 0
    gains = jnp.stack([jnp.tile(gq, heads) * A_HEAD_DIM ** -0.5, jnp.tile(gk, heads)])
    ones, perm = _head_helper_matrices()
    ones = (ones.astype(F32) * (1.0 / A_HEAD_DIM)).astype(BF16)
    const = lambda a: pl.BlockSpec(a.shape, lambda i, j: (0, 0))
    return pl.pallas_call(
        functools.partial(_qkv_group_kernel, dil=dil),
        grid=(t // tm, 3),
        in_specs=[pl.BlockSpec((tm, d), lambda i, j: (i, 0)),
                  pl.BlockSpec((1, d), lambda i, j: (0, 0)),
                  pl.BlockSpec((d, width), lambda i, j: (0, j * n_groups + group)),
                  pl.BlockSpec((tm, LANES), lambda i, j: (i, 0)),
                  pl.BlockSpec((tm, LANES), lambda i, j: (i, 0)),
                  const(gains), const(ones), const(perm)],
        out_specs=pl.BlockSpec((1, dil, tm // dil, width), lambda i, j: (i // tiles, 0, i % tiles, j)),
        out_shape=jax.ShapeDtypeStruct((batch, dil, s_len // dil, 3 * width), BF16),
        scratch_shapes=[pltpu.VMEM((tm, d), BF16), pltpu.VMEM((width // LANES, tm, LANES), F32)],
        compiler_params=_params("parallel", "arbitrary"),
        name=f"qkv_projection_g{group}",
    )(x, ln.reshape(1, d), wqkv, cs, sn, gains, ones, perm)


def dilated_group_attention(qkv, group, batch, s_len, heads):
    window, dil = DILATED_PAIRS[group]
    half = window // (2 * dil)
    seq = s_len // dil
    width = heads * A_HEAD_DIM
    qt = min(256, seq)
    kw = min(qt + 2 * half, seq)
    assert seq % qt == 0 and (seq - kw) % math.gcd(half, qt) == 0
    rb = max(1, min(dil, DIL_ATTN_ROWS // qt))
    assert dil % rb == 0
    whole = lambda c: pl.BlockSpec((None, rb, seq, width), lambda b, r, qi: (b, r, 0, c))
    tile = lambda n: pl.BlockSpec((None, rb, qt, n), lambda b, r, qi: (b, r, qi, 0))
    kern = functools.partial(_dil_attn_kernel, seq=seq, qt=qt, kw=kw, half=half, heads=heads)
    return pl.pallas_call(
        kern,
        grid=(batch, dil // rb, seq // qt),
        in_specs=[tile(width), whole(1), whole(2)],
        out_specs=[tile(width), tile(LANES)],
        out_shape=[jax.ShapeDtypeStruct((batch, dil, seq, width), BF16),
                   jax.ShapeDtypeStruct((batch, dil, seq, LANES), F32)],
        compiler_params=_params("parallel", "parallel", "parallel"),
        name=f"dilated_attention_g{group}",
    )(qkv, qkv, qkv)


def _combine_wo_kernel(x_ref, o0_ref, o1_ref, o2_ref, l0_ref, l1_ref, l2_ref, w_ref, out_ref, ot_ref, lt_ref, *, heads):
    hd = A_HEAD_DIM
    rep = LANES // heads
    tm = x_ref.shape[0]

    def token_order(src_ref, dst_ref, g):
        dil = src_ref.shape[0]
        if dil == 1:
            return src_ref[0].astype(F32)
        chunks = src_ref.shape[-1] // LANES
        for r in range(dil):
            blk = src_ref[r].astype(F32)
            for c in range(chunks):
                dst_ref[g, c, pl.ds(r, tm // dil, stride=dil), :] = blk[:, c * LANES:(c + 1) * LANES]
        return jnp.concatenate([dst_ref[g, c] for c in range(chunks)], axis=-1)

    ls = [token_order(ref, lt_ref, g) for g, ref in enumerate((l0_ref, l1_ref, l2_ref))]
    os_ = [token_order(ref, ot_ref, g) for g, ref in enumerate((o0_ref, o1_ref, o2_ref))]
    m = jnp.maximum(jnp.maximum(ls[0], ls[1]), ls[2])
    es = [jnp.exp(l - m) for l in ls]
    inv = 1.0 / (es[0] + es[1] + es[2])
    al = [e * inv for e in es]
    parts = []
    for h in range(heads):
        sl = slice(h * hd, (h + 1) * hd)
        c = slice(h * rep, h * rep + 1)
        parts.append((al[0][:, c] * os_[0][:, sl] + al[1][:, c] * os_[1][:, sl] + al[2][:, c] * os_[2][:, sl]).astype(BF16))
    o = jnp.concatenate(parts, axis=-1)
    out_ref[...] = x_ref[...] + jnp.dot(o, w_ref[...], preferred_element_type=F32)


def combine_wo(x, outs, lses, wo, heads, batch, s_len):
    t, d = x.shape
    width = wo.shape[0]
    n_groups = len(DILATED_PAIRS)
    tm = _row_tile(s_len, 512)
    tiles = s_len // tm
    assert all(tm % (16 * dil) == 0 for _, dil in DILATED_PAIRS)
    row = lambda n: pl.BlockSpec((tm, n), lambda i: (i, 0))
    res = lambda g, n: pl.BlockSpec((None, DILATED_PAIRS[g][1], tm // DILATED_PAIRS[g][1], n),
                                    lambda i: (i // tiles, 0, i % tiles, 0))
    return pl.pallas_call(
        functools.partial(_combine_wo_kernel, heads=heads),
        grid=(t // tm,),
        in_specs=[row(d)] + [res(g, width) for g in range(n_groups)] + [res(g, LANES) for g in range(n_groups)]
                 + [pl.BlockSpec((width, d), lambda i: (0, 0))],
        out_specs=row(d),
        out_shape=jax.ShapeDtypeStruct((t, d), F32),
        scratch_shapes=[pltpu.VMEM((n_groups, width // LANES, tm, LANES), F32),
                        pltpu.VMEM((n_groups, 1, tm, LANES), F32)],
        compiler_params=_params("parallel"),
        name="combine_wo",
    )(x, *outs, *lses, wo)


def mixer_a(x, ln, wqkv, qk_norm, wo, cs, sn, batch, s_len):
    heads = wo.shape[0] // A_HEAD_DIM
    wqkv = wqkv.astype(BF16)
    outs, lses = [], []
    for g in range(len(DILATED_PAIRS)):
        qkv = qkv_group_projection(x, ln, wqkv, cs, sn, qk_norm[0, g], qk_norm[1, g], g, batch, s_len, wo.shape[0])
        o, lse = dilated_group_attention(qkv, g, batch, s_len, heads)
        outs.append(o)
        lses.append(lse)
    return combine_wo(x, outs, lses, wo.astype(BF16), heads, batch, s_len)


def _ffn_kernel(x_ref, g_ref, w1_ref, w3_ref, w2_ref, o_ref, xn_ref, acc_ref):
    j = pl.program_id(1)

    @pl.when(j == 0)
    def _():
        xn_ref[...] = _rms(x_ref[...], g_ref[...]).astype(BF16)
        acc_ref[...] = jnp.zeros_like(acc_ref)

    acc_ref[...] += _swiglu_chunks(xn_ref[...], w1_ref, w3_ref, w2_ref)

    @pl.when(j == pl.num_programs(1) - 1)
    def _():
        o_ref[...] = x_ref[...] + acc_ref[...]


FFN_SUB = 256


def _swiglu_chunks(xn, w1_ref, w3_ref, w2_ref):
    tf = w2_ref.shape[0]
    assert tf % FFN_SUB == 0
    total = None
    for s in range(tf // FFN_SUB):
        sl = slice(s * FFN_SUB, (s + 1) * FFN_SUB)
        gate = jnp.dot(xn, w1_ref[:, sl], preferred_element_type=F32)
        up = jnp.dot(xn, w3_ref[:, sl], preferred_element_type=F32)
        act = (gate * jax.nn.sigmoid(gate) * up).astype(BF16)
        part = jnp.dot(act, w2_ref[sl, :], preferred_element_type=F32)
        total = part if total is None else total + part
    return total


def dense_ffn(x, ln, w13, w2):
    t, d = x.shape
    tf = dff = w2.shape[0]
    nj = dff // tf
    tm = _row_tile(t, 512)
    resident = dict(pipeline_mode=pl.Buffered(1))
    return pl.pallas_call(
        _ffn_kernel,
        grid=(t // tm, nj),
        in_specs=[pl.BlockSpec((tm, d), lambda i, j: (i, 0)),
                  pl.BlockSpec((1, d), lambda i, j: (0, 0)),
                  pl.BlockSpec((d, tf), lambda i, j: (0, j), **resident),
                  pl.BlockSpec((d, tf), lambda i, j: (0, nj + j), **resident),
                  pl.BlockSpec((tf, d), lambda i, j: (j, 0), **resident)],
        out_specs=pl.BlockSpec((tm, d), lambda i, j: (i, 0)),
        out_shape=jax.ShapeDtypeStruct((t, d), F32),
        scratch_shapes=[pltpu.VMEM((tm, d), BF16), pltpu.VMEM((tm, d), F32)],
        compiler_params=_params("parallel", "arbitrary"),
        name="dense_ffn",
    )(x, ln.reshape(1, d), w13, w13, w2)


C_ROPE = ROT_DIM
C_NOPE = 64
C_V = 64
C_QK = C_ROPE + C_NOPE
LOG2_E = math.log2(math.e)


def _mla_qkv_kernel(lat_ref, qa_ref, kva_ref, gq_ref, gk_ref, wq_ref, wkv_ref, cs_ref, sn_ref,
                    ones_ref, perm_ref, q_ref, k_ref, vt_ref, *, heads, q_lora, kv_lora):
    two = lambda a: jnp.concatenate([a, a], axis=1)
    cs, sn = two(cs_ref[...]), two(sn_ref[...])
    lat = lat_ref[...]
    qn = _rms(lat[:, :q_lora], qa_ref[...]).astype(BF16)
    kvn = _rms(lat[:, q_lora:q_lora + kv_lora], kva_ref[...]).astype(BF16)
    k_rope = two(lat[:, q_lora + kv_lora:])

    def head_norm(a, g, post):
        sumsq = jnp.dot((a * a).astype(BF16), ones_ref[...], preferred_element_type=F32)
        ag = a * g
        pair = jnp.dot(ag.astype(BF16), perm_ref[...], preferred_element_type=F32)
        return (ag * cs + pair * sn) * (lax.rsqrt(sumsq * (1.0 / C_QK) + NORM_EPS) * post)

    gq, gk = two(gq_ref[...]), two(gk_ref[...])
    kv = jnp.dot(kvn, wkv_ref[...], preferred_element_type=F32)
    for p in range(heads * LANES // A_PAIR):
        sl = slice(p * A_PAIR, (p + 1) * A_PAIR)
        q = jnp.dot(qn, wq_ref[:, sl], preferred_element_type=F32)
        q_ref[:, sl] = head_norm(q, gq, C_QK ** -0.5 * LOG2_E).astype(BF16)
        k_ref[:, sl] = head_norm(kv[:, sl] + k_rope, gk, 1.0).astype(BF16)
    vt_ref[...] = kv[:, heads * LANES:].T.astype(BF16)


def _mla_attn_kernel(q_ref, k_ref, vt_ref, o_ref, *, pack):
    ones = jnp.ones((16, k_ref.shape[1]), BF16)
    tq = q_ref.shape[1]
    halves = 2 if tq % (2 * LANES) == 0 else 1
    qw = tq // halves
    work = [(i, j) for i in range(pack) for j in range(halves)]
    st = [lax.dot_general(k_ref[0, :, i * LANES:(i + 1) * LANES], q_ref[0, j * qw:(j + 1) * qw, i * LANES:(i + 1) * LANES],
                          (((1,), (1,)), ((), ())), preferred_element_type=F32).astype(BF16) for i, j in work]
    m = [jnp.max(t, axis=0, keepdims=True) for t in st]
    p = [jnp.exp2(t - mm) for t, mm in zip(st, m)]
    vts = [jnp.concatenate([vt_ref[0, i * C_V:(i + 1) * C_V, :], ones], axis=0) for i in range(pack)]
    ol = [jnp.dot(vts[i], pp, preferred_element_type=F32) for (i, j), pp in zip(work, p)]
    for (i, j), t in zip(work, ol):
        o_ref[0, i * C_V:(i + 1) * C_V, j * qw:(j + 1) * qw] = (t[:C_V] / t[C_V:C_V + 1]).astype(o_ref.dtype)


def _matmul_res_t_kernel(at_ref, w_ref, x_ref, o_ref):
    o_ref[...] = x_ref[...] + lax.dot_general(at_ref[...], w_ref[...], (((0,), (0,)), ((), ())),
                                              preferred_element_type=F32)


def matmul_residual_t(a_t, w, x, batch, s_len):
    k, n = w.shape
    tm = _row_tile(s_len, 512)
    tiles = s_len // tm
    return pl.pallas_call(
        _matmul_res_t_kernel,
        grid=(batch * tiles,),
        in_specs=[pl.BlockSpec((None, k, tm), lambda i: (i // tiles, 0, i % tiles)),
                  pl.BlockSpec((k, n), lambda i: (0, 0)),
                  pl.BlockSpec((tm, n), lambda i: (i, 0))],
        out_specs=pl.BlockSpec((tm, n), lambda i: (i, 0)),
        out_shape=jax.ShapeDtypeStruct(x.shape, F32),
        compiler_params=_params("parallel"),
        name="matmul_residual_t",
    )(a_t, w, x)


def _pad_cols(w, heads, real, slot):
    k = w.shape[0]
    w = w.reshape(k, heads, real)
    return jnp.pad(w, ((0, 0), (0, 0), (0, slot - real))).reshape(k, heads * slot)


def mixer_c(x, ln, wa, qa_norm, kva_norm, wq_b, wkv_b, qk_norm, wo, cs, sn, batch, s_len):
    t, d = x.shape
    q_lora, kv_lora = qa_norm.shape[0], kva_norm.shape[0]
    heads = wo.shape[0] // C_V
    lat_w = q_lora + kv_lora + LANES
    wa_p = jnp.pad(wa, ((0, 0), (0, lat_w - wa.shape[1]))).astype(BF16)
    lat = norm_matmul(x, ln, wa_p, F32, tn=lat_w)
    wq_p = _pad_cols(wq_b, heads, C_QK, LANES).astype(BF16)
    wkv = wkv_b.reshape(kv_lora, heads, C_NOPE + C_V)
    wk_p = jnp.pad(wkv[:, :, :C_NOPE], ((0, 0), (0, 0), (C_ROPE, LANES - C_QK))).reshape(kv_lora, heads * LANES)
    wv = wkv[:, :, C_NOPE:].reshape(kv_lora, heads * C_V)
    wkv_p = jnp.concatenate([wk_p, wv], axis=1).astype(BF16)
    gq = jnp.pad(qk_norm[0], (0, LANES - C_QK)).reshape(1, LANES)
    gk = jnp.pad(qk_norm[1], (0, LANES - C_QK)).reshape(1, LANES)
    tm = _row_tile(s_len, 512)
    tiles = s_len // tm
    row = lambda n: pl.BlockSpec((tm, n), lambda i: (i, 0))
    full = lambda a: pl.BlockSpec(a.shape, lambda i: (0, 0))
    qa2, kva2 = qa_norm.reshape(1, -1), kva_norm.reshape(1, -1)
    ones, perm = _head_helper_matrices()
    q, k, vt = pl.pallas_call(
        functools.partial(_mla_qkv_kernel, heads=heads, q_lora=q_lora, kv_lora=kv_lora),
        grid=(t // tm,),
        in_specs=[row(lat_w), full(qa2), full(kva2), full(gq), full(gk), full(wq_p), full(wkv_p),
                  row(LANES), row(LANES), full(ones), full(perm)],
        out_specs=[row(heads * LANES), row(heads * LANES),
                   pl.BlockSpec((None, heads * C_V, tm), lambda i: (i // tiles, 0, i % tiles))],
        out_shape=[jax.ShapeDtypeStruct((t, heads * LANES), BF16),
                   jax.ShapeDtypeStruct((t, heads * LANES), BF16),
                   jax.ShapeDtypeStruct((batch, heads * C_V, s_len), BF16)],
        compiler_params=_params("parallel"),
        name="mla_qkv",
    )(lat, qa2, kva2, gq, gk, wq_p, wkv_p, cs, sn, ones, perm)
    pack = LANES // C_V
    tq = min(512, s_len)
    o_t = pl.pallas_call(
        functools.partial(_mla_attn_kernel, pack=pack),
        grid=(batch, heads // pack, s_len // tq),
        in_specs=[pl.BlockSpec((1, tq, pack * LANES), lambda b, h, i: (b, i, h)),
                  pl.BlockSpec((1, s_len, pack * LANES), lambda b, h, i: (b, 0, h)),
                  pl.BlockSpec((1, pack * C_V, s_len), lambda b, h, i: (b, h, 0))],
        out_specs=pl.BlockSpec((1, pack * C_V, tq), lambda b, h, i: (b, h, i)),
        out_shape=jax.ShapeDtypeStruct((batch, heads * C_V, s_len), BF16),
        compiler_params=_params("parallel", "parallel", "arbitrary"),
        name="mla_attention",
    )(q.reshape(batch, s_len, -1), k.reshape(batch, s_len, -1), vt)
    return matmul_residual_t(o_t, wo.astype(BF16), x, batch, s_len)


B_HEAD_DIM = 64
B_DECAY_SCALE = math.exp(-0.5)
B_GN_EPS = 64e-5
WKV_CHUNK = 64
WKV_HEADS_PER_STEP = 16
WKV_SEQS_PER_STEP = 1
WKV_PACK = 4


def _bdot(a, b):
    return jnp.dot(a.astype(BF16), b.astype(BF16), preferred_element_type=F32)


def _rwkv_prep_kernel(x_ref, xp_ref, xn_ref, ln_ref, mu_ref, wrkv_ref, g1_ref, g2_ref, w0_ref, w1_ref, w2_ref,
                      a0_ref, a1_ref, a2_ref, r_ref, k_ref, v_ref, g_ref, lw_ref, as_ref, *, tiles_per_seq):
    i = pl.program_id(0)
    tm = x_ref.shape[0]
    ln = ln_ref[...]
    h = _rms(x_ref[...], ln)
    pos = i % tiles_per_seq
    prev_row = jnp.where(pos == 0, 0.0, _rms(xp_ref[7:8, :], ln))
    next_row = jnp.where(pos == tiles_per_seq - 1, 0.0, _rms(xn_ref[0:1, :], ln))
    row = lax.broadcasted_iota(I32, (tm, 1), 0)
    d_f = jnp.where(row == 0, prev_row, pltpu.roll(h, 1, 0)) - h
    d_b = jnp.where(row == tm - 1, next_row, pltpu.roll(h, tm - 1, 0)) - h
    d_c = 0.5 * (d_f + d_b)
    mu = mu_ref[...]
    mix = lambda dd, n: h + dd * mu[n:n + 1]
    r_ref[...] = _bdot(mix(d_c, 0), wrkv_ref[0]).astype(r_ref.dtype)
    k_ref[...] = _bdot(mix(d_c, 1), wrkv_ref[1]).astype(k_ref.dtype)
    v_ref[...] = _bdot(mix(d_c, 2), wrkv_ref[2]).astype(v_ref.dtype)
    g_ref[...] = _bdot(jax.nn.sigmoid(_bdot(mix(d_c, 3), g1_ref[...])), g2_ref[...]).astype(g_ref.dtype)
    for dr, dd in enumerate((d_f, d_b)):
        z = w0_ref[dr:dr + 1, :] + _bdot(jnp.tanh(_bdot(mix(dd, 4 + dr), w1_ref[dr])), w2_ref[dr])
        lw_ref[dr] = -B_DECAY_SCALE * jax.nn.sigmoid(z)
        aa = a0_ref[dr:dr + 1, :] + _bdot(_bdot(mix(dd, 6 + dr), a1_ref[dr]), a2_ref[dr])
        as_ref[dr] = jax.nn.sigmoid(aa).astype(as_ref.dtype)


def _wkv_kernel(rf_ref, kf_ref, vf_ref, rb_ref, kb_ref, vb_ref, lwf_ref, lwb_ref, asf_ref, asb_ref, kk_ref, ka_ref,
                yf_ref, yb_ref, st_ref, *, hb):
    n = B_HEAD_DIM
    nb, c = rf_ref.shape[0], rf_ref.shape[1]

    @pl.when(pl.program_id(1) == 0)
    def _():
        st_ref[...] = jnp.zeros_like(st_ref)

    nt = (((1,), (1,)), ((), ()))
    tn = (((0,), (0,)), ((), ()))
    mm = (((1,), (0,)), ((), ()))
    low = lax.broadcasted_iota(I32, (1, LANES), 1) < n
    pk = WKV_PACK
    gw = pk * n
    per_dir = hb // pk
    gsl = [slice(g * gw, (g + 1) * gw) for g in range(per_dir)]
    cut = lambda a: [a[:, sl] for sl in gsl]
    row_p = lax.broadcasted_iota(I32, (c, gw), 0)
    col_p = lax.broadcasted_iota(I32, (c, gw), 1) % n

    def operands(sgn, bi, r_ref, k_ref, v_ref, lw_ref, as_ref):
        r_ref, k_ref, v_ref, lw_ref, as_ref = (ref.at[bi] for ref in (r_ref, k_ref, v_ref, lw_ref, as_ref))
        ahead = (lax.broadcasted_iota(I32, (c, c), 0) - lax.broadcasted_iota(I32, (c, c), 1)) * sgn
        lw = lw_ref[...]
        cl = jnp.dot((ahead >= 0).astype(F32), lw, preferred_element_type=F32, precision=lax.Precision.HIGHEST)
        total = jnp.sum(lw, axis=0, keepdims=True)
        e_in, e_ex, e_ng, e_end = jnp.exp(cl), jnp.exp(cl - lw), jnp.exp(-cl), jnp.exp(total - cl)
        rr, kk_raw, vv, asig = (ref[...].astype(F32) for ref in (r_ref, k_ref, v_ref, as_ref))
        kk_all = kk_raw * kk_ref[...]
        kdir = kk_raw * (1.0 + (asig - 1.0) * ka_ref[...])
        sq = kk_all * kk_all
        norms = []
        for p in range(sq.shape[1] // LANES):
            slab = sq[:, p * LANES:(p + 1) * LANES]
            s_lo = jnp.sum(jnp.where(low, slab, 0.0), axis=-1, keepdims=True)
            s_hi = jnp.sum(jnp.where(low, 0.0, slab), axis=-1, keepdims=True)
            norms.append(jnp.where(low, s_lo, s_hi))
        kkn_all = kk_all * lax.rsqrt(jnp.maximum(jnp.concatenate(norms, axis=-1), 1e-24))
        beta_all = kkn_all * asig
        ahead_p = (row_p - col_p) * sgn
        return dict(
            a_t=cut((-kkn_all * e_ex).astype(BF16)), b_t=cut((beta_all * e_ng).astype(BF16)),
            rt=cut((rr * e_in).astype(BF16)), kt=cut((kdir * e_ng).astype(BF16)), v=cut(vv.astype(BF16)),
            b_h=cut((beta_all * e_end).astype(BF16)), k_hb=cut((kdir * e_end).astype(BF16)),
            p_end=cut(jnp.exp(total)), strict=[ahead_p > 0] * per_dir, incl=[ahead_p >= 0] * per_dir)

    both = None
    for bi in range(nb):
        for part in (operands(1, bi, rf_ref, kf_ref, vf_ref, lwf_ref, asf_ref),
                     operands(-1, bi, rb_ref, kb_ref, vb_ref, lwb_ref, asb_ref)):
            both = part if both is None else {key: both[key] + part[key] for key in part}
    a_t, b_t, rt, kt, v, b_h, k_hb = (both[key] for key in ("a_t", "b_t", "rt", "kt", "v", "b_h", "k_hb"))
    p_end, strict_p, incl_p = both["p_end"], both["strict"], both["incl"]
    groups = range(nb * 2 * per_dir)

    diag_block = ((lax.broadcasted_iota(I32, (gw, gw), 0) // n)
                  == (lax.broadcasted_iota(I32, (gw, gw), 1) // n))
    diag_bf = diag_block.astype(BF16)

    def bd(p):
        return jnp.concatenate([p] * pk, axis=0) * diag_bf

    def fold(full):
        kept = jnp.where(diag_block, full, 0.0)
        out = kept[:n]
        for h in range(1, pk):
            out = out + kept[h * n:(h + 1) * n]
        return out

    eye_p = (row_p == col_p).astype(F32)
    same_block = lambda m: (row_p // m) == (col_p // m)
    dots = lambda a, b, dims: [lax.dot_general(p, q, dims, preferred_element_type=F32) for p, q in zip(a, b)]
    rows = lambda *parts: [jnp.concatenate(list(ps), axis=0) for ps in zip(*parts)]
    bf = lambda xs: [t.astype(BF16) for t in xs]
    bds = lambda xs: [bd(t) for t in xs]
    gram = dots(rows(a_t, rt), rows(bds(b_t), bds(kt)), nt)
    l_ab = [jnp.where(s, g[:c, :gw], 0.0) for s, g in zip(strict_p, gram)]
    l_ak = [jnp.where(s, g[:c, gw:], 0.0).astype(BF16) for s, g in zip(strict_p, gram)]
    m_r = [jnp.concatenate([jnp.where(i, g[c:, :gw], 0.0), jnp.where(i, g[c:, gw:], 0.0)],
                           axis=1).astype(BF16) for i, g in zip(incl_p, gram)]
    in_pair = same_block(2)
    x = [eye_p + jnp.where(in_pair, l, 0.0) for l in l_ab]
    m = 2
    while m < c:
        level = same_block(2 * m) & jnp.logical_not(same_block(m))
        xb = bf(x)
        l_m = [jnp.where(level, l, 0.0).astype(BF16) for l in l_ab]
        xl = bf(dots(xb, bds(l_m), mm))
        x = [xx + t for xx, t in zip(x, dots(xl, bds(xb), mm))]
        m *= 2
    s0 = [st_ref[g] for g in groups]
    bd_s = bds(bf(s0))
    bd_v = bds(v)
    w = [p + q for p, q in zip(dots(a_t, bd_s, nt), dots(l_ak, bd_v, mm))]
    u = bf(dots(bf(x), bds(bf(w)), mm))
    y = [p + q for p, q in zip(dots(rt, bd_s, nt), dots(m_r, rows(bds(u), bd_v), mm))]
    s_new = dots(rows(u, v), rows(b_h, k_hb), tn)
    for g in groups:
        st_ref[g] = s0[g] * p_end[g] + fold(s_new[g])
        bi, within = divmod(g, 2 * per_dir)
        out_ref = yf_ref if within < per_dir else yb_ref
        out_ref[bi, :, gsl[within % per_dir]] = y[g]


def _rwkv_out_kernel(x_ref, yf_ref, yb_ref, r_ref, k_ref, v_ref, g_ref, as_ref, ka_ref, rk_ref, lnx_ref, wo_ref,
                     sum_ref, o_ref):
    d = x_ref.shape[1]
    n = B_HEAD_DIM
    y = yf_ref[...] + yb_ref[...]
    a_sum = as_ref[0].astype(F32) + as_ref[1].astype(F32)
    kd = k_ref[...].astype(F32) * (2.0 + (a_sum - 2.0) * ka_ref[...])
    prod = r_ref[...].astype(F32) * kd * rk_ref[...]
    slab = sum_ref.shape[0]

    def head_sum(a):
        return jnp.dot(a.astype(BF16), sum_ref[...], preferred_element_type=F32)

    parts = []
    for p in range(d // slab):
        sl = slice(p * slab, (p + 1) * slab)
        ys = y[:, sl]
        cen = ys - head_sum(ys) * (1.0 / n)
        var = head_sum(cen * cen) * (1.0 / n)
        yn = cen * lax.rsqrt(var + B_GN_EPS) * lnx_ref[0:1, sl] + lnx_ref[1:2, sl]
        bonus = head_sum(prod[:, sl]) * v_ref[:, sl]
        parts.append(((yn + bonus) * g_ref[:, sl]).astype(BF16))
    o_ref[...] = x_ref[...] + jnp.dot(jnp.concatenate(parts, axis=-1), wo_ref[...], preferred_element_type=F32)


def mixer_b(x, ln, mu, w_rkv, w0, w1, w2, a0, a1, a2, g1, g2, k_k, k_a, r_k, ln_x, wo, batch, s_len):
    t, d = x.shape
    bf = lambda a: a.astype(BF16)
    halo = 8
    full = lambda a: pl.BlockSpec(a.shape, lambda i: (0,) * a.ndim)
    resident = lambda a: pl.BlockSpec(a.shape, lambda i: (0,) * a.ndim, pipeline_mode=pl.Buffered(1))
    ln2 = ln.reshape(1, d)
    weights = (ln2, mu, bf(w_rkv), bf(g1), bf(g2), w0, bf(w1), bf(w2), a0, bf(a1), bf(a2))
    tok = lambda dt: jax.ShapeDtypeStruct((t, d), dt)
    tok2 = lambda dt: jax.ShapeDtypeStruct((2, t, d), dt)
    tm = _row_tile(s_len, 512)
    tiles_per_seq = s_len // tm
    row = lambda n: pl.BlockSpec((tm, n), lambda i: (i, 0))
    row2 = lambda n: pl.BlockSpec((2, tm, n), lambda i: (0, i, 0))
    r, k, v, g, lw, asig = pl.pallas_call(
        functools.partial(_rwkv_prep_kernel, tiles_per_seq=tiles_per_seq),
        grid=(t // tm,),
        in_specs=[row(d),
                  pl.BlockSpec((halo, d), lambda i: (jnp.maximum(i * (tm // halo) - 1, 0), 0)),
                  pl.BlockSpec((halo, d), lambda i: (jnp.minimum((i + 1) * (tm // halo), t // halo - 1), 0))]
                 + [resident(a) for a in weights],
        out_specs=[row(d), row(d), row(d), row(d), row2(d), row2(d)],
        out_shape=[tok(BF16), tok(BF16), tok(BF16), tok(BF16), tok2(F32), tok2(BF16)],
        compiler_params=_params("parallel"),
        name="rwkv_prep",
    )(x, x, x, *weights)
    tm = _row_tile(s_len, 256)
    row = lambda n: pl.BlockSpec((tm, n), lambda i: (i, 0))
    row2 = lambda n: pl.BlockSpec((2, tm, n), lambda i: (0, i, 0))

    hb = WKV_HEADS_PER_STEP
    width = hb * B_HEAD_DIM
    c = min(WKV_CHUNK, s_len)
    nc = s_len // c
    assert width == d
    nb = math.gcd(batch, WKV_SEQS_PER_STEP)
    tok_f = pl.BlockSpec((nb, c, d), lambda b, ci: (b, ci, 0))
    tok_b = pl.BlockSpec((nb, c, d), lambda b, ci: (b, nc - 1 - ci, 0))
    dir_f = pl.BlockSpec((None, nb, c, d), lambda b, ci: (0, b, ci, 0))
    dir_b = pl.BlockSpec((None, nb, c, d), lambda b, ci: (1, b, nc - 1 - ci, 0))
    par = pl.BlockSpec((1, d), lambda b, ci: (0, 0))
    v3 = lambda a: a.reshape(batch, s_len, d)
    v4 = lambda a: a.reshape(2, batch, s_len, d)
    seq_out = jax.ShapeDtypeStruct((batch, s_len, d), F32)
    y_f, y_b = pl.pallas_call(
        functools.partial(_wkv_kernel, hb=hb),
        grid=(batch // nb, nc),
        in_specs=[tok_f, tok_f, tok_f, tok_b, tok_b, tok_b, dir_f, dir_b, dir_f, dir_b, par, par],
        out_specs=[tok_f, tok_b],
        out_shape=[seq_out, seq_out],
        scratch_shapes=[pltpu.VMEM((nb * 2 * hb // WKV_PACK, B_HEAD_DIM, WKV_PACK * B_HEAD_DIM), F32)],
        compiler_params=_params("parallel", "arbitrary"),
        name="wkv_scan",
    )(v3(r), v3(k), v3(v), v3(r), v3(k), v3(v), v4(lw), v4(lw), v4(asig), v4(asig),
      k_k.reshape(1, d), k_a.reshape(1, d))

    lane = jnp.arange(WKV_PACK * B_HEAD_DIM)
    same_head = ((lane[:, None] // B_HEAD_DIM) == (lane[None, :] // B_HEAD_DIM)).astype(BF16)
    params = (k_a.reshape(1, d), r_k.reshape(1, d), ln_x, bf(wo), same_head)
    return pl.pallas_call(
        _rwkv_out_kernel,
        grid=(t // tm,),
        in_specs=[row(d), row(d), row(d), row(d), row(d), row(d), row(d), row2(d)] + [full(a) for a in params],
        out_specs=row(d),
        out_shape=tok(F32),
        compiler_params=_params("parallel"),
        name="rwkv_out",
    )(x, y_f.reshape(t, d), y_b.reshape(t, d), r, k, v, g, asig, *params)


TOP_K = 2
ROUTE_EXPERT, ROUTE_GATE, ROUTE_RANK = 0, TOP_K, 2 * TOP_K
MOE_ROW_TILE = 512
ROW_DMA_UNROLL = 8


def _router_kernel(x_ref, g_ref, wr_ref, hn_ref, route_ref, cnt_ref, run_ref, *, n_exp):
    @pl.when(pl.program_id(0) == 0)
    def _():
        run_ref[...] = jnp.zeros_like(run_ref)

    tm = x_ref.shape[0]
    h = _rms(x_ref[...], g_ref[...])
    hn_ref[...] = h
    logits = jnp.dot(h, wr_ref[...], preferred_element_type=F32, precision=lax.Precision.HIGHEST)
    lane = lax.broadcasted_iota(I32, (tm, LANES), 1)
    logits = jnp.where(lane < n_exp, logits, NEG_INF)

    def top1(v):
        m = jnp.max(v, axis=-1, keepdims=True)
        return m, jnp.min(jnp.where(v == m, lane, LANES), axis=-1, keepdims=True)

    m1, i1 = top1(logits)
    m2, i2 = top1(jnp.where(lane == i1, NEG_INF, logits))
    e = jnp.exp(m2 - m1)
    g1 = 1.0 / (1.0 + e)
    g2 = e * g1
    sel = ((lane == i1) | (lane == i2)).astype(F32)
    earlier = (lax.broadcasted_iota(I32, (tm, tm), 0) > lax.broadcasted_iota(I32, (tm, tm), 1)).astype(BF16)
    before = jnp.dot(earlier, sel.astype(BF16), preferred_element_type=F32) + run_ref[...]
    r1 = jnp.sum(jnp.where(lane == i1, before, 0.0), axis=-1, keepdims=True)
    r2 = jnp.sum(jnp.where(lane == i2, before, 0.0), axis=-1, keepdims=True)
    run_ref[...] += jnp.sum(sel, axis=0, keepdims=True)
    cnt_ref[...] = run_ref[...]
    route = jnp.zeros((tm, LANES), F32)
    for off, vals in ((ROUTE_EXPERT, (i1.astype(F32), i2.astype(F32))), (ROUTE_GATE, (g1, g2)), (ROUTE_RANK, (r1, r2))):
        for kk, v in enumerate(vals):
            route = jnp.where(lane == off + kk, v, route)
    route_ref[...] = route


def _dispatch_kernel(dest_ref, hn_ref, xs_in_ref, xs_ref, sem):
    del xs_in_ref
    tm = hn_ref.shape[0]

    def row_copy(r, slot):
        return pltpu.make_async_copy(hn_ref.at[pl.ds(r, 1)], xs_ref.at[pl.ds(dest_ref[TOP_K * r + slot], 1)], sem)

    def start(r, c):
        for slot in range(TOP_K):
            row_copy(r, slot).start(priority=slot % 2)
        return c

    def wait(r, c):
        for slot in range(TOP_K):
            row_copy(r, slot).wait()
        return c

    lax.fori_loop(0, tm, start, 0, unroll=ROW_DMA_UNROLL)
    lax.fori_loop(0, tm, wait, 0, unroll=ROW_DMA_UNROLL)


def _expert_ffn_kernel(te_ref, used_ref, x_ref, w1_ref, w3_ref, w2_ref, o_ref, xb_ref, acc_ref):
    del te_ref
    j = pl.program_id(1)
    last = j == pl.num_programs(1) - 1
    in_use = pl.program_id(0) < used_ref[0]

    @pl.when(in_use)
    def _():
        @pl.when(j == 0)
        def _():
            xb_ref[...] = x_ref[...].astype(BF16)
            acc_ref[...] = jnp.zeros_like(acc_ref)

        acc_ref[...] += _swiglu_chunks(xb_ref[...], w1_ref, w3_ref, w2_ref)

        @pl.when(last)
        def _():
            o_ref[...] = acc_ref[...]

    @pl.when(jnp.logical_not(in_use) & last)
    def _():
        o_ref[...] = jnp.zeros_like(o_ref)


def _moe_combine_kernel(dest_ref, route_ref, x_ref, ys_ref, o_ref, buf_ref, sem):
    tm = x_ref.shape[0]

    def row_copy(r, slot):
        return pltpu.make_async_copy(ys_ref.at[pl.ds(dest_ref[TOP_K * r + slot], 1)], buf_ref.at[slot, pl.ds(r, 1)], sem)

    def start(r, c):
        for slot in range(TOP_K):
            row_copy(r, slot).start(priority=slot % 2)
        return c

    def wait(r, c):
        for slot in range(TOP_K):
            row_copy(r, slot).wait()
        return c

    lax.fori_loop(0, tm, start, 0, unroll=ROW_DMA_UNROLL)
    lax.fori_loop(0, tm, wait, 0, unroll=ROW_DMA_UNROLL)
    route = route_ref[...]
    out = x_ref[...]
    for slot in range(TOP_K):
        out = out + route[:, ROUTE_GATE + slot:ROUTE_GATE + slot + 1] * buf_ref[slot]
    o_ref[...] = out


def moe_ffn(x, ln, router, w13, w2, tf=1792):
    t, d = x.shape
    n_exp, dff = w2.shape[0], w2.shape[1]
    assert dff % tf == 0
    nj = dff // tf
    tmx = min(MOE_ROW_TILE, t)
    tm = _row_tile(t, 512)
    wr = jnp.pad(router, ((0, 0), (0, LANES - n_exp)))
    hn, route, cnt = pl.pallas_call(
        functools.partial(_router_kernel, n_exp=n_exp),
        grid=(t // tm,),
        in_specs=[pl.BlockSpec((tm, d), lambda i: (i, 0)),
                  pl.BlockSpec((1, d), lambda i: (0, 0)),
                  pl.BlockSpec((d, LANES), lambda i: (0, 0))],
        out_specs=[pl.BlockSpec((tm, d), lambda i: (i, 0)),
                   pl.BlockSpec((tm, LANES), lambda i: (i, 0)),
                   pl.BlockSpec((1, LANES), lambda i: (0, 0))],
        out_shape=[jax.ShapeDtypeStruct((t, d), F32),
                   jax.ShapeDtypeStruct((t, LANES), F32),
                   jax.ShapeDtypeStruct((1, LANES), F32)],
        scratch_shapes=[pltpu.VMEM((1, LANES), F32)],
        compiler_params=_params("arbitrary"),
        name="moe_router",
    )(x, ln.reshape(1, d), wr)

    expert = route[:, ROUTE_EXPERT:ROUTE_EXPERT + TOP_K].astype(I32)
    rank = route[:, ROUTE_RANK:ROUTE_RANK + TOP_K].astype(I32)
    counts = cnt[0, :n_exp].astype(I32)
    padded = (counts + tmx - 1) // tmx * tmx
    ends = jnp.cumsum(padded)
    dest = ((ends - padded)[expert] + rank).reshape(TOP_K * t)
    n_tiles = (TOP_K * t) // tmx + n_exp
    rows = n_tiles * tmx
    tile_start = jnp.arange(n_tiles, dtype=I32) * tmx
    tile_expert = jnp.minimum(jnp.sum((tile_start[:, None] >= ends[None, :]).astype(I32), axis=1), n_exp - 1)

    smem_rows = lambda n: pl.BlockSpec((TOP_K * n,), lambda i: (i,), memory_space=pltpu.SMEM)
    xs = pl.pallas_call(
        _dispatch_kernel,
        grid=(t // tm,),
        in_specs=[smem_rows(tm),
                  pl.BlockSpec((tm, d), lambda i: (i, 0)),
                  pl.BlockSpec(memory_space=pl.ANY)],
        out_specs=pl.BlockSpec(memory_space=pl.ANY),
        out_shape=jax.ShapeDtypeStruct((rows, d), F32),
        scratch_shapes=[pltpu.SemaphoreType.DMA(())],
        input_output_aliases={2: 0},
        compiler_params=_params("arbitrary"),
        name="moe_dispatch",
    )(dest, hn, jnp.zeros((rows, d), F32))

    ys = pl.pallas_call(
        _expert_ffn_kernel,
        grid_spec=pltpu.PrefetchScalarGridSpec(
            num_scalar_prefetch=2,
            grid=(n_tiles, nj),
            in_specs=[pl.BlockSpec((tmx, d), lambda i, j, te, nu: (i, 0)),
                      pl.BlockSpec((None, d, tf), lambda i, j, te, nu: (te[i], 0, j)),
                      pl.BlockSpec((None, d, tf), lambda i, j, te, nu: (te[i], 0, nj + j)),
                      pl.BlockSpec((None, tf, d), lambda i, j, te, nu: (te[i], j, 0))],
            out_specs=pl.BlockSpec((tmx, d), lambda i, j, te, nu: (i, 0)),
            scratch_shapes=[pltpu.VMEM((tmx, d), BF16), pltpu.VMEM((tmx, d), F32)]),
        out_shape=jax.ShapeDtypeStruct((rows, d), F32),
        compiler_params=_params("parallel", "arbitrary"),
        name="moe_expert_ffn",
    )(tile_expert, (ends[-1:] // tmx).astype(I32), xs, w13, w13, w2)

    return pl.pallas_call(
        _moe_combine_kernel,
        grid=(t // tm,),
        in_specs=[smem_rows(tm),
                  pl.BlockSpec((tm, LANES), lambda i: (i, 0)),
                  pl.BlockSpec((tm, d), lambda i: (i, 0)),
                  pl.BlockSpec(memory_space=pl.ANY)],
        out_specs=pl.BlockSpec((tm, d), lambda i: (i, 0)),
        out_shape=jax.ShapeDtypeStruct((t, d), F32),
        scratch_shapes=[pltpu.VMEM((TOP_K, tm, d), F32), pltpu.SemaphoreType.DMA(())],
        compiler_params=_params("arbitrary"),
        name="moe_combine",
    )(dest, route, x, ys)


def kernel(x, positions, ln_mix, ln_ffn, a_wqkv, a_qk_norm, a_wo, b_mu, b_wrkv, b_w0, b_w1, b_w2, b_a0, b_a1, b_a2, b_g1, b_g2, b_kk, b_ka, b_rk, b_lnx, b_wo, c_wa, c_qa_norm, c_kva_norm, c_wq_b, c_wkv_b, c_qk_norm, c_wo, f_w13, f_w2, m_router, m_w13, m_w2):
    batch, s_len, d = x.shape
    t = batch * s_len
    cs, sn = _rope_tables(positions)
    cs, sn = cs.reshape(t, LANES), sn.reshape(t, LANES)
    h = x.reshape(t, d)
    n_mixers = 3
    for i in range(ln_mix.shape[0]):
        j, kind = divmod(i, n_mixers)
        if kind == 0:
            h = mixer_a(h, ln_mix[i], a_wqkv[j], a_qk_norm[j], a_wo[j], cs, sn, batch, s_len)
        elif kind == 1:
            h = mixer_b(h, ln_mix[i], b_mu[j], b_wrkv[j], b_w0[j], b_w1[j], b_w2[j], b_a0[j], b_a1[j], b_a2[j],
                        b_g1[j], b_g2[j], b_kk[j], b_ka[j], b_rk[j], b_lnx[j], b_wo[j], batch, s_len)
        else:
            h = mixer_c(h, ln_mix[i], c_wa[j], c_qa_norm[j], c_kva_norm[j], c_wq_b[j], c_wkv_b[j], c_qk_norm[j],
                        c_wo[j], cs, sn, batch, s_len)
        f = i // 2
        if i % 2 == 0:
            h = dense_ffn(h, ln_ffn[i], f_w13[f].astype(BF16), f_w2[f].astype(BF16))
        else:
            h = moe_ffn(h, ln_ffn[i], m_router[f], m_w13[f].astype(BF16), m_w2[f].astype(BF16))
    return h.reshape(batch, s_len, d)
```
